```python
import jax, jax.numpy as jnp
from jax import lax
import numpy as np

D_MODEL = 4096
BATCH = 4
SEQ = 2048
DEPTH = 2
DEC_BATCH = 8
DEC_SEQ = 1
PAST_LEN = 16384
PAGE_SIZE = 128

HEAD_DIM = 128
FOX_HEADS = D_MODEL // HEAD_DIM
FOX_KV_HEADS = FOX_HEADS // 4
SB_HEADS = D_MODEL // HEAD_DIM
SB_KV_HEADS = SB_HEADS // 4
RWKV_HEAD = 64
RWKV_HEADS = D_MODEL // RWKV_HEAD
RWKV_DECAY_RANK = max(32, int(round(1.8 * D_MODEL ** 0.5 / 32)) * 32)
RWKV_ICLR_RANK = max(32, int(round(1.8 * D_MODEL ** 0.5 / 32)) * 32)
RWKV_GATE_RANK = max(32, int(round(0.6 * D_MODEL ** 0.8 / 32)) * 32)
RWKV_SHIFT_WIDTH = 3 * D_MODEL + RWKV_DECAY_RANK + RWKV_ICLR_RANK + RWKV_GATE_RANK
IN_SIZES = (FOX_HEADS * HEAD_DIM, FOX_KV_HEADS * HEAD_DIM, FOX_KV_HEADS * HEAD_DIM, FOX_HEADS,
            SB_HEADS * HEAD_DIM, SB_KV_HEADS * HEAD_DIM, SB_KV_HEADS * HEAD_DIM,
            RWKV_SHIFT_WIDTH, 3 * D_MODEL)
N_IN = sum(IN_SIZES)
PEER_KEYS = 128
PEER_EXPERTS = PEER_KEYS * PEER_KEYS
PEER_HEADS = 8
PEER_QUERY_DIM = 256
PEER_TOPK = 16
Q_BLOCK = 128
PEER_TOKEN_BLOCK = 128
NORM_EPS = 1e-6
RWKV_GN_EPS = 64e-5

kernel_name = 'fox_rwkv7_stickbreak_peer_hybrid_step'


def _split(a, sizes):
    offs = [sum(sizes[:i + 1]) for i in range(len(sizes) - 1)]
    return jnp.split(a, offs, axis=-1)


def _rms_norm(x, g):
    xf = x.astype(jnp.float32)
    y = xf * lax.rsqrt(jnp.mean(xf * xf, axis=-1, keepdims=True) + NORM_EPS)
    return (y * g.astype(jnp.float32)).astype(x.dtype)


def _head_rms(x, g):
    return _rms_norm(x, g)


def _gather_pages(pool, page_table):
    rows = pool[page_table]
    return rows.reshape(page_table.shape[0], -1, *pool.shape[2:])


def _sweep_query_blocks(fn, q_arrays, q_pos):
    tq = q_pos.shape[0]
    blk = min(Q_BLOCK, tq)
    pad = (-tq) % blk
    nb = (tq + pad) // blk

    def to_blocks(a):
        a = jnp.pad(a, [(0, 0), (0, pad)] + [(0, 0)] * (a.ndim - 2))
        return jnp.moveaxis(a.reshape(a.shape[0], nb, blk, *a.shape[2:]), 1, 0)

    pos = jnp.pad(q_pos, (0, pad), mode='edge').reshape(nb, blk)
    out = lax.map(lambda args: fn(*args), (*[to_blocks(a) for a in q_arrays], pos))
    out = jnp.moveaxis(out, 0, 1)
    return out.reshape(out.shape[0], nb * blk, *out.shape[3:])[:, :tq]


def _fox_attention(q, k, v, cq, ck, q_pos, k_pos):
    b, _, h, dh = q.shape
    hkv = k.shape[2]
    g = h // hkv
    scale = dh ** -0.5
    ck_t = jnp.moveaxis(ck.reshape(b, -1, hkv, g), 1, -1)

    def block(qb, cqb, pb):
        s = jnp.einsum('bqhgd,bkhd->bhgqk', qb.reshape(b, -1, hkv, g, dh), k,
                       preferred_element_type=jnp.float32) * scale
        cq_t = jnp.moveaxis(cqb.reshape(b, -1, hkv, g), 1, -1)
        s = s + cq_t[..., None] - ck_t[..., None, :]
        s = jnp.where(k_pos[None, :] <= pb[:, None], s, -jnp.inf)
        pr = jax.nn.softmax(s, axis=-1).astype(v.dtype)
        o = jnp.einsum('bhgqk,bkhd->bqhgd', pr, v)
        return o.reshape(b, -1, h * dh)

    return _sweep_query_blocks(block, (q, cq), q_pos)


def _stick_breaking_attention(q, k, v, q_pos, k_pos):
    b, _, h, dh = q.shape
    hkv = k.shape[2]
    g = h // hkv
    scale = dh ** -0.5

    def block(qb, pb):
        z = jnp.einsum('bqhgd,bkhd->bhgqk', qb.reshape(b, -1, hkv, g, dh), k,
                       preferred_element_type=jnp.float32) * scale
        mask = k_pos[None, :] < pb[:, None]
        log_keep = jnp.where(mask, jax.nn.log_sigmoid(-z), 0.0)
        log_keep_later = lax.cumsum(log_keep, axis=z.ndim - 1, reverse=True) - log_keep
        wts = jnp.where(mask, jnp.exp(jax.nn.log_sigmoid(z) + log_keep_later), 0.0).astype(v.dtype)
        o = jnp.einsum('bhgqk,bkhd->bqhgd', wts, v)
        return o.reshape(b, -1, h * dh)

    return _sweep_query_blocks(block, (q,), q_pos)


def _rwkv7_scan(r, w, k, v, a, bvec, s0):
    def step(s, inp):
        r_t, w_t, k_t, v_t, a_t, b_t = inp
        sa = jnp.einsum('bhij,bhj->bhi', s, a_t)
        s = s * w_t[:, :, None, :] + sa[..., None] * b_t[:, :, None, :] + v_t[..., None] * k_t[:, :, None, :]
        return s, jnp.einsum('bhij,bhj->bhi', s, r_t)
    xs = tuple(jnp.moveaxis(t.astype(jnp.float32), 1, 0) for t in (r, w, k, v, a, bvec))
    s_fin, ys = lax.scan(step, s0.astype(jnp.float32), xs)
    return s_fin, jnp.moveaxis(ys, 0, 1)


def _rwkv7_time_mix(zr, z_first, s0, p):
    b, t, _ = zr.shape
    f32 = jnp.float32
    z_prev = jnp.concatenate([z_first.astype(zr.dtype), zr[:, :-1]], axis=1)
    zs = zr + (z_prev - zr) * p['rwkv_mu']
    r, k, v, zw, za, zg = _split(zs, (D_MODEL, D_MODEL, D_MODEL, RWKV_DECAY_RANK, RWKV_ICLR_RANK, RWKV_GATE_RANK))
    w_log = -jax.nn.softplus(-(p['rwkv_w0'] + jnp.tanh(zw) @ p['rwkv_w2']).astype(f32)) - 0.5
    decay = jnp.exp(-jnp.exp(w_log))
    a = jax.nn.sigmoid((p['rwkv_a0'] + za @ p['rwkv_a2']).astype(f32))
    g = jax.nn.sigmoid(zg) @ p['rwkv_g2']

    def heads(x):
        return x.reshape(b, t, RWKV_HEADS, RWKV_HEAD)

    kk = heads(k.astype(f32) * p['rwkv_k_k'])
    kk = kk / jnp.maximum(jnp.linalg.norm(kk, axis=-1, keepdims=True), 1e-12)
    k_mod = k.astype(f32) * (1.0 + (a - 1.0) * p['rwkv_k_a'])
    r_h, k_h, v_h = heads(r.astype(f32)), heads(k_mod), heads(v.astype(f32))
    s_new, y = _rwkv7_scan(r_h, heads(decay), k_h, v_h, -kk, kk * heads(a), s0)
    mu = jnp.mean(y, axis=-1, keepdims=True)
    var = jnp.mean(jnp.square(y - mu), axis=-1, keepdims=True)
    y = ((y - mu) * lax.rsqrt(var + RWKV_GN_EPS)).reshape(b, t, D_MODEL) * p['rwkv_ln_w'] + p['rwkv_ln_b']
    bonus = jnp.sum(r_h * k_h * p['rwkv_r_k'], axis=-1, keepdims=True) * v_h
    y = (y + bonus.reshape(b, t, D_MODEL)) * g
    return y.astype(zr.dtype), s_new, zr[:, -1]


def _token_mixers(h, p, past):
    b, t, _ = h.shape
    proj = jnp.einsum('btd,dn->btn', h, p['w_in'])
    fq, fk, fv, ff, sq, sk, sv, zr, gates = _split(proj, IN_SIZES)
    fq = _head_rms(fq.reshape(b, t, FOX_HEADS, HEAD_DIM), p['fox_q_norm'])
    fk = _head_rms(fk.reshape(b, t, FOX_KV_HEADS, HEAD_DIM), p['fox_k_norm'])
    fv = fv.reshape(b, t, FOX_KV_HEADS, HEAD_DIM)
    logf = jax.nn.log_sigmoid((ff + p['b_forget']).astype(jnp.float32))
    sq = _head_rms(sq.reshape(b, t, SB_HEADS, HEAD_DIM), p['sb_q_norm'])
    sk = _head_rms(sk.reshape(b, t, SB_KV_HEADS, HEAD_DIM), p['sb_k_norm'])
    sv = sv.reshape(b, t, SB_KV_HEADS, HEAD_DIM)
    if past is None:
        n_past = 0
        fk_all, fv_all, logf_all, sk_all, sv_all = fk, fv, logf, sk, sv
        s0 = jnp.zeros((b, RWKV_HEADS, RWKV_HEAD, RWKV_HEAD), jnp.float32)
        z_first = jnp.zeros_like(zr[:, :1])
    else:
        n_past = past['fox_k'].shape[1]
        fk_all = jnp.concatenate([past['fox_k'], fk], axis=1)
        fv_all = jnp.concatenate([past['fox_v'], fv], axis=1)
        logf_all = jnp.concatenate([past['fox_logf'].astype(jnp.float32), logf], axis=1)
        sk_all = jnp.concatenate([past['sb_k'], sk], axis=1)
        sv_all = jnp.concatenate([past['sb_v'], sv], axis=1)
        s0 = past['rwkv_state']
        z_first = past['rwkv_shift'][:, None]
    k_pos = jnp.arange(n_past + t)
    q_pos = n_past + jnp.arange(t)
    c = jnp.cumsum(logf_all, axis=1)
    o_fox = _fox_attention(fq, fk_all, fv_all, c[:, n_past:], c, q_pos, k_pos)
    o_sb = _stick_breaking_attention(sq, sk_all, sv_all, q_pos, k_pos)
    o_rwkv, s_new, shift_new = _rwkv7_time_mix(zr, z_first, s0, p)
    g_fox, g_rwkv, g_sb = jnp.split(jax.nn.sigmoid(gates), 3, axis=-1)
    merged = g_fox * o_fox + g_rwkv * o_rwkv + g_sb * o_sb
    out = jnp.einsum('btd,de->bte', merged, p['w_out'])
    return out, (fk, fv, logf, sk, sv, s_new, shift_new)


def _peer(h, wq, subkeys, u_tab, v_tab):
    b, t, d = h.shape
    x = h.reshape(b * t, d)
    n = x.shape[0]
    q = (x @ wq).reshape(n, PEER_HEADS, 2, PEER_QUERY_DIM // 2)
    scores = jnp.einsum('nhpc,hpkc->nhpk', q, subkeys, preferred_element_type=jnp.float32)
    top_s, top_i = lax.top_k(scores, PEER_TOPK)
    cand_s = top_s[:, :, 0, :, None] + top_s[:, :, 1, None, :]
    cand_id = top_i[:, :, 0, :, None] * PEER_KEYS + top_i[:, :, 1, None, :]
    best_s, best_j = lax.top_k(cand_s.reshape(n, PEER_HEADS, PEER_TOPK * PEER_TOPK), PEER_TOPK)
    expert = jnp.take_along_axis(cand_id.reshape(n, PEER_HEADS, PEER_TOPK * PEER_TOPK), best_j, axis=-1)
    gate = jax.nn.softmax(best_s, axis=-1)
    expert = expert.reshape(n, PEER_HEADS * PEER_TOPK)
    gate = gate.reshape(n, PEER_HEADS * PEER_TOPK)
    blk = min(PEER_TOKEN_BLOCK, n)
    pad = (-n) % blk
    nb = (n + pad) // blk

    def to_blocks(a):
        return jnp.pad(a, [(0, pad)] + [(0, 0)] * (a.ndim - 1)).reshape(nb, blk, *a.shape[1:])

    def block(args):
        xb, eb, gb = args
        act = jax.nn.gelu(jnp.einsum('ted,td->te', u_tab[eb], xb, preferred_element_type=jnp.float32),
                          approximate=False)
        coef = (gb * act).astype(v_tab.dtype)
        return jnp.einsum('te,ted->td', coef, v_tab[eb])

    y = lax.map(block, (to_blocks(x), to_blocks(expert), to_blocks(gate)))
    return y.reshape(nb * blk, d)[:n].reshape(b, t, d)


def setup_inputs(seed: int = 0) -> dict:
    key = jax.random.key(seed)
    keys = iter(jax.random.split(key, 64))
    f32 = jnp.float32
    n_pages = PAST_LEN // PAGE_SIZE
    n_used = DEC_BATCH * n_pages
    n_pool = n_used + max(1, n_used // 4)

    def normal(shape, scale=1.0):
        return scale * jax.random.normal(next(keys), shape, f32)

    def uniform(shape, lo, hi):
        return jax.random.uniform(next(keys), shape, f32, lo, hi)

    fox_kv_shape = (DEPTH, n_pool, PAGE_SIZE, FOX_KV_HEADS, HEAD_DIM)
    sb_kv_shape = (DEPTH, n_pool, PAGE_SIZE, SB_KV_HEADS, HEAD_DIM)
    inputs = {
        'x_prompt': normal((BATCH, SEQ, D_MODEL)),
        'x_sample': normal((DEC_BATCH, DEC_SEQ, D_MODEL)),
        'cache_fox_k': normal(fox_kv_shape),
        'cache_fox_v': normal(fox_kv_shape),
        'cache_fox_logf': jax.nn.log_sigmoid(uniform((DEPTH, n_pool, PAGE_SIZE, FOX_HEADS), 1.0, 6.0)),
        'cache_sb_k': normal(sb_kv_shape),
        'cache_sb_v': normal(sb_kv_shape),
        'state_rwkv': normal((DEPTH, DEC_BATCH, RWKV_HEADS, RWKV_HEAD, RWKV_HEAD), 0.3),
        'state_rwkv_shift': normal((DEPTH, DEC_BATCH, RWKV_SHIFT_WIDTH)),
        'page_table': jax.random.permutation(next(keys), n_pool)[:n_used].reshape(DEC_BATCH, n_pages).astype(jnp.int32),
        'norm_mix': 1.0 + normal((DEPTH, D_MODEL), 0.02),
        'w_in': normal((DEPTH, D_MODEL, N_IN), D_MODEL ** -0.5),
        'b_forget': uniform((DEPTH, FOX_HEADS), 1.0, 6.0),
        'fox_q_norm': 1.0 + normal((DEPTH, HEAD_DIM), 0.02),
        'fox_k_norm': 1.0 + normal((DEPTH, HEAD_DIM), 0.02),
        'sb_q_norm': 1.0 + normal((DEPTH, HEAD_DIM), 0.02),
        'sb_k_norm': 1.0 + normal((DEPTH, HEAD_DIM), 0.02),
        'rwkv_mu': uniform((DEPTH, RWKV_SHIFT_WIDTH), 0.1, 0.9),
        'rwkv_w0': uniform((DEPTH, D_MODEL), -2.0, 1.0),
        'rwkv_w2': normal((DEPTH, RWKV_DECAY_RANK, D_MODEL), 0.5 * RWKV_DECAY_RANK ** -0.5),
        'rwkv_a0': normal((DEPTH, D_MODEL), 0.1),
        'rwkv_a2': normal((DEPTH, RWKV_ICLR_RANK, D_MODEL), RWKV_ICLR_RANK ** -0.5),
        'rwkv_g2': normal((DEPTH, RWKV_GATE_RANK, D_MODEL), RWKV_GATE_RANK ** -0.5),
        'rwkv_k_k': 0.85 + normal((DEPTH, D_MODEL), 0.05),
        'rwkv_k_a': 1.0 + normal((DEPTH, D_MODEL), 0.05),
        'rwkv_r_k': normal((DEPTH, RWKV_HEADS, RWKV_HEAD), 0.1),
        'rwkv_ln_w': 1.0 + normal((DEPTH, D_MODEL), 0.02),
        'rwkv_ln_b': normal((DEPTH, D_MODEL), 0.02),
        'w_out': normal((DEPTH, D_MODEL, D_MODEL), D_MODEL ** -0.5),
        'norm_ffn': 1.0 + normal((DEPTH, D_MODEL), 0.02),
        'peer_wq': normal((DEPTH, D_MODEL, PEER_HEADS * PEER_QUERY_DIM), D_MODEL ** -0.5),
        'peer_subkeys': normal((DEPTH, PEER_HEADS, 2, PEER_KEYS, PEER_QUERY_DIM // 2), (PEER_QUERY_DIM // 2) ** -0.5),
        'peer_u': normal((DEPTH, PEER_EXPERTS, D_MODEL), D_MODEL ** -0.5),
        'peer_v': normal((DEPTH, PEER_EXPERTS, D_MODEL), PEER_HEADS ** -0.5),
    }
    return inputs


def reference(x_prompt, x_sample, cache_fox_k, cache_fox_v, cache_fox_logf, cache_sb_k, cache_sb_v,
              state_rwkv, state_rwkv_shift, page_table, norm_mix, w_in, b_forget, fox_q_norm, fox_k_norm,
              sb_q_norm, sb_k_norm, rwkv_mu, rwkv_w0, rwkv_w2, rwkv_a0, rwkv_a2, rwkv_g2, rwkv_k_k, rwkv_k_a,
              rwkv_r_k, rwkv_ln_w, rwkv_ln_b, w_out, norm_ffn, peer_wq, peer_subkeys, peer_u, peer_v):
    xp, xs = x_prompt, x_sample
    prompt_states, sample_states = [], []
    for l in range(DEPTH):
        mp = dict(w_in=w_in[l], b_forget=b_forget[l], fox_q_norm=fox_q_norm[l], fox_k_norm=fox_k_norm[l],
                  sb_q_norm=sb_q_norm[l], sb_k_norm=sb_k_norm[l], rwkv_mu=rwkv_mu[l], rwkv_w0=rwkv_w0[l],
                  rwkv_w2=rwkv_w2[l], rwkv_a0=rwkv_a0[l], rwkv_a2=rwkv_a2[l], rwkv_g2=rwkv_g2[l],
                  rwkv_k_k=rwkv_k_k[l], rwkv_k_a=rwkv_k_a[l], rwkv_r_k=rwkv_r_k[l], rwkv_ln_w=rwkv_ln_w[l],
                  rwkv_ln_b=rwkv_ln_b[l], w_out=w_out[l])
        out, st = _token_mixers(_rms_norm(xp, norm_mix[l]), mp, None)
        xp = xp + out
        xp = xp + _peer(_rms_norm(xp, norm_ffn[l]), peer_wq[l], peer_subkeys[l], peer_u[l], peer_v[l])
        prompt_states.append(st)
        past = dict(fox_k=_gather_pages(cache_fox_k[l], page_table),
                    fox_v=_gather_pages(cache_fox_v[l], page_table),
                    fox_logf=_gather_pages(cache_fox_logf[l], page_table),
                    sb_k=_gather_pages(cache_sb_k[l], page_table),
                    sb_v=_gather_pages(cache_sb_v[l], page_table),
                    rwkv_state=state_rwkv[l], rwkv_shift=state_rwkv_shift[l])
        out, st = _token_mixers(_rms_norm(xs, norm_mix[l]), mp, past)
        xs = xs + out
        xs = xs + _peer(_rms_norm(xs, norm_ffn[l]), peer_wq[l], peer_subkeys[l], peer_u[l], peer_v[l])
        sample_states.append(st)

    def stack(states, i):
        return jnp.stack([s[i] for s in states])

    fox_k_p, fox_v_p, fox_logf_p = stack(prompt_states, 0), stack(prompt_states, 1), stack(prompt_states, 2)
    sb_k_p, sb_v_p = stack(prompt_states, 3), stack(prompt_states, 4)
    rwkv_state_p, rwkv_shift_p = stack(prompt_states, 5), stack(prompt_states, 6)
    fox_k_s, fox_v_s, fox_logf_s = stack(sample_states, 0), stack(sample_states, 1), stack(sample_states, 2)
    sb_k_s, sb_v_s = stack(sample_states, 3), stack(sample_states, 4)
    rwkv_state_s, rwkv_shift_s = stack(sample_states, 5), stack(sample_states, 6)
    return (xp, xs, fox_k_p, fox_v_p, fox_logf_p, sb_k_p, sb_v_p, rwkv_state_p, rwkv_shift_p,
            fox_k_s, fox_v_s, fox_logf_s, sb_k_s, sb_v_s, rwkv_state_s, rwkv_shift_s)
```

```python
import functools

import numpy as np
import jax
import jax.numpy as jnp
from jax import lax
from jax.experimental import pallas as pl
from jax.experimental.pallas import tpu as pltpu

F32 = jnp.float32
BF16 = jnp.bfloat16
HIGHEST = lax.Precision.HIGHEST

NORM_EPS = 1e-6
RWKV_GN_EPS = 64e-5
LANES = 128
SUBLANES = 8
HEAD_DIM = 128
RWKV_HEAD = 64
GQA = 4
PEER_TOPK = 16
VMEM_LIMIT_BYTES = 56 * 1024 * 1024

MODE_ID, MODE_RMS, MODE_SIGMOID, MODE_LOGSIG = 0, 1, 2, 3


def _params(*sem):
    return pltpu.CompilerParams(dimension_semantics=sem, vmem_limit_bytes=VMEM_LIMIT_BYTES)


def _round_up(n, m):
    return (n + m - 1) // m * m


def _pick(n, cands):
    for c in cands:
        if c <= n and n % c == 0:
            return c
    raise ValueError(f"no tile for {n} in {cands}")


def _sigmoid(x):
    return 1.0 / (1.0 + jnp.exp(-x))


def _log_sigmoid(x):
    return jnp.minimum(x, 0.0) - jnp.log1p(jnp.exp(-jnp.abs(x)))


def _rmsnorm_kernel(x_ref, g_ref, o_ref):
    x = x_ref[...]
    ms = jnp.mean(x * x, axis=-1, keepdims=True)
    o_ref[...] = (x * lax.rsqrt(ms + NORM_EPS) * g_ref[...]).astype(o_ref.dtype)


def _rmsnorm(x, g, tm):
    n, d = x.shape
    return pl.pallas_call(
        _rmsnorm_kernel,
        out_shape=jax.ShapeDtypeStruct((n, d), BF16),
        grid=(n // tm,),
        in_specs=[pl.BlockSpec((tm, d), lambda i: (i, 0)), pl.BlockSpec((1, d), lambda i: (0, 0))],
        out_specs=pl.BlockSpec((tm, d), lambda i: (i, 0)),
        compiler_params=_params("parallel"),
    )(x, g.reshape(1, d))


def _inproj_kernel(modes_ref, h_ref, w_ref, cp_ref, o_ref):
    acc = jnp.dot(h_ref[...], w_ref[...], preferred_element_type=F32)
    mode = modes_ref[pl.program_id(0)]

    @pl.when(mode == MODE_ID)
    def _():
        o_ref[...] = acc

    @pl.when(mode == MODE_RMS)
    def _():
        for c in range(acc.shape[1] // HEAD_DIM):
            sl = slice(c * HEAD_DIM, (c + 1) * HEAD_DIM)
            xs = acc[:, sl]
            ms = jnp.mean(xs * xs, axis=-1, keepdims=True)
            o_ref[:, sl] = xs * lax.rsqrt(ms + NORM_EPS) * cp_ref[:, sl]

    @pl.when(mode == MODE_SIGMOID)
    def _():
        o_ref[...] = _sigmoid(acc)

    @pl.when(mode == MODE_LOGSIG)
    def _():
        o_ref[...] = _log_sigmoid(acc + cp_ref[...])


def _inproj(h, w, colp, modes, tm, tn):
    n, d = h.shape
    ncol = w.shape[1]
    return pl.pallas_call(
        _inproj_kernel,
        out_shape=jax.ShapeDtypeStruct((n, ncol), F32),
        grid_spec=pltpu.PrefetchScalarGridSpec(
            num_scalar_prefetch=1,
            grid=(ncol // tn, n // tm),
            in_specs=[
                pl.BlockSpec((tm, d), lambda j, i, m: (i, 0)),
                pl.BlockSpec((d, tn), lambda j, i, m: (0, j)),
                pl.BlockSpec((1, tn), lambda j, i, m: (0, j)),
            ],
            out_specs=pl.BlockSpec((tm, tn), lambda j, i, m: (i, j)),
        ),
        compiler_params=_params("parallel", "parallel"),
    )(modes, h, w, colp)


class _Layout:
    def __init__(self, d, hq, rd, ri, rg, tn):
        dkv = d // GQA
        zw = 3 * d + rd + ri + rg
        src = {}
        o = 0
        for name, width in (("fq", d), ("fk", dkv), ("fv", dkv), ("ff", hq), ("sq", d), ("sk", dkv), ("sv", dkv)):
            src[name] = (o, width)
            o += width
        for name, width in (("r", d), ("k", d), ("v", d), ("zw", rd), ("za", ri), ("zg", rg)):
            src[name] = (o, width)
            o += width
        src["gates"] = (o, 3 * d)
        self.n_in = o + 3 * d
        self.zr_width = zw
        order = (("fq", MODE_RMS), ("sq", MODE_RMS), ("fk", MODE_RMS), ("fv", MODE_ID), ("sk", MODE_RMS),
                 ("sv", MODE_ID), ("r", MODE_ID), ("k", MODE_ID), ("v", MODE_ID), ("zw", MODE_ID),
                 ("za", MODE_ID), ("zg", MODE_ID), ("gates", MODE_SIGMOID), ("ff", MODE_LOGSIG))
        self.src = src
        self.off = {}
        self.pieces = []
        seg_modes = []
        o, prev = 0, None
        for name, mode in order:
            o = _round_up(o, tn if mode != prev else _round_up(min(src[name][1], 4 * LANES), LANES))
            self.off[name] = o
            seg_modes.append((o, mode))
            self.pieces.append((o, src[name][0], src[name][1]))
            o += src[name][1]
            prev = mode
        self.ncol = _round_up(o, tn)
        self.z_lo = self.off["r"]
        self.z_width = _round_up(self.off["zg"] + rg, LANES) - self.z_lo
        modes = np.zeros((self.ncol // tn,), np.int32)
        for t in range(self.ncol // tn):
            for so, m in seg_modes:
                if so <= t * tn:
                    modes[t] = m
        self.modes = modes

    def scatter_cols(self, a, width):
        parts, o = [], 0
        for dst, s, w in self.pieces:
            if dst > o:
                parts.append(jnp.zeros(a.shape[:-1] + (dst - o,), a.dtype))
            parts.append(a[..., s:s + w])
            o = dst + w
        if width > o:
            parts.append(jnp.zeros(a.shape[:-1] + (width - o,), a.dtype))
        return jnp.concatenate(parts, axis=-1)


def _cumsum_kernel(x_ref, o_ref, *, cb):
    t = x_ref.shape[0]
    row = lax.broadcasted_iota(jnp.int32, (cb, cb), 0)
    col = lax.broadcasted_iota(jnp.int32, (cb, cb), 1)
    tri = (row >= col).astype(F32)
    carry = jnp.zeros((1, x_ref.shape[1]), F32)
    for i in range(t // cb):
        c = jnp.dot(tri, x_ref[i * cb:(i + 1) * cb, :], precision=HIGHEST, preferred_element_type=F32) + carry
        o_ref[i * cb:(i + 1) * cb, :] = c
        carry = c[cb - 1:cb, :]


def _cumsum_time(p, b, t, col_block):
    cb = _pick(t, (256, 128, 64, 32, 16, 8))
    return pl.pallas_call(
        functools.partial(_cumsum_kernel, cb=cb),
        out_shape=jax.ShapeDtypeStruct((b * t, LANES), F32),
        grid=(b,),
        in_specs=[pl.BlockSpec((t, LANES), lambda i: (i, col_block))],
        out_specs=pl.BlockSpec((t, LANES), lambda i: (i, 0)),
        compiler_params=_params("parallel"),
    )(p)


def _stack_heads(q_ref, scale):
    tq = q_ref.shape[0]
    q = q_ref[...] * scale
    return jnp.concatenate([q[:, h * HEAD_DIM:(h + 1) * HEAD_DIM] for h in range(GQA)], axis=0).astype(BF16)


def _unstack_heads(o, o_ref):
    tq = o_ref.shape[0]
    for h in range(GQA):
        o_ref[:, h * HEAD_DIM:(h + 1) * HEAD_DIM] = o[h * tq:(h + 1) * tq, :]


def _fox_prompt_kernel(q_ref, k_ref, v_ref, cq_ref, ck_ref, o_ref, *, tq):
    qi = pl.program_id(2)
    rows = GQA * tq
    qs = _stack_heads(q_ref, HEAD_DIM ** -0.5)
    cq = cq_ref[0].reshape(rows, 1)

    def scores(ki):
        k = k_ref[pl.ds(pl.multiple_of(ki * tq, tq), tq), :].astype(BF16)
        s = lax.dot_general(qs, k, (((1,), (1,)), ((), ())), preferred_element_type=F32)
        ck = ck_ref[0, :, ki]
        return ((s + cq).reshape(GQA, tq, tq) - ck).reshape(rows, tq)

    def update(ki, s, carry):
        m, l, acc = carry
        m_new = jnp.maximum(m, jnp.max(s, axis=-1, keepdims=True))
        alpha = jnp.exp(m - m_new)
        p = jnp.exp(s - m_new)
        v = v_ref[pl.ds(pl.multiple_of(ki * tq, tq), tq), :].astype(BF16)
        acc = acc * alpha + jnp.dot(p.astype(BF16), v, preferred_element_type=F32)
        return m_new, l * alpha + jnp.sum(p, axis=-1, keepdims=True), acc

    init = (jnp.full((rows, 1), -jnp.inf, F32), jnp.zeros((rows, 1), F32), jnp.zeros((rows, HEAD_DIM), F32))
    carry = lax.fori_loop(0, qi, lambda ki, c: update(ki, scores(ki), c), init)
    r = lax.broadcasted_iota(jnp.int32, (GQA, tq, tq), 1)
    c = lax.broadcasted_iota(jnp.int32, (GQA, tq, tq), 2)
    s = jnp.where((c <= r).reshape(rows, tq), scores(qi), -jnp.inf)
    m, l, acc = update(qi, s, carry)
    _unstack_heads(acc / l, o_ref)


def _sb_prompt_kernel(q_ref, k_ref, v_ref, o_ref, *, tq):
    qi = pl.program_id(2)
    rows = GQA * tq
    qs = _stack_heads(q_ref, HEAD_DIM ** -0.5)
    jr = lax.broadcasted_iota(jnp.int32, (tq, tq), 0)
    sc = lax.broadcasted_iota(jnp.int32, (tq, tq), 1)
    later = (jr > sc).astype(F32)

    def block(ki, carry, mask):
        run, acc = carry
        k = k_ref[pl.ds(pl.multiple_of(ki * tq, tq), tq), :].astype(BF16)
        v = v_ref[pl.ds(pl.multiple_of(ki * tq, tq), tq), :].astype(BF16)
        z = lax.dot_general(qs, k, (((1,), (1,)), ((), ())), preferred_element_type=F32)
        log_keep = -(jnp.maximum(z, 0.0) + jnp.log1p(jnp.exp(-jnp.abs(z))))
        if mask is not None:
            log_keep = jnp.where(mask, log_keep, 0.0)
        after = jnp.dot(log_keep, later, precision=HIGHEST, preferred_element_type=F32) + run
        w = jnp.exp(z + log_keep + after)
        if mask is not None:
            w = jnp.where(mask, w, 0.0)
        acc = acc + jnp.dot(w.astype(BF16), v, preferred_element_type=F32)
        return run + jnp.sum(log_keep, axis=-1, keepdims=True), acc

    r3 = lax.broadcasted_iota(jnp.int32, (GQA, tq, tq), 1)
    c3 = lax.broadcasted_iota(jnp.int32, (GQA, tq, tq), 2)
    carry = block(qi, (jnp.zeros((rows, 1), F32), jnp.zeros((rows, HEAD_DIM), F32)), (c3 < r3).reshape(rows, tq))
    run, acc = lax.fori_loop(0, qi, lambda it, c: block(qi - 1 - it, c, None), carry)
    _unstack_heads(acc, o_ref)


def _prompt_attention(p, lay, b, t, d, cq=None, ck=None):
    hkv = d // (GQA * HEAD_DIM)
    tq = _pick(t, (256, 128))
    nq = t // tq
    gw = GQA * HEAD_DIM
    fox = cq is not None
    qn, kn, vn = ("fq", "fk", "fv") if fox else ("sq", "sk", "sv")
    qb, kb, vb = lay.off[qn] // gw, lay.off[kn] // HEAD_DIM, lay.off[vn] // HEAD_DIM
    in_specs = [
        pl.BlockSpec((tq, gw), lambda bi, g, qi: (bi * nq + qi, qb + g)),
        pl.BlockSpec((t, HEAD_DIM), lambda bi, g, qi: (bi, kb + g)),
        pl.BlockSpec((t, HEAD_DIM), lambda bi, g, qi: (bi, vb + g)),
    ]
    args = [p, p, p]
    if fox:
        in_specs += [
            pl.BlockSpec((1, GQA, tq, 1), lambda bi, g, qi: (bi, g, qi, 0)),
            pl.BlockSpec((1, GQA, nq, 1, tq), lambda bi, g, qi: (bi, g, 0, 0, 0)),
        ]
        args += [cq, ck]
        body = functools.partial(_fox_prompt_kernel, tq=tq)
    else:
        body = functools.partial(_sb_prompt_kernel, tq=tq)
    return pl.pallas_call(
        body,
        out_shape=jax.ShapeDtypeStruct((b * t, d), F32),
        grid=(b, hkv, nq),
        in_specs=in_specs,
        out_specs=pl.BlockSpec((tq, gw), lambda bi, g, qi: (bi * nq + qi, g)),
        compiler_params=_params("parallel", "parallel", "arbitrary"),
    )(*args)


DEC_ROWS = 2 * SUBLANES


def _decode_kernel(pt_ref, q_ref, kc_ref, vc_ref, *rest, fox, hkv):
    if fox:
        lf_ref, knew_ref, vnew_ref, lfnew_ref, o_ref, m_ref, l_ref, run_ref, acc_ref = rest
    else:
        o_ref, run_ref, acc_ref = rest
    pg = pl.program_id(1)
    npg = pl.num_programs(1)
    page = kc_ref.shape[2]
    rows = hkv * DEC_ROWS

    @pl.when(pg == 0)
    def _():
        if fox:
            for g in range(hkv):
                sl = slice(g * DEC_ROWS, (g + 1) * DEC_ROWS)
                qg = q_ref[0, sl, :].astype(BF16).astype(F32)
                kg = knew_ref[0, g:g + 1, :].astype(BF16).astype(F32)
                m_ref[sl, :] = jnp.sum(qg * kg, axis=-1, keepdims=True)
                acc_ref[sl, :] = jnp.broadcast_to(vnew_ref[0, g:g + 1, :].astype(BF16).astype(F32), (DEC_ROWS, HEAD_DIM))
            l_ref[...] = jnp.ones_like(l_ref)
            run_ref[...] = lfnew_ref[0]
        else:
            run_ref[...] = jnp.zeros_like(run_ref)
            acc_ref[...] = jnp.zeros_like(acc_ref)

    jr = lax.broadcasted_iota(jnp.int32, (page, page), 0)
    sc = lax.broadcasted_iota(jnp.int32, (page, page), 1)
    later = (jr > sc).astype(F32)
    z = jnp.concatenate(
        [lax.dot_general(q_ref[0, g * DEC_ROWS:(g + 1) * DEC_ROWS, :].astype(BF16), kc_ref[0, 0, :, g, :].astype(BF16),
                         (((1,), (1,)), ((), ())), preferred_element_type=F32) for g in range(hkv)], axis=0)
    run = run_ref[...]
    if fox:
        lf = lf_ref[0, 0]
        s = z + run + jnp.dot(lf, later, precision=HIGHEST, preferred_element_type=F32)
        m = m_ref[...]
        m_new = jnp.maximum(m, jnp.max(s, axis=-1, keepdims=True))
        alpha = jnp.exp(m - m_new)
        w = jnp.exp(s - m_new)
        l_ref[...] = l_ref[...] * alpha + jnp.sum(w, axis=-1, keepdims=True)
        m_ref[...] = m_new
        run_ref[...] = run + jnp.sum(lf, axis=-1, keepdims=True)
    else:
        log_keep = -(jnp.maximum(z, 0.0) + jnp.log1p(jnp.exp(-jnp.abs(z))))
        after = jnp.dot(log_keep, later, precision=HIGHEST, preferred_element_type=F32) + run
        w = jnp.exp(z + log_keep + after)
        alpha = None
        run_ref[...] = run + jnp.sum(log_keep, axis=-1, keepdims=True)
    wb = w.astype(BF16)
    for g in range(hkv):
        sl = slice(g * DEC_ROWS, (g + 1) * DEC_ROWS)
        pv = jnp.dot(wb[sl, :], vc_ref[0, 0, :, g, :].astype(BF16), preferred_element_type=F32)
        if fox:
            acc_ref[sl, :] = acc_ref[sl, :] * alpha[sl, :] + pv
        else:
            acc_ref[sl, :] = acc_ref[sl, :] + pv

    @pl.when(pg == npg - 1)
    def _():
        if fox:
            o_ref[0] = acc_ref[...] / l_ref[...]
        else:
            o_ref[0] = acc_ref[...]


def _decode_attention(layer, page_table, q, kcache, vcache, lf_cache=None, knew=None, vnew=None, lfnew=None):
    bs, npg = page_table.shape
    _, _, page, hkv, _ = kcache.shape
    rows = hkv * DEC_ROWS
    fox = lf_cache is not None
    cache_spec = pl.BlockSpec((1, 1, page, hkv, HEAD_DIM), lambda b, p, pt: (layer, pt[b, npg - 1 - p], 0, 0, 0))
    in_specs = [pl.BlockSpec((1, rows, HEAD_DIM), lambda b, p, pt: (b, 0, 0)), cache_spec, cache_spec]
    args = [q, kcache, vcache]
    scratch = [pltpu.VMEM((rows, 1), F32), pltpu.VMEM((rows, HEAD_DIM), F32)]
    if fox:
        in_specs += [
            pl.BlockSpec((1, 1, rows, page), lambda b, p, pt: (layer, pt[b, npg - 1 - p], 0, 0)),
            pl.BlockSpec((1, hkv, HEAD_DIM), lambda b, p, pt: (b, 0, 0)),
            pl.BlockSpec((1, hkv, HEAD_DIM), lambda b, p, pt: (b, 0, 0)),
            pl.BlockSpec((1, rows, 1), lambda b, p, pt: (b, 0, 0)),
        ]
        args += [lf_cache, knew, vnew, lfnew]
        scratch = [pltpu.VMEM((rows, 1), F32), pltpu.VMEM((rows, 1), F32)] + scratch
    return pl.pallas_call(
        functools.partial(_decode_kernel, fox=fox, hkv=hkv),
        out_shape=jax.ShapeDtypeStruct((bs, rows, HEAD_DIM), F32),
        grid_spec=pltpu.PrefetchScalarGridSpec(
            num_scalar_prefetch=1,
            grid=(bs, npg),
            in_specs=in_specs,
            out_specs=pl.BlockSpec((1, rows, HEAD_DIM), lambda b, p, pt: (b, 0, 0)),
            scratch_shapes=scratch,
        ),
        compiler_params=_params("parallel", "arbitrary"),
    )(page_table, *args)


def _pad_heads(a, hkv):
    bs, _, x = a.shape
    a = a.reshape(bs, hkv, GQA, x)
    a = jnp.concatenate([a, jnp.zeros((bs, hkv, DEC_ROWS - GQA, x), a.dtype)], axis=2)
    return a.reshape(bs, hkv * DEC_ROWS, x)


def _unpad_heads(a, hkv):
    bs, _, x = a.shape
    return a.reshape(bs, hkv, DEC_ROWS, x)[:, :, :GQA].reshape(bs, hkv * GQA * x)


def _shift_kernel(z_ref, prev_ref, st_ref, mu_ref, o_ref, *, tiles_per_seq, sample_tile):
    i = pl.program_id(0)
    z = z_ref[...]
    first = (i % tiles_per_seq) == 0
    row0 = jnp.where(first, 0.0, prev_ref[SUBLANES - 1:SUBLANES, :])
    rid = lax.broadcasted_iota(jnp.int32, z.shape, 0)
    zp = jnp.where(rid == 0, row0, pltpu.roll(z, 1, 0))
    zp = jnp.where(i == sample_tile, st_ref[...], zp)
    o_ref[...] = z + (zp - z) * mu_ref[...]


def _token_shift(p, lay, state_pad, mu, t, n_prompt, tr):
    n = p.shape[0]
    zp = lay.z_width
    tc = _pick(zp, tuple(c for c in (1024, 896, 768, 640, 512, 384, 256, 128) if lay.z_lo % c == 0))
    cb = lay.z_lo // tc
    per8 = tr // SUBLANES
    return pl.pallas_call(
        functools.partial(_shift_kernel, tiles_per_seq=t // tr, sample_tile=n_prompt // tr),
        out_shape=jax.ShapeDtypeStruct((n, zp), F32),
        grid=(n // tr, zp // tc),
        in_specs=[
            pl.BlockSpec((tr, tc), lambda i, c: (i, cb + c)),
            pl.BlockSpec((SUBLANES, tc), lambda i, c: (jnp.maximum(i * per8 - 1, 0), cb + c)),
            pl.BlockSpec((tr, tc), lambda i, c: (0, c)),
            pl.BlockSpec((1, tc), lambda i, c: (0, c)),
        ],
        out_specs=pl.BlockSpec((tr, tc), lambda i, c: (i, c)),
        compiler_params=_params("parallel", "parallel"),
    )(p, p, state_pad, mu)


def _head_sum_matrix(width):
    a = lax.broadcasted_iota(jnp.int32, (width, width), 0) // RWKV_HEAD
    b = lax.broadcasted_iota(jnp.int32, (width, width), 1) // RWKV_HEAD
    return (a == b).astype(F32)


def _rwkv_prep_kernel(k_ref, zw_ref, za_ref, zg_ref, w2_ref, a2_ref, g2_ref, w0_ref, a0_ref, kk_ref, ka_ref,
                      dec_ref, kmod_ref, na_ref, nb_ref, g_ref):
    k = k_ref[...]
    dw = jnp.dot(jnp.tanh(zw_ref[...]).astype(BF16), w2_ref[...], preferred_element_type=F32)
    x = w0_ref[...] + dw
    w_log = _log_sigmoid(x) - 0.5
    dec_ref[...] = jnp.exp(-jnp.exp(w_log))
    a = _sigmoid(a0_ref[...] + jnp.dot(za_ref[...].astype(BF16), a2_ref[...], preferred_element_type=F32))
    g_ref[...] = jnp.dot(_sigmoid(zg_ref[...]).astype(BF16), g2_ref[...], preferred_element_type=F32)
    kk = k * kk_ref[...]
    hs = _head_sum_matrix(LANES)
    for c in range(k.shape[1] // LANES):
        sl = slice(c * LANES, (c + 1) * LANES)
        kc = kk[:, sl]
        ss = jnp.dot(kc * kc, hs, precision=HIGHEST, preferred_element_type=F32)
        kn = kc / jnp.maximum(jnp.sqrt(ss), 1e-12)
        na_ref[:, sl] = -kn
        nb_ref[:, sl] = kn * a[:, sl]
    kmod_ref[...] = k * (1.0 + (a - 1.0) * ka_ref[...])


def _rwkv_prep(zs, zoff, w2, a2, g2, w0, a0, k_k, k_a, d, tr):
    n = zs.shape[0]
    tc = _pick(d, (512, 256, 128))
    rdp, rip, rgp = w2.shape[0], a2.shape[0], g2.shape[0]
    row = lambda i, c: (i, c)
    vec = pl.BlockSpec((1, tc), lambda i, c: (0, c))
    out = jax.ShapeDtypeStruct((n, d), F32)
    return pl.pallas_call(
        _rwkv_prep_kernel,
        out_shape=[out] * 5,
        grid=(n // tr, d // tc),
        in_specs=[
            pl.BlockSpec((tr, tc), lambda i, c: (i, zoff["k"] // tc + c)),
            pl.BlockSpec((tr, rdp), lambda i, c: (i, zoff["zw"] // rdp)),
            pl.BlockSpec((tr, rip), lambda i, c: (i, zoff["za"] // rip)),
            pl.BlockSpec((tr, rgp), lambda i, c: (i, zoff["zg"] // rgp)),
            pl.BlockSpec((rdp, tc), lambda i, c: (0, c)),
            pl.BlockSpec((rip, tc), lambda i, c: (0, c)),
            pl.BlockSpec((rgp, tc), lambda i, c: (0, c)),
            vec, vec, vec, vec,
        ],
        out_specs=[pl.BlockSpec((tr, tc), row)] * 5,
        compiler_params=_params("parallel", "parallel"),
    )(zs, zs, zs, zs, w2, a2, g2, w0, a0, k_k, k_a)


def _scan_step(state, r, w, k, v, a, b, lo, eye0, eye1):
    def half_sums(x):
        s0 = jnp.sum(jnp.where(lo, x, 0.0), axis=1, keepdims=True)
        s1 = jnp.sum(jnp.where(lo, 0.0, x), axis=1, keepdims=True)
        return s0, s1

    sa0, sa1 = half_sums(state * a)
    vb = jnp.broadcast_to(v, state.shape)
    v0 = jnp.sum(jnp.where(eye0, vb, 0.0), axis=1, keepdims=True)
    v1 = jnp.sum(jnp.where(eye1, vb, 0.0), axis=1, keepdims=True)
    state = state * w + jnp.where(lo, sa0, sa1) * b + jnp.where(lo, v0, v1) * k
    y0, y1 = half_sums(state * r)
    y = jnp.sum(jnp.where(eye0, y0, jnp.where(eye1, y1, 0.0)), axis=0, keepdims=True)
    return state, y


def _scan_masks():
    shape = (RWKV_HEAD, LANES)
    i = lax.broadcasted_iota(jnp.int32, shape, 0)
    c = lax.broadcasted_iota(jnp.int32, shape, 1)
    return c < RWKV_HEAD, c == i, c == i + RWKV_HEAD


def _load_state(s_ref, p):
    return jnp.concatenate([s_ref[0, 2 * p], s_ref[0, 2 * p + 1]], axis=-1)


def _store_state(s_ref, p, state):
    s_ref[0, 2 * p] = state[:, :RWKV_HEAD]
    s_ref[0, 2 * p + 1] = state[:, RWKV_HEAD:]


def _scan_prompt_kernel(r_ref, w_ref, k_ref, v_ref, a_ref, b_ref, y_ref, sout_ref, st_ref, *, pairs, tchunk):
    tc = pl.program_id(2)

    @pl.when(tc == 0)
    def _():
        st_ref[...] = jnp.zeros_like(st_ref)

    lo, eye0, eye1 = _scan_masks()

    def group(t8, carry):
        base = pl.multiple_of(t8 * SUBLANES, SUBLANES)
        for p in range(pairs):
            sl = slice(p * LANES, (p + 1) * LANES)
            r, w, k, v, a, b = (ref[pl.ds(base, SUBLANES), sl] for ref in (r_ref, w_ref, k_ref, v_ref, a_ref, b_ref))
            state = st_ref[p]
            ys = []
            for u in range(SUBLANES):
                row = slice(u, u + 1)
                state, y = _scan_step(state, r[row], w[row], k[row], v[row], a[row], b[row], lo, eye0, eye1)
                ys.append(y)
            st_ref[p] = state
            y_ref[pl.ds(base, SUBLANES), sl] = jnp.concatenate(ys, axis=0)
        return carry

    lax.fori_loop(0, tchunk // SUBLANES, group, 0)

    @pl.when(tc == pl.num_programs(2) - 1)
    def _():
        for p in range(pairs):
            _store_state(sout_ref, p, st_ref[p])


def _scan_sample_kernel(r_ref, w_ref, k_ref, v_ref, a_ref, b_ref, s0_ref, y_ref, sout_ref, *, pairs):
    lo, eye0, eye1 = _scan_masks()
    for p in range(pairs):
        sl = slice(p * LANES, (p + 1) * LANES)
        row = lambda ref: ref[0, :, sl]
        state, y = _scan_step(_load_state(s0_ref, p), row(r_ref), row(w_ref), row(k_ref), row(v_ref), row(a_ref),
                              row(b_ref), lo, eye0, eye1)
        _store_state(sout_ref, p, state)
        y_ref[0, :, sl] = y


def _rwkv_scan_prompt(r_src, r_cb, v_cb, dec, kmod, na, nb, b, t, d):
    pairs = _pick(d // LANES, (4, 2, 1))
    wcol = pairs * LANES
    tchunk = _pick(t, (256, 128))
    nt = t // tchunk
    nh = d // RWKV_HEAD
    rowmap = lambda bi, g, tc: (bi * nt + tc, g)
    blk = pl.BlockSpec((tchunk, wcol), rowmap)
    return pl.pallas_call(
        functools.partial(_scan_prompt_kernel, pairs=pairs, tchunk=tchunk),
        out_shape=[jax.ShapeDtypeStruct((b * t, d), F32), jax.ShapeDtypeStruct((b, nh, RWKV_HEAD, RWKV_HEAD), F32)],
        grid=(b, d // wcol, nt),
        in_specs=[
            pl.BlockSpec((tchunk, wcol), lambda bi, g, tc: (bi * nt + tc, r_cb // pairs + g)),
            blk, blk,
            pl.BlockSpec((tchunk, wcol), lambda bi, g, tc: (bi * nt + tc, v_cb // pairs + g)),
            blk, blk,
        ],
        out_specs=[blk, pl.BlockSpec((1, 2 * pairs, RWKV_HEAD, RWKV_HEAD), lambda bi, g, tc: (bi, g, 0, 0))],
        scratch_shapes=[pltpu.VMEM((pairs, RWKV_HEAD, LANES), F32)],
        compiler_params=_params("parallel", "parallel", "arbitrary"),
    )(r_src, dec, kmod, r_src, na, nb)


def _rwkv_scan_sample(r, dec, kmod, v, na, nb, s0):
    bs, _, d = r.shape
    pairs = _pick(d // LANES, (4, 2, 1))
    wcol = pairs * LANES
    blk = pl.BlockSpec((1, 1, wcol), lambda bi, g: (bi, 0, g))
    sblk = pl.BlockSpec((1, 2 * pairs, RWKV_HEAD, RWKV_HEAD), lambda bi, g: (bi, g, 0, 0))
    return pl.pallas_call(
        functools.partial(_scan_sample_kernel, pairs=pairs),
        out_shape=[jax.ShapeDtypeStruct((bs, 1, d), F32), jax.ShapeDtypeStruct(s0.shape, F32)],
        grid=(bs, d // wcol),
        in_specs=[blk] * 6 + [sblk],
        out_specs=[blk, sblk],
        compiler_params=_params("parallel", "parallel"),
    )(r, dec, kmod, v, na, nb, s0)


def _rwkv_post_kernel(y_ref, r_ref, k_ref, v_ref, g_ref, rk_ref, lw_ref, lb_ref, o_ref):
    hs = _head_sum_matrix(LANES)
    inv = 1.0 / RWKV_HEAD
    for c in range(y_ref.shape[1] // LANES):
        sl = slice(c * LANES, (c + 1) * LANES)
        y = y_ref[:, sl]
        mu = jnp.dot(y, hs, precision=HIGHEST, preferred_element_type=F32) * inv
        yc = y - mu
        var = jnp.dot(yc * yc, hs, precision=HIGHEST, preferred_element_type=F32) * inv
        yn = yc * lax.rsqrt(var + RWKV_GN_EPS) * lw_ref[:, sl] + lb_ref[:, sl]
        v = v_ref[:, sl]
        bonus = jnp.dot(r_ref[:, sl] * k_ref[:, sl] * rk_ref[:, sl], hs, precision=HIGHEST,
                        preferred_element_type=F32) * v
        o_ref[:, sl] = (yn + bonus) * g_ref[:, sl]


def _rwkv_post(y, zs, zoff, kmod, g, r_k, ln_w, ln_b, d, tr):
    n = y.shape[0]
    tc = _pick(d, (512, 256, 128))
    blk = pl.BlockSpec((tr, tc), lambda i, c: (i, c))
    vec = pl.BlockSpec((1, tc), lambda i, c: (0, c))
    return pl.pallas_call(
        _rwkv_post_kernel,
        out_shape=jax.ShapeDtypeStruct((n, d), F32),
        grid=(n // tr, d // tc),
        in_specs=[
            blk,
            pl.BlockSpec((tr, tc), lambda i, c: (i, zoff["r"] // tc + c)),
            blk,
            pl.BlockSpec((tr, tc), lambda i, c: (i, zoff["v"] // tc + c)),
            blk, vec, vec, vec,
        ],
        out_specs=blk,
        compiler_params=_params("parallel", "parallel"),
    )(y, zs, kmod, zs, g, r_k, ln_w, ln_b)


def _merge_kernel(gf_ref, gr_ref, gs_ref, of_ref, or_ref, os_ref, o_ref):
    o_ref[...] = (gf_ref[...] * of_ref[...] + gr_ref[...] * or_ref[...] + gs_ref[...] * os_ref[...]).astype(BF16)


def _merge(p, lay, o_fox, o_rwkv, o_sb, d, tr):
    n = p.shape[0]
    tc = _pick(d, tuple(c for c in (1024, 512, 256, 128) if lay.off["gates"] % c == 0))
    gb = lay.off["gates"] // tc
    per = d // tc
    blk = pl.BlockSpec((tr, tc), lambda i, c: (i, c))
    gate = lambda which: pl.BlockSpec((tr, tc), lambda i, c: (i, gb + which * per + c))
    return pl.pallas_call(
        _merge_kernel,
        out_shape=jax.ShapeDtypeStruct((n, d), BF16),
        grid=(n // tr, d // tc),
        in_specs=[gate(0), gate(1), gate(2), blk, blk, blk],
        out_specs=blk,
        compiler_params=_params("parallel", "parallel"),
    )(p, p, p, o_fox, o_rwkv, o_sb)


def _outproj_kernel(m_ref, w_ref, x_ref, o_ref):
    o_ref[...] = x_ref[...] + jnp.dot(m_ref[...], w_ref[...], preferred_element_type=F32)


def _outproj(merged, w, x, tm):
    n, d = merged.shape
    tn = _pick(w.shape[1], (512, 256, 128))
    return pl.pallas_call(
        _outproj_kernel,
        out_shape=jax.ShapeDtypeStruct((n, w.shape[1]), F32),
        grid=(w.shape[1] // tn, n // tm),
        in_specs=[
            pl.BlockSpec((tm, d), lambda j, i: (i, 0)),
            pl.BlockSpec((d, tn), lambda j, i: (0, j)),
            pl.BlockSpec((tm, tn), lambda j, i: (i, j)),
        ],
        out_specs=pl.BlockSpec((tm, tn), lambda j, i: (i, j)),
        compiler_params=_params("parallel", "parallel"),
    )(merged, w, x)


def _top16_rows(s):
    nk = s.shape[0]
    idx = lax.broadcasted_iota(jnp.int32, s.shape, 0).astype(F32)
    work = s
    kept = jnp.full(s.shape, -jnp.inf, F32)
    tops = []
    for _ in range(PEER_TOPK):
        m = jnp.max(work, axis=0, keepdims=True)
        first = jnp.min(jnp.where(work == m, idx, float(nk)), axis=0, keepdims=True)
        hit = idx == first
        kept = jnp.where(hit, s, kept)
        work = jnp.where(hit, -jnp.inf, work)
        tops.append(m)
    return jnp.concatenate(tops, axis=0), kept


def _kth_largest(c, kth):
    n = c.shape[0]
    idx = lax.broadcasted_iota(jnp.int32, c.shape, 0).astype(F32)
    m = None
    for _ in range(kth):
        m = jnp.max(c, axis=0, keepdims=True)
        first = jnp.min(jnp.where(c == m, idx, float(n)), axis=0, keepdims=True)
        c = jnp.where(idx == first, -jnp.inf, c)
    return m


def _peer_gate_kernel(h_ref, wq_ref, sk_ref, s1_ref, s2_ref, e1_ref, e2_ref, thr_ref):
    nk = sk_ref.shape[2]
    qt = lax.dot_general(wq_ref[0], h_ref[...], (((1,), (1,)), ((), ())), preferred_element_type=F32)
    half = qt.shape[0] // 2
    sc = [jnp.dot(sk_ref[0, p], qt[p * half:(p + 1) * half].astype(BF16), preferred_element_type=F32)
          for p in range(2)]
    t1, s1m = _top16_rows(sc[0])
    t2, s2m = _top16_rows(sc[1])
    tn = t1.shape[1]
    cand = (t1[:, None, :] + t2[None, :, :]).reshape(PEER_TOPK * PEER_TOPK, tn)
    thr = _kth_largest(cand, PEER_TOPK)
    x1 = jnp.exp(t1 - t1[0:1])
    x2 = jnp.exp(t2 - t2[0:1])
    pair = (x1[:, None, :] * x2[None, :, :]).reshape(PEER_TOPK * PEER_TOPK, tn)
    zsum = jnp.sum(jnp.where(cand >= thr, pair, 0.0), axis=0, keepdims=True)
    s1_ref[0] = s1m
    s2_ref[0] = s2m
    e1_ref[0] = jnp.exp(s1m - t1[0:1])
    e2_ref[0] = jnp.exp(s2m - t2[0:1]) / zsum
    thr_ref[0] = thr


def _peer_gate(h, wq_t, subkeys, tn):
    n, d = h.shape
    hp, _, nk, half = subkeys.shape
    arr = jax.ShapeDtypeStruct((hp, nk, n), F32)
    blk = pl.BlockSpec((1, nk, tn), lambda i, hh: (hh, 0, i))
    return pl.pallas_call(
        _peer_gate_kernel,
        out_shape=[arr, arr, arr, arr, jax.ShapeDtypeStruct((hp, 1, n), F32)],
        grid=(n // tn, hp),
        in_specs=[
            pl.BlockSpec((tn, d), lambda i, hh: (i, 0)),
            pl.BlockSpec((1, 2 * half, d), lambda i, hh: (hh, 0, 0)),
            pl.BlockSpec((1, 2, nk, half), lambda i, hh: (hh, 0, 0, 0)),
        ],
        out_specs=[blk, blk, blk, blk, pl.BlockSpec((1, 1, tn), lambda i, hh: (hh, 0, i))],
        compiler_params=_params("parallel", "arbitrary"),
    )(h, wq_t, subkeys)


def _gelu_exact(x):
    return 0.5 * x * (1.0 + lax.erf(x * (2.0 ** -0.5)))


def _peer_dense_kernel(h_ref, u_ref, vt_ref, s1_ref, s2_ref, e1_ref, e2_ref, thr_ref, x_ref, o_ref, acc_ref,
                       *, rows_per_step):
    e = pl.program_id(1)
    hp, nk, tn = s2_ref.shape

    @pl.when(e == 0)
    def _():
        acc_ref[...] = jnp.zeros_like(acc_ref)

    act = _gelu_exact(lax.dot_general(u_ref[...], h_ref[...], (((1,), (1,)), ((), ())), preferred_element_type=F32))
    per_group = SUBLANES // rows_per_step
    base = pl.multiple_of((e // per_group) * SUBLANES, SUBLANES)
    sub = e % per_group

    def key_rows(ref, hh):
        grp = ref[hh, pl.ds(base, SUBLANES), :]
        rows = grp[0:rows_per_step]
        for o in range(1, per_group):
            rows = jnp.where(sub == o, grp[o * rows_per_step:(o + 1) * rows_per_step], rows)
        return rows

    s1_rows = [key_rows(s1_ref, hh) for hh in range(hp)]
    e1_rows = [key_rows(e1_ref, hh) for hh in range(hp)]
    coef = []
    for ii in range(rows_per_step):
        gate = jnp.zeros((nk, tn), F32)
        for hh in range(hp):
            val = s1_rows[hh][ii:ii + 1] + s2_ref[hh]
            gate = gate + jnp.where(val >= thr_ref[hh], e1_rows[hh][ii:ii + 1] * e2_ref[hh], 0.0)
        coef.append((gate * act[ii * nk:(ii + 1) * nk]).astype(BF16))
    acc_ref[...] += jnp.dot(vt_ref[...], jnp.concatenate(coef, axis=0), preferred_element_type=F32)

    @pl.when(e == pl.num_programs(1) - 1)
    def _():
        o_ref[...] = x_ref[...] + acc_ref[...].T


def _peer_dense(h, u, vt, s1, s2, e1, e2, thr, x, tn):
    n, d = h.shape
    hp, nk, _ = s1.shape
    ne = u.shape[0]
    rows_per_step = _pick(ne // nk, (4, 2, 1))
    te = rows_per_step * nk
    full = pl.BlockSpec((hp, nk, tn), lambda i, e: (0, 0, i))
    return pl.pallas_call(
        functools.partial(_peer_dense_kernel, rows_per_step=rows_per_step),
        out_shape=jax.ShapeDtypeStruct((n, d), F32),
        grid=(n // tn, ne // te),
        in_specs=[
            pl.BlockSpec((tn, d), lambda i, e: (i, 0)),
            pl.BlockSpec((te, d), lambda i, e: (e, 0)),
            pl.BlockSpec((d, te), lambda i, e: (0, e)),
            full, full, full, full,
            pl.BlockSpec((hp, 1, tn), lambda i, e: (0, 0, i)),
            pl.BlockSpec((tn, d), lambda i, e: (i, 0)),
        ],
        out_specs=pl.BlockSpec((tn, d), lambda i, e: (i, 0)),
        scratch_shapes=[pltpu.VMEM((d, tn), F32)],
        compiler_params=_params("parallel", "arbitrary"),
    )(h, u, vt, s1, s2, e1, e2, thr, x)


def kernel(x_prompt, x_sample, cache_fox_k, cache_fox_v, cache_fox_logf, cache_sb_k, cache_sb_v, state_rwkv, state_rwkv_shift, page_table, norm_mix, w_in, b_forget, fox_q_norm, fox_k_norm, sb_q_norm, sb_k_norm, rwkv_mu, rwkv_w0, rwkv_w2, rwkv_a0, rwkv_a2, rwkv_g2, rwkv_k_k, rwkv_k_a, rwkv_r_k, rwkv_ln_w, rwkv_ln_b, w_out, norm_ffn, peer_wq, peer_subkeys, peer_u, peer_v):
    b, t, d = x_prompt.shape
    bs = x_sample.shape[0]
    depth = w_in.shape[0]
    hq = d // HEAD_DIM
    hkv = hq // GQA
    dkv = hkv * HEAD_DIM
    rd, ri, rg = rwkv_w2.shape[1], rwkv_a2.shape[1], rwkv_g2.shape[1]
    n_prompt = b * t
    tr = _pick(t, (256, 128))
    n = _round_up(n_prompt + bs, tr)
    tm = _pick(n, (768, 512, 384, 256, 128))
    tn_in = 512 if d % 2048 == 0 else LANES
    lay = _Layout(d, hq, rd, ri, rg, tn_in)
    zoff = {k: lay.off[k] - lay.z_lo for k in ("r", "k", "v", "zw", "za", "zg")}
    rdp, rip, rgp = _round_up(rd, LANES), _round_up(ri, LANES), _round_up(rg, LANES)
    tn_peer = _pick(n, (256, 128))

    def tile_heads(g, count):
        return jnp.tile(g, count)

    def pad_rows(a, rows):
        return jnp.concatenate([a, jnp.zeros((rows - a.shape[0],) + a.shape[1:], a.dtype)], axis=0)

    def zr_original_order(rows):
        return jnp.concatenate([rows[..., zoff[k]:zoff[k] + w] for k, w in
                                (("r", d), ("k", d), ("v", d), ("zw", rd), ("za", ri), ("zg", rg))], axis=-1)

    def to_z_layout(a):
        parts, o, src = [], 0, 0
        for k, w in (("r", d), ("k", d), ("v", d), ("zw", rd), ("za", ri), ("zg", rg)):
            if zoff[k] > o:
                parts.append(jnp.zeros(a.shape[:-1] + (zoff[k] - o,), a.dtype))
            parts.append(a[..., src:src + w])
            o, src = zoff[k] + w, src + w
        if lay.z_width > o:
            parts.append(jnp.zeros(a.shape[:-1] + (lay.z_width - o,), a.dtype))
        return jnp.concatenate(parts, axis=-1)

    x = pad_rows(jnp.concatenate([x_prompt.reshape(n_prompt, d), x_sample.reshape(bs, d)], axis=0), n)
    n_pool, page = cache_fox_logf.shape[1], cache_fox_logf.shape[2]
    lf_cache = jnp.swapaxes(cache_fox_logf, 2, 3).reshape(depth * n_pool, hq, page)
    lf_cache = _pad_heads(lf_cache, hkv).reshape(depth, n_pool, hkv * DEC_ROWS, page)

    prompt_states, sample_states = [], []
    for l in range(depth):
        h = _rmsnorm(x, norm_mix[l], tm)
        w_l = lay.scatter_cols(w_in[l], lay.ncol).astype(BF16)
        colp = jnp.zeros((lay.n_in,), F32)
        for name, g, cnt in (("fq", fox_q_norm[l], hq), ("fk", fox_k_norm[l], hkv), ("sq", sb_q_norm[l], hq),
                             ("sk", sb_k_norm[l], hkv)):
            s0, w0 = lay.src[name]
            colp = colp.at[s0:s0 + w0].set(tile_heads(g, cnt))
        s0, w0 = lay.src["ff"]
        colp = colp.at[s0:s0 + w0].set(b_forget[l])
        colp = lay.scatter_cols(colp.reshape(1, -1), lay.ncol)
        p = _inproj(h, w_l, colp, jnp.asarray(lay.modes), tm, tn_in)

        def cols(name, width, rows=slice(None)):
            return p[rows, lay.off[name]:lay.off[name] + width]

        c = _cumsum_time(p, b, t, lay.off["ff"] // LANES)[:, :hq].reshape(b, t, hq)
        tq = _pick(t, (256, 128))
        c_t = jnp.swapaxes(c, 1, 2)
        cq = c_t.reshape(b, hq, t, 1)
        ck = c_t.reshape(b, hq, t // tq, 1, tq)
        o_fox_p = _prompt_attention(p, lay, b, t, d, cq, ck)
        o_sb_p = _prompt_attention(p, lay, b, t, d)

        srow = slice(n_prompt, n_prompt + bs)
        scale = HEAD_DIM ** -0.5
        fq_s = _pad_heads((cols("fq", d, srow) * scale).reshape(bs, hq, HEAD_DIM), hkv)
        sq_s = _pad_heads((cols("sq", d, srow) * scale).reshape(bs, hq, HEAD_DIM), hkv)
        fk_s = cols("fk", dkv, srow).reshape(bs, hkv, HEAD_DIM)
        fv_s = cols("fv", dkv, srow).reshape(bs, hkv, HEAD_DIM)
        lf_s = cols("ff", hq, srow)
        o_fox_s = _decode_attention(l, page_table, fq_s, cache_fox_k, cache_fox_v, lf_cache, fk_s, fv_s,
                                    _pad_heads(lf_s.reshape(bs, hq, 1), hkv))
        o_sb_s = _decode_attention(l, page_table, sq_s, cache_sb_k, cache_sb_v)
        o_fox = pad_rows(jnp.concatenate([o_fox_p, _unpad_heads(o_fox_s, hkv)], axis=0), n)
        o_sb = pad_rows(jnp.concatenate([o_sb_p, _unpad_heads(o_sb_s, hkv)], axis=0), n)

        state_pad = pad_rows(to_z_layout(state_rwkv_shift[l]), tr)
        zs = _token_shift(p, lay, state_pad, to_z_layout(rwkv_mu[l]).reshape(1, -1), t, n_prompt, tr)
        pad_k = lambda w2, rp: pad_rows(w2, rp).astype(BF16)
        vec = lambda a: a.reshape(1, d)
        dec, kmod, na, nb, gg = _rwkv_prep(zs, zoff, pad_k(rwkv_w2[l], rdp), pad_k(rwkv_a2[l], rip),
                                           pad_k(rwkv_g2[l], rgp), vec(rwkv_w0[l]), vec(rwkv_a0[l]),
                                           vec(rwkv_k_k[l]), vec(rwkv_k_a[l]), d, tr)
        y_p, s_p = _rwkv_scan_prompt(zs, zoff["r"] // LANES, zoff["v"] // LANES, dec, kmod, na, nb, b, t, d)
        one = lambda a: a[srow].reshape(bs, 1, d)
        y_s, s_s = _rwkv_scan_sample(one(zs[:, zoff["r"]:zoff["r"] + d]), one(dec), one(kmod),
                                     one(zs[:, zoff["v"]:zoff["v"] + d]), one(na), one(nb), state_rwkv[l])
        y = pad_rows(jnp.concatenate([y_p, y_s.reshape(bs, d)], axis=0), n)
        o_rwkv = _rwkv_post(y, zs, zoff, kmod, gg, vec(rwkv_r_k[l].reshape(-1)), vec(rwkv_ln_w[l]),
                            vec(rwkv_ln_b[l]), d, tr)

        merged = _merge(p, lay, o_fox, o_rwkv, o_sb, d, tr)
        x = _outproj(merged, w_out[l].astype(BF16), x, tm)

        h2 = _rmsnorm(x, norm_ffn[l], tm)
        s1, s2, e1, e2, thr = _peer_gate(h2, peer_wq[l].T.reshape(-1, peer_wq.shape[-1] // peer_subkeys.shape[1],
                                                                   d).astype(BF16),
                                         peer_subkeys[l].astype(BF16), tn_peer)
        x = _peer_dense(h2, peer_u[l].astype(BF16), peer_v[l].T.astype(BF16), s1, s2, e1, e2, thr, x, tn_peer)

        zr_rows = p[:, lay.z_lo:lay.z_lo + lay.z_width]
        last = jnp.arange(b) * t + (t - 1)
        prompt_states.append((
            cols("fk", dkv, slice(0, n_prompt)).reshape(b, t, hkv, HEAD_DIM),
            cols("fv", dkv, slice(0, n_prompt)).reshape(b, t, hkv, HEAD_DIM),
            cols("ff", hq, slice(0, n_prompt)).reshape(b, t, hq),
            cols("sk", dkv, slice(0, n_prompt)).reshape(b, t, hkv, HEAD_DIM),
            cols("sv", dkv, slice(0, n_prompt)).reshape(b, t, hkv, HEAD_DIM),
            s_p, zr_original_order(zr_rows[last])))
        sample_states.append((
            fk_s.reshape(bs, 1, hkv, HEAD_DIM), fv_s.reshape(bs, 1, hkv, HEAD_DIM), lf_s.reshape(bs, 1, hq),
            cols("sk", dkv, srow).reshape(bs, 1, hkv, HEAD_DIM), cols("sv", dkv, srow).reshape(bs, 1, hkv, HEAD_DIM),
            s_s, zr_original_order(zr_rows[srow])))

    stack = lambda states, i: jnp.stack([s[i] for s in states])
    y_prompt = x[:n_prompt].reshape(b, t, d)
    y_sample = x[n_prompt:n_prompt + bs].reshape(bs, 1, d)
    return ((y_prompt, y_sample) + tuple(stack(prompt_states, i) for i in range(7))
            + tuple(stack(sample_states, i) for i in range(7)))
```

```python
import functools

import numpy as np
import jax
import jax.numpy as jnp
from jax import lax
from jax.experimental import pallas as pl
from jax.experimental.pallas import tpu as pltpu

F32 = jnp.float32
BF16 = jnp.bfloat16
HIGHEST = lax.Precision.HIGHEST

NORM_EPS = 1e-6
RWKV_GN_EPS = 64e-5
LANES = 128
SUBLANES = 8
HEAD_DIM = 128
RWKV_HEAD = 64
CHUNK = 64
GQA = 4
PEER_TOPK = 16
VMEM_LIMIT_BYTES = 56 * 1024 * 1024

MODE_ID, MODE_RMS, MODE_SIGMOID, MODE_LOGSIG = 0, 1, 2, 3


def _params(*sem):
    return pltpu.CompilerParams(dimension_semantics=sem, vmem_limit_bytes=VMEM_LIMIT_BYTES)


def _round_up(n, m):
    return (n + m - 1) // m * m


def _pick(n, cands):
    for c in cands:
        if c <= n and n % c == 0:
            return c
    raise ValueError(f"no tile for {n} in {cands}")


def _sigmoid(x):
    return 1.0 / (1.0 + jnp.exp(-x))


def _log_sigmoid(x):
    return jnp.minimum(x, 0.0) - jnp.log1p(jnp.exp(-jnp.abs(x)))


def _rmsnorm_kernel(x_ref, g_ref, o_ref, *, transposed):
    x = x_ref[...]
    ms = jnp.mean(x * x, axis=-1, keepdims=True)
    y = x * lax.rsqrt(ms + NORM_EPS) * g_ref[...]
    o_ref[...] = (y.T if transposed else y).astype(o_ref.dtype)


def _rmsnorm(x, g, tm, transposed=False):
    n, d = x.shape
    return pl.pallas_call(
        functools.partial(_rmsnorm_kernel, transposed=transposed),
        out_shape=jax.ShapeDtypeStruct((d, n) if transposed else (n, d), BF16),
        grid=(n // tm,),
        in_specs=[pl.BlockSpec((tm, d), lambda i: (i, 0)), pl.BlockSpec((1, d), lambda i: (0, 0))],
        out_specs=pl.BlockSpec((d, tm), lambda i: (0, i)) if transposed else pl.BlockSpec((tm, d), lambda i: (i, 0)),
        compiler_params=_params("parallel"),
    )(x, g.reshape(1, d))


def _inproj_kernel(modes_ref, h_ref, w_ref, cp_ref, o_ref):
    acc = jnp.dot(h_ref[...], w_ref[...], preferred_element_type=F32)
    mode = modes_ref[pl.program_id(1)]

    @pl.when(mode == MODE_ID)
    def _():
        o_ref[...] = acc

    @pl.when(mode == MODE_RMS)
    def _():
        for c in range(acc.shape[1] // HEAD_DIM):
            sl = slice(c * HEAD_DIM, (c + 1) * HEAD_DIM)
            xs = acc[:, sl]
            ms = jnp.mean(xs * xs, axis=-1, keepdims=True)
            o_ref[:, sl] = xs * lax.rsqrt(ms + NORM_EPS) * cp_ref[:, sl]

    @pl.when(mode == MODE_SIGMOID)
    def _():
        o_ref[...] = _sigmoid(acc)

    @pl.when(mode == MODE_LOGSIG)
    def _():
        o_ref[...] = _log_sigmoid(acc + cp_ref[...])


def _inproj(h, w, colp, modes, tm, tn):
    n, d = h.shape
    ncol = w.shape[1]
    return pl.pallas_call(
        _inproj_kernel,
        out_shape=jax.ShapeDtypeStruct((n, ncol), F32),
        grid_spec=pltpu.PrefetchScalarGridSpec(
            num_scalar_prefetch=1,
            grid=(n // tm, ncol // tn),
            in_specs=[
                pl.BlockSpec((tm, d), lambda i, j, m: (i, 0)),
                pl.BlockSpec((d, tn), lambda i, j, m: (0, j)),
                pl.BlockSpec((1, tn), lambda i, j, m: (0, j)),
            ],
            out_specs=pl.BlockSpec((tm, tn), lambda i, j, m: (i, j)),
        ),
        compiler_params=_params("parallel", "parallel"),
    )(modes, h, w, colp)


class _Layout:
    def __init__(self, d, hq, rd, ri, rg, tn):
        dkv = d // GQA
        zw = 3 * d + rd + ri + rg
        src = {}
        o = 0
        for name, width in (("fq", d), ("fk", dkv), ("fv", dkv), ("ff", hq), ("sq", d), ("sk", dkv), ("sv", dkv)):
            src[name] = (o, width)
            o += width
        for name, width in (("r", d), ("k", d), ("v", d), ("zw", rd), ("za", ri), ("zg", rg)):
            src[name] = (o, width)
            o += width
        src["gates"] = (o, 3 * d)
        self.n_in = o + 3 * d
        self.zr_width = zw
        order = (("fq", MODE_RMS), ("sq", MODE_RMS), ("fk", MODE_RMS), ("fv", MODE_ID), ("sk", MODE_RMS),
                 ("sv", MODE_ID), ("r", MODE_ID), ("k", MODE_ID), ("v", MODE_ID), ("zw", MODE_ID),
                 ("za", MODE_ID), ("zg", MODE_ID), ("gates", MODE_SIGMOID), ("ff", MODE_LOGSIG))
        self.src = src
        self.off = {}
        self.pieces = []
        seg_modes = []
        o, prev = 0, None
        for name, mode in order:
            o = _round_up(o, tn if mode != prev else _round_up(min(src[name][1], 4 * LANES), LANES))
            self.off[name] = o
            seg_modes.append((o, mode))
            self.pieces.append((o, src[name][0], src[name][1]))
            o += src[name][1]
            prev = mode
        self.ncol = _round_up(o, tn)
        self.z_lo = self.off["r"]
        self.z_width = _round_up(self.off["zg"] + rg, LANES) - self.z_lo
        modes = np.zeros((self.ncol // tn,), np.int32)
        for t in range(self.ncol // tn):
            for so, m in seg_modes:
                if so <= t * tn:
                    modes[t] = m
        self.modes = modes

    def scatter_cols(self, a, width):
        parts, o = [], 0
        for dst, s, w in self.pieces:
            if dst > o:
                parts.append(jnp.zeros(a.shape[:-1] + (dst - o,), a.dtype))
            parts.append(a[..., s:s + w])
            o = dst + w
        if width > o:
            parts.append(jnp.zeros(a.shape[:-1] + (width - o,), a.dtype))
        return jnp.concatenate(parts, axis=-1)


def _cumsum_kernel(x_ref, o_ref, *, cb):
    t = x_ref.shape[0]
    row = lax.broadcasted_iota(jnp.int32, (cb, cb), 0)
    col = lax.broadcasted_iota(jnp.int32, (cb, cb), 1)
    tri = (row >= col).astype(F32)
    carry = jnp.zeros((1, x_ref.shape[1]), F32)
    for i in range(t // cb):
        c = jnp.dot(tri, x_ref[i * cb:(i + 1) * cb, :], precision=HIGHEST, preferred_element_type=F32) + carry
        o_ref[i * cb:(i + 1) * cb, :] = c
        carry = c[cb - 1:cb, :]


def _cumsum_time(p, b, t, col_block):
    cb = _pick(t, (256, 128, 64, 32, 16, 8))
    return pl.pallas_call(
        functools.partial(_cumsum_kernel, cb=cb),
        out_shape=jax.ShapeDtypeStruct((b * t, LANES), F32),
        grid=(b,),
        in_specs=[pl.BlockSpec((t, LANES), lambda i: (i, col_block))],
        out_specs=pl.BlockSpec((t, LANES), lambda i: (i, 0)),
        compiler_params=_params("parallel"),
    )(p)


def _stack_heads(q_ref, scale):
    tq = q_ref.shape[0]
    q = q_ref[...] * scale
    return jnp.concatenate([q[:, h * HEAD_DIM:(h + 1) * HEAD_DIM] for h in range(GQA)], axis=0).astype(BF16)


def _unstack_heads(o, o_ref):
    tq = o_ref.shape[0]
    for h in range(GQA):
        o_ref[:, h * HEAD_DIM:(h + 1) * HEAD_DIM] = o[h * tq:(h + 1) * tq, :]


def _fox_prompt_kernel(q_ref, k_ref, v_ref, cq_ref, ck_ref, o_ref, *, tq):
    qi = pl.program_id(2)
    rows = GQA * tq
    qs = _stack_heads(q_ref, HEAD_DIM ** -0.5)
    cq = cq_ref[0].reshape(rows, 1)

    def scores(ki):
        k = k_ref[pl.ds(pl.multiple_of(ki * tq, tq), tq), :].astype(BF16)
        s = lax.dot_general(qs, k, (((1,), (1,)), ((), ())), preferred_element_type=F32)
        ck = ck_ref[0, :, ki]
        return ((s + cq).reshape(GQA, tq, tq) - ck).reshape(rows, tq)

    def update(ki, s, carry):
        m, l, acc = carry
        m_new = jnp.maximum(m, jnp.max(s, axis=-1, keepdims=True))
        alpha = jnp.exp(m - m_new)
        p = jnp.exp(s - m_new)
        v = v_ref[pl.ds(pl.multiple_of(ki * tq, tq), tq), :].astype(BF16)
        acc = acc * alpha + jnp.dot(p.astype(BF16), v, preferred_element_type=F32)
        return m_new, l * alpha + jnp.sum(p, axis=-1, keepdims=True), acc

    init = (jnp.full((rows, 1), -jnp.inf, F32), jnp.zeros((rows, 1), F32), jnp.zeros((rows, HEAD_DIM), F32))
    carry = lax.fori_loop(0, qi, lambda ki, c: update(ki, scores(ki), c), init)
    r = lax.broadcasted_iota(jnp.int32, (GQA, tq, tq), 1)
    c = lax.broadcasted_iota(jnp.int32, (GQA, tq, tq), 2)
    s = jnp.where((c <= r).reshape(rows, tq), scores(qi), -jnp.inf)
    m, l, acc = update(qi, s, carry)
    _unstack_heads(acc / l, o_ref)


def _sb_prompt_kernel(q_ref, k_ref, v_ref, o_ref, *, tq):
    qi = pl.program_id(2)
    rows = GQA * tq
    qs = _stack_heads(q_ref, HEAD_DIM ** -0.5)
    jr = lax.broadcasted_iota(jnp.int32, (tq, tq), 0)
    sc = lax.broadcasted_iota(jnp.int32, (tq, tq), 1)
    later = (jr > sc).astype(F32)

    def block(ki, carry, mask):
        run, acc = carry
        k = k_ref[pl.ds(pl.multiple_of(ki * tq, tq), tq), :].astype(BF16)
        v = v_ref[pl.ds(pl.multiple_of(ki * tq, tq), tq), :].astype(BF16)
        z = lax.dot_general(qs, k, (((1,), (1,)), ((), ())), preferred_element_type=F32)
        log_keep = -(jnp.maximum(z, 0.0) + jnp.log1p(jnp.exp(-jnp.abs(z))))
        if mask is not None:
            log_keep = jnp.where(mask, log_keep, 0.0)
        after = jnp.dot(log_keep, later, precision=HIGHEST, preferred_element_type=F32) + run
        w = jnp.exp(z + log_keep + after)
        if mask is not None:
            w = jnp.where(mask, w, 0.0)
        acc = acc + jnp.dot(w.astype(BF16), v, preferred_element_type=F32)
        return run + jnp.sum(log_keep, axis=-1, keepdims=True), acc

    r3 = lax.broadcasted_iota(jnp.int32, (GQA, tq, tq), 1)
    c3 = lax.broadcasted_iota(jnp.int32, (GQA, tq, tq), 2)
    carry = block(qi, (jnp.zeros((rows, 1), F32), jnp.zeros((rows, HEAD_DIM), F32)), (c3 < r3).reshape(rows, tq))
    run, acc = lax.fori_loop(0, qi, lambda it, c: block(qi - 1 - it, c, None), carry)
    _unstack_heads(acc, o_ref)


def _prompt_attention(p, lay, b, t, d, cq=None, ck=None):
    hkv = d // (GQA * HEAD_DIM)
    tq = _pick(t, (256, 128))
    nq = t // tq
    gw = GQA * HEAD_DIM
    fox = cq is not None
    qn, kn, vn = ("fq", "fk", "fv") if fox else ("sq", "sk", "sv")
    qb, kb, vb = lay.off[qn] // gw, lay.off[kn] // HEAD_DIM, lay.off[vn] // HEAD_DIM
    in_specs = [
        pl.BlockSpec((tq, gw), lambda bi, g, qi: (bi * nq + qi, qb + g)),
        pl.BlockSpec((t, HEAD_DIM), lambda bi, g, qi: (bi, kb + g)),
        pl.BlockSpec((t, HEAD_DIM), lambda bi, g, qi: (bi, vb + g)),
    ]
    args = [p, p, p]
    if fox:
        in_specs += [
            pl.BlockSpec((1, GQA, tq, 1), lambda bi, g, qi: (bi, g, qi, 0)),
            pl.BlockSpec((1, GQA, nq, 1, tq), lambda bi, g, qi: (bi, g, 0, 0, 0)),
        ]
        args += [cq, ck]
        body = functools.partial(_fox_prompt_kernel, tq=tq)
    else:
        body = functools.partial(_sb_prompt_kernel, tq=tq)
    return pl.pallas_call(
        body,
        out_shape=jax.ShapeDtypeStruct((b * t, d), F32),
        grid=(b, hkv, nq),
        in_specs=in_specs,
        out_specs=pl.BlockSpec((tq, gw), lambda bi, g, qi: (bi * nq + qi, g)),
        compiler_params=_params("parallel", "parallel", "arbitrary"),
    )(*args)


DEC_ROWS = 2 * SUBLANES


def _decode_kernel(pt_ref, q_ref, kc_ref, vc_ref, *rest, fox, hkv):
    if fox:
        lf_ref, knew_ref, vnew_ref, lfnew_ref, o_ref, m_ref, l_ref, run_ref, acc_ref = rest
    else:
        o_ref, run_ref, acc_ref = rest
    pg = pl.program_id(1)
    npg = pl.num_programs(1)
    page = kc_ref.shape[2]
    rows = hkv * DEC_ROWS

    @pl.when(pg == 0)
    def _():
        if fox:
            for g in range(hkv):
                sl = slice(g * DEC_ROWS, (g + 1) * DEC_ROWS)
                qg = q_ref[0, sl, :].astype(BF16).astype(F32)
                kg = knew_ref[0, g:g + 1, :].astype(BF16).astype(F32)
                m_ref[sl, :] = jnp.sum(qg * kg, axis=-1, keepdims=True)
                acc_ref[sl, :] = jnp.broadcast_to(vnew_ref[0, g:g + 1, :].astype(BF16).astype(F32), (DEC_ROWS, HEAD_DIM))
            l_ref[...] = jnp.ones_like(l_ref)
            run_ref[...] = lfnew_ref[0]
        else:
            run_ref[...] = jnp.zeros_like(run_ref)
            acc_ref[...] = jnp.zeros_like(acc_ref)

    jr = lax.broadcasted_iota(jnp.int32, (page, page), 0)
    sc = lax.broadcasted_iota(jnp.int32, (page, page), 1)
    later = (jr > sc).astype(F32)
    z = jnp.concatenate(
        [lax.dot_general(q_ref[0, g * DEC_ROWS:(g + 1) * DEC_ROWS, :].astype(BF16), kc_ref[0, 0, :, g, :].astype(BF16),
                         (((1,), (1,)), ((), ())), preferred_element_type=F32) for g in range(hkv)], axis=0)
    run = run_ref[...]
    if fox:
        lf = lf_ref[0, 0]
        s = z + run + jnp.dot(lf, later, precision=HIGHEST, preferred_element_type=F32)
        m = m_ref[...]
        m_new = jnp.maximum(m, jnp.max(s, axis=-1, keepdims=True))
        alpha = jnp.exp(m - m_new)
        w = jnp.exp(s - m_new)
        l_ref[...] = l_ref[...] * alpha + jnp.sum(w, axis=-1, keepdims=True)
        m_ref[...] = m_new
        run_ref[...] = run + jnp.sum(lf, axis=-1, keepdims=True)
    else:
        log_keep = -(jnp.maximum(z, 0.0) + jnp.log1p(jnp.exp(-jnp.abs(z))))
        after = jnp.dot(log_keep, later, precision=HIGHEST, preferred_element_type=F32) + run
        w = jnp.exp(z + log_keep + after)
        alpha = None
        run_ref[...] = run + jnp.sum(log_keep, axis=-1, keepdims=True)
    wb = w.astype(BF16)
    for g in range(hkv):
        sl = slice(g * DEC_ROWS, (g + 1) * DEC_ROWS)
        pv = jnp.dot(wb[sl, :], vc_ref[0, 0, :, g, :].astype(BF16), preferred_element_type=F32)
        if fox:
            acc_ref[sl, :] = acc_ref[sl, :] * alpha[sl, :] + pv
        else:
            acc_ref[sl, :] = acc_ref[sl, :] + pv

    @pl.when(pg == npg - 1)
    def _():
        if fox:
            o_ref[0] = acc_ref[...] / l_ref[...]
        else:
            o_ref[0] = acc_ref[...]


def _decode_attention(layer, page_table, q, kcache, vcache, lf_cache=None, knew=None, vnew=None, lfnew=None):
    bs, npg = page_table.shape
    _, _, page, hkv, _ = kcache.shape
    rows = hkv * DEC_ROWS
    fox = lf_cache is not None
    cache_spec = pl.BlockSpec((1, 1, page, hkv, HEAD_DIM), lambda b, p, pt: (layer, pt[b, npg - 1 - p], 0, 0, 0))
    in_specs = [pl.BlockSpec((1, rows, HEAD_DIM), lambda b, p, pt: (b, 0, 0)), cache_spec, cache_spec]
    args = [q, kcache, vcache]
    scratch = [pltpu.VMEM((rows, 1), F32), pltpu.VMEM((rows, HEAD_DIM), F32)]
    if fox:
        in_specs += [
            pl.BlockSpec((1, 1, rows, page), lambda b, p, pt: (layer, pt[b, npg - 1 - p], 0, 0)),
            pl.BlockSpec((1, hkv, HEAD_DIM), lambda b, p, pt: (b, 0, 0)),
            pl.BlockSpec((1, hkv, HEAD_DIM), lambda b, p, pt: (b, 0, 0)),
            pl.BlockSpec((1, rows, 1), lambda b, p, pt: (b, 0, 0)),
        ]
        args += [lf_cache, knew, vnew, lfnew]
        scratch = [pltpu.VMEM((rows, 1), F32), pltpu.VMEM((rows, 1), F32)] + scratch
    return pl.pallas_call(
        functools.partial(_decode_kernel, fox=fox, hkv=hkv),
        out_shape=jax.ShapeDtypeStruct((bs, rows, HEAD_DIM), F32),
        grid_spec=pltpu.PrefetchScalarGridSpec(
            num_scalar_prefetch=1,
            grid=(bs, npg),
            in_specs=in_specs,
            out_specs=pl.BlockSpec((1, rows, HEAD_DIM), lambda b, p, pt: (b, 0, 0)),
            scratch_shapes=scratch,
        ),
        compiler_params=_params("parallel", "arbitrary"),
    )(page_table, *args)


def _pad_heads(a, hkv):
    bs, _, x = a.shape
    a = a.reshape(bs, hkv, GQA, x)
    a = jnp.concatenate([a, jnp.zeros((bs, hkv, DEC_ROWS - GQA, x), a.dtype)], axis=2)
    return a.reshape(bs, hkv * DEC_ROWS, x)


def _unpad_heads(a, hkv):
    bs, _, x = a.shape
    return a.reshape(bs, hkv, DEC_ROWS, x)[:, :, :GQA].reshape(bs, hkv * GQA * x)


def _shift_kernel(z_ref, prev_ref, st_ref, mu_ref, o_ref, *, tiles_per_seq, sample_tile):
    i = pl.program_id(0)
    z = z_ref[...]
    first = (i % tiles_per_seq) == 0
    row0 = jnp.where(first, 0.0, prev_ref[SUBLANES - 1:SUBLANES, :])
    rid = lax.broadcasted_iota(jnp.int32, z.shape, 0)
    zp = jnp.where(rid == 0, row0, pltpu.roll(z, 1, 0))
    zp = jnp.where(i == sample_tile, st_ref[...], zp)
    o_ref[...] = z + (zp - z) * mu_ref[...]


def _token_shift(p, lay, state_pad, mu, t, n_prompt, tr):
    n = p.shape[0]
    zp = lay.z_width
    tc = _pick(zp, tuple(c for c in (1024, 896, 768, 640, 512, 384, 256, 128) if lay.z_lo % c == 0))
    cb = lay.z_lo // tc
    per8 = tr // SUBLANES
    return pl.pallas_call(
        functools.partial(_shift_kernel, tiles_per_seq=t // tr, sample_tile=n_prompt // tr),
        out_shape=jax.ShapeDtypeStruct((n, zp), F32),
        grid=(n // tr, zp // tc),
        in_specs=[
            pl.BlockSpec((tr, tc), lambda i, c: (i, cb + c)),
            pl.BlockSpec((SUBLANES, tc), lambda i, c: (jnp.maximum(i * per8 - 1, 0), cb + c)),
            pl.BlockSpec((tr, tc), lambda i, c: (0, c)),
            pl.BlockSpec((1, tc), lambda i, c: (0, c)),
        ],
        out_specs=pl.BlockSpec((tr, tc), lambda i, c: (i, c)),
        compiler_params=_params("parallel", "parallel"),
    )(p, p, state_pad, mu)


def _head_sum_matrix(width):
    a = lax.broadcasted_iota(jnp.int32, (width, width), 0) // RWKV_HEAD
    b = lax.broadcasted_iota(jnp.int32, (width, width), 1) // RWKV_HEAD
    return (a == b).astype(F32)


def _rwkv_prep_kernel(k_ref, zw_ref, za_ref, zg_ref, w2_ref, a2_ref, g2_ref, w0_ref, a0_ref, kk_ref, ka_ref,
                      dec_ref, cl_ref, kmod_ref, na_ref, nb_ref, g_ref):
    k = k_ref[...]
    dw = jnp.dot(jnp.tanh(zw_ref[...]).astype(BF16), w2_ref[...], preferred_element_type=F32)
    x = w0_ref[...] + dw
    w_log = _log_sigmoid(x) - 0.5
    log_dec = -jnp.exp(w_log)
    dec_ref[...] = jnp.exp(log_dec)
    tok = lax.broadcasted_iota(jnp.int32, (CHUNK, CHUNK), 0)
    src = lax.broadcasted_iota(jnp.int32, (CHUNK, CHUNK), 1)
    upto = (src <= tok).astype(F32)
    for c in range(k.shape[0] // CHUNK):
        rows = slice(c * CHUNK, (c + 1) * CHUNK)
        cl_ref[rows, :] = jnp.dot(upto, log_dec[rows, :], precision=HIGHEST, preferred_element_type=F32)
    a = _sigmoid(a0_ref[...] + jnp.dot(za_ref[...].astype(BF16), a2_ref[...], preferred_element_type=F32))
    g_ref[...] = jnp.dot(_sigmoid(zg_ref[...]).astype(BF16), g2_ref[...], preferred_element_type=F32)
    kk = k * kk_ref[...]
    hs = _head_sum_matrix(LANES)
    for c in range(k.shape[1] // LANES):
        sl = slice(c * LANES, (c + 1) * LANES)
        kc = kk[:, sl]
        ss = jnp.dot(kc * kc, hs, precision=HIGHEST, preferred_element_type=F32)
        kn = kc / jnp.maximum(jnp.sqrt(ss), 1e-12)
        na_ref[:, sl] = -kn
        nb_ref[:, sl] = kn * a[:, sl]
    kmod_ref[...] = k * (1.0 + (a - 1.0) * ka_ref[...])


def _rwkv_prep(zs, zoff, w2, a2, g2, w0, a0, k_k, k_a, d, tr):
    n = zs.shape[0]
    tc = _pick(d, (512, 256, 128))
    rdp, rip, rgp = w2.shape[0], a2.shape[0], g2.shape[0]
    row = lambda i, c: (i, c)
    vec = pl.BlockSpec((1, tc), lambda i, c: (0, c))
    out = jax.ShapeDtypeStruct((n, d), F32)
    return pl.pallas_call(
        _rwkv_prep_kernel,
        out_shape=[out] * 6,
        grid=(n // tr, d // tc),
        in_specs=[
            pl.BlockSpec((tr, tc), lambda i, c: (i, zoff["k"] // tc + c)),
            pl.BlockSpec((tr, rdp), lambda i, c: (i, zoff["zw"] // rdp)),
            pl.BlockSpec((tr, rip), lambda i, c: (i, zoff["za"] // rip)),
            pl.BlockSpec((tr, rgp), lambda i, c: (i, zoff["zg"] // rgp)),
            pl.BlockSpec((rdp, tc), lambda i, c: (0, c)),
            pl.BlockSpec((rip, tc), lambda i, c: (0, c)),
            pl.BlockSpec((rgp, tc), lambda i, c: (0, c)),
            vec, vec, vec, vec,
        ],
        out_specs=[pl.BlockSpec((tr, tc), row)] * 6,
        compiler_params=_params("parallel", "parallel"),
    )(zs, zs, zs, zs, w2, a2, g2, w0, a0, k_k, k_a)


def _scan_step(state, r, w, k, v, a, b, lo, eye0, eye1):
    def half_sums(x):
        s0 = jnp.sum(jnp.where(lo, x, 0.0), axis=1, keepdims=True)
        s1 = jnp.sum(jnp.where(lo, 0.0, x), axis=1, keepdims=True)
        return s0, s1

    sa0, sa1 = half_sums(state * a)
    vb = jnp.broadcast_to(v, state.shape)
    v0 = jnp.sum(jnp.where(eye0, vb, 0.0), axis=1, keepdims=True)
    v1 = jnp.sum(jnp.where(eye1, vb, 0.0), axis=1, keepdims=True)
    state = state * w + jnp.where(lo, sa0, sa1) * b + jnp.where(lo, v0, v1) * k
    y0, y1 = half_sums(state * r)
    y = jnp.sum(jnp.where(eye0, y0, jnp.where(eye1, y1, 0.0)), axis=0, keepdims=True)
    return state, y


def _scan_masks():
    shape = (RWKV_HEAD, LANES)
    i = lax.broadcasted_iota(jnp.int32, shape, 0)
    c = lax.broadcasted_iota(jnp.int32, shape, 1)
    return c < RWKV_HEAD, c == i, c == i + RWKV_HEAD


def _load_state(s_ref, p):
    return jnp.concatenate([s_ref[0, 2 * p], s_ref[0, 2 * p + 1]], axis=-1)


def _store_state(s_ref, p, state):
    s_ref[0, 2 * p] = state[:, :RWKV_HEAD]
    s_ref[0, 2 * p + 1] = state[:, RWKV_HEAD:]


def _mm(a, b):
    return jnp.dot(a.astype(BF16), b.astype(BF16), preferred_element_type=F32)


def _mm_nt(a, b):
    return lax.dot_general(a.astype(BF16), b.astype(BF16), (((1,), (1,)), ((), ())), preferred_element_type=F32)


def _mm_tn(a, b):
    return lax.dot_general(a.astype(BF16), b.astype(BF16), (((0,), (0,)), ((), ())), preferred_element_type=F32)


def _scan_chunk(sd, r, cl, k, v, a, b, lo, strict, incl, same_head):
    def blockdiag(x):
        return jnp.concatenate([jnp.where(lo, x, 0.0), jnp.where(lo, 0.0, x)], axis=0)

    pairs = range(len(sd))
    first = lax.broadcasted_iota(jnp.int32, cl[0].shape, 0) == 0
    g = [jnp.exp(cl[p]) for p in pairs]
    g_prev = [jnp.exp(jnp.where(first, 0.0, pltpu.roll(cl[p], 1, 0))) for p in pairs]
    g_inv = [jnp.exp(-cl[p]) for p in pairs]
    bt = [b[p] * g_inv[p] for p in pairs]
    kt = [k[p] * g_inv[p] for p in pairs]
    lhs = [jnp.concatenate([a[p] * g_prev[p], r[p] * g[p]], axis=0) for p in pairs]
    keys = [jnp.concatenate([jnp.where(lo, bt[p], 0.0), jnp.where(lo, 0.0, bt[p]),
                             jnp.where(lo, kt[p], 0.0), jnp.where(lo, 0.0, kt[p])], axis=0) for p in pairs]
    gram = [_mm_nt(lhs[p], keys[p]) for p in pairs]
    from_state = [_mm_nt(lhs[p], sd[p]) for p in pairs]
    nk = [jnp.where(strict, gram[p][:CHUNK, :LANES], 0.0) for p in pairs]
    vd = [blockdiag(v[p]) for p in pairs]
    x = [from_state[p][:CHUNK] + _mm(jnp.where(strict, gram[p][:CHUNK, LANES:], 0.0), vd[p]) for p in pairs]
    steps = CHUNK.bit_length() - 1
    for it in range(steps):
        x = [x[p] + _mm(nk[p], blockdiag(x[p])) for p in pairs]
        if it + 1 < steps:
            nk = [_mm(nk[p], blockdiag(nk[p])) for p in pairs]
    r_bk = [jnp.concatenate([jnp.where(incl, gram[p][CHUNK:, :LANES], 0.0),
                             jnp.where(incl, gram[p][CHUNK:, LANES:], 0.0)], axis=1) for p in pairs]
    y = [from_state[p][CHUNK:] + _mm(r_bk[p], jnp.concatenate([blockdiag(x[p]), vd[p]], axis=0)) for p in pairs]
    g_last = [g[p][CHUNK - 1:CHUNK, :] for p in pairs]
    update = [_mm_tn(jnp.concatenate([x[p], v[p]], axis=0),
                     jnp.concatenate([bt[p] * g_last[p], kt[p] * g_last[p]], axis=0)) for p in pairs]
    return [sd[p] * g_last[p] + jnp.where(same_head, update[p], 0.0) for p in pairs], y


def _scan_prompt_kernel(r_ref, cl_ref, k_ref, v_ref, a_ref, b_ref, y_ref, sout_ref, sd_ref, *, pairs, nchunk):
    tc = pl.program_id(2)

    @pl.when(tc == 0)
    def _():
        sd_ref[...] = jnp.zeros_like(sd_ref)

    lane = lax.broadcasted_iota(jnp.int32, (CHUNK, LANES), 1)
    tok = lax.broadcasted_iota(jnp.int32, (CHUNK, LANES), 0)
    lo = lane < RWKV_HEAD
    src = lane % RWKV_HEAD
    strict, incl = src < tok, src <= tok
    hr = lax.broadcasted_iota(jnp.int32, (LANES, LANES), 0) // RWKV_HEAD
    hc = lax.broadcasted_iota(jnp.int32, (LANES, LANES), 1) // RWKV_HEAD
    same_head = hr == hc

    def chunk(ci, carry):
        rows = pl.ds(pl.multiple_of(ci * CHUNK, CHUNK), CHUNK)
        cols = [slice(p * LANES, (p + 1) * LANES) for p in range(pairs)]
        take = lambda ref: [ref[rows, sl] for sl in cols]
        sd, y = _scan_chunk([sd_ref[p] for p in range(pairs)], take(r_ref), take(cl_ref), take(k_ref), take(v_ref),
                            take(a_ref), take(b_ref), lo, strict, incl, same_head)
        for p in range(pairs):
            sd_ref[p] = sd[p]
            y_ref[rows, cols[p]] = y[p]
        return carry

    lax.fori_loop(0, nchunk, chunk, 0)

    @pl.when(tc == pl.num_programs(2) - 1)
    def _():
        for p in range(pairs):
            sd = sd_ref[p]
            sout_ref[0, 2 * p] = sd[:RWKV_HEAD, :RWKV_HEAD]
            sout_ref[0, 2 * p + 1] = sd[RWKV_HEAD:, RWKV_HEAD:]


def _scan_sample_kernel(r_ref, w_ref, k_ref, v_ref, a_ref, b_ref, s0_ref, y_ref, sout_ref, *, pairs):
    lo, eye0, eye1 = _scan_masks()
    for p in range(pairs):
        sl = slice(p * LANES, (p + 1) * LANES)
        row = lambda ref: ref[0, :, sl]
        state, y = _scan_step(_load_state(s0_ref, p), row(r_ref), row(w_ref), row(k_ref), row(v_ref), row(a_ref),
                              row(b_ref), lo, eye0, eye1)
        _store_state(sout_ref, p, state)
        y_ref[0, :, sl] = y


def _rwkv_scan_prompt(r_src, r_cb, v_cb, cl, kmod, na, nb, b, t, d):
    pairs = _pick(d // LANES, (8, 4, 2, 1))
    wcol = pairs * LANES
    tchunk = _pick(t, (256, 128))
    nt = t // tchunk
    nh = d // RWKV_HEAD
    rowmap = lambda bi, g, tc: (bi * nt + tc, g)
    blk = pl.BlockSpec((tchunk, wcol), rowmap)
    return pl.pallas_call(
        functools.partial(_scan_prompt_kernel, pairs=pairs, nchunk=tchunk // CHUNK),
        out_shape=[jax.ShapeDtypeStruct((b * t, d), F32), jax.ShapeDtypeStruct((b, nh, RWKV_HEAD, RWKV_HEAD), F32)],
        grid=(b, d // wcol, nt),
        in_specs=[
            pl.BlockSpec((tchunk, wcol), lambda bi, g, tc: (bi * nt + tc, r_cb // pairs + g)),
            blk, blk,
            pl.BlockSpec((tchunk, wcol), lambda bi, g, tc: (bi * nt + tc, v_cb // pairs + g)),
            blk, blk,
        ],
        out_specs=[blk, pl.BlockSpec((1, 2 * pairs, RWKV_HEAD, RWKV_HEAD), lambda bi, g, tc: (bi, g, 0, 0))],
        scratch_shapes=[pltpu.VMEM((pairs, LANES, LANES), F32)],
        compiler_params=_params("parallel", "parallel", "arbitrary"),
    )(r_src, cl, kmod, r_src, na, nb)


def _rwkv_scan_sample(r, dec, kmod, v, na, nb, s0):
    bs, _, d = r.shape
    pairs = _pick(d // LANES, (4, 2, 1))
    wcol = pairs * LANES
    blk = pl.BlockSpec((1, 1, wcol), lambda bi, g: (bi, 0, g))
    sblk = pl.BlockSpec((1, 2 * pairs, RWKV_HEAD, RWKV_HEAD), lambda bi, g: (bi, g, 0, 0))
    return pl.pallas_call(
        functools.partial(_scan_sample_kernel, pairs=pairs),
        out_shape=[jax.ShapeDtypeStruct((bs, 1, d), F32), jax.ShapeDtypeStruct(s0.shape, F32)],
        grid=(bs, d // wcol),
        in_specs=[blk] * 6 + [sblk],
        out_specs=[blk, sblk],
        compiler_params=_params("parallel", "parallel"),
    )(r, dec, kmod, v, na, nb, s0)


def _rwkv_post_kernel(y_ref, r_ref, k_ref, v_ref, g_ref, rk_ref, lw_ref, lb_ref, o_ref):
    hs = _head_sum_matrix(LANES)
    inv = 1.0 / RWKV_HEAD
    for c in range(y_ref.shape[1] // LANES):
        sl = slice(c * LANES, (c + 1) * LANES)
        y = y_ref[:, sl]
        mu = jnp.dot(y, hs, precision=HIGHEST, preferred_element_type=F32) * inv
        yc = y - mu
        var = jnp.dot(yc * yc, hs, precision=HIGHEST, preferred_element_type=F32) * inv
        yn = yc * lax.rsqrt(var + RWKV_GN_EPS) * lw_ref[:, sl] + lb_ref[:, sl]
        v = v_ref[:, sl]
        bonus = jnp.dot(r_ref[:, sl] * k_ref[:, sl] * rk_ref[:, sl], hs, precision=HIGHEST,
                        preferred_element_type=F32) * v
        o_ref[:, sl] = (yn + bonus) * g_ref[:, sl]


def _rwkv_post(y, zs, zoff, kmod, g, r_k, ln_w, ln_b, d, tr):
    n = y.shape[0]
    tc = _pick(d, (512, 256, 128))
    blk = pl.BlockSpec((tr, tc), lambda i, c: (i, c))
    vec = pl.BlockSpec((1, tc), lambda i, c: (0, c))
    return pl.pallas_call(
        _rwkv_post_kernel,
        out_shape=jax.ShapeDtypeStruct((n, d), F32),
        grid=(n // tr, d // tc),
        in_specs=[
            blk,
            pl.BlockSpec((tr, tc), lambda i, c: (i, zoff["r"] // tc + c)),
            blk,
            pl.BlockSpec((tr, tc), lambda i, c: (i, zoff["v"] // tc + c)),
            blk, vec, vec, vec,
        ],
        out_specs=blk,
        compiler_params=_params("parallel", "parallel"),
    )(y, zs, kmod, zs, g, r_k, ln_w, ln_b)


def _merge_kernel(gf_ref, gr_ref, gs_ref, of_ref, or_ref, os_ref, o_ref):
    o_ref[...] = (gf_ref[...] * of_ref[...] + gr_ref[...] * or_ref[...] + gs_ref[...] * os_ref[...]).astype(BF16)


def _merge(p, lay, o_fox, o_rwkv, o_sb, d, tr):
    n = p.shape[0]
    tc = _pick(d, tuple(c for c in (1024, 512, 256, 128) if lay.off["gates"] % c == 0))
    gb = lay.off["gates"] // tc
    per = d // tc
    blk = pl.BlockSpec((tr, tc), lambda i, c: (i, c))
    gate = lambda which: pl.BlockSpec((tr, tc), lambda i, c: (i, gb + which * per + c))
    return pl.pallas_call(
        _merge_kernel,
        out_shape=jax.ShapeDtypeStruct((n, d), BF16),
        grid=(n // tr, d // tc),
        in_specs=[gate(0), gate(1), gate(2), blk, blk, blk],
        out_specs=blk,
        compiler_params=_params("parallel", "parallel"),
    )(p, p, p, o_fox, o_rwkv, o_sb)


def _outproj_kernel(m_ref, w_ref, x_ref, o_ref):
    o_ref[...] = x_ref[...] + jnp.dot(m_ref[...], w_ref[...], preferred_element_type=F32)


def _outproj(merged, w, x, tm):
    n, d = merged.shape
    tn = _pick(w.shape[1], (512, 256, 128))
    return pl.pallas_call(
        _outproj_kernel,
        out_shape=jax.ShapeDtypeStruct((n, w.shape[1]), F32),
        grid=(w.shape[1] // tn, n // tm),
        in_specs=[
            pl.BlockSpec((tm, d), lambda j, i: (i, 0)),
            pl.BlockSpec((d, tn), lambda j, i: (0, j)),
            pl.BlockSpec((tm, tn), lambda j, i: (i, j)),
        ],
        out_specs=pl.BlockSpec((tm, tn), lambda j, i: (i, j)),
        compiler_params=_params("parallel", "parallel"),
    )(merged, w, x)


def _top16_rows(s):
    nk = s.shape[0]
    idx = lax.broadcasted_iota(jnp.int32, s.shape, 0).astype(F32)
    work = s
    kept = jnp.full(s.shape, -jnp.inf, F32)
    tops = []
    for _ in range(PEER_TOPK):
        m = jnp.max(work, axis=0, keepdims=True)
        first = jnp.min(jnp.where(work == m, idx, float(nk)), axis=0, keepdims=True)
        hit = idx == first
        kept = jnp.where(hit, s, kept)
        work = jnp.where(hit, -jnp.inf, work)
        tops.append(m)
    return jnp.concatenate(tops, axis=0), kept


def _kth_largest(c, kth):
    n = c.shape[0]
    idx = lax.broadcasted_iota(jnp.int32, c.shape, 0).astype(F32)
    m = None
    for _ in range(kth):
        m = jnp.max(c, axis=0, keepdims=True)
        first = jnp.min(jnp.where(c == m, idx, float(n)), axis=0, keepdims=True)
        c = jnp.where(idx == first, -jnp.inf, c)
    return m


STAT_THR, STAT_MAX1, STAT_MAX2, STAT_INVZ = 0, 1, 2, 3


def _peer_gate_kernel(ht_ref, wq_ref, sk_ref, s1_ref, s2_ref, st_ref):
    qt = jnp.dot(wq_ref[0], ht_ref[...], preferred_element_type=F32)
    half = qt.shape[0] // 2
    sc = [jnp.dot(sk_ref[0, p], qt[p * half:(p + 1) * half].astype(BF16), preferred_element_type=F32)
          for p in range(2)]
    t1, s1m = _top16_rows(sc[0])
    t2, s2m = _top16_rows(sc[1])
    tn = t1.shape[1]
    cand = (t1[:, None, :] + t2[None, :, :]).reshape(PEER_TOPK * PEER_TOPK, tn)
    thr = _kth_largest(cand, PEER_TOPK)
    x1 = jnp.exp(t1 - t1[0:1])
    x2 = jnp.exp(t2 - t2[0:1])
    pair = (x1[:, None, :] * x2[None, :, :]).reshape(PEER_TOPK * PEER_TOPK, tn)
    zsum = jnp.sum(jnp.where(cand >= thr, pair, 0.0), axis=0, keepdims=True)
    s1_ref[0] = s1m
    s2_ref[0] = s2m
    st_ref[0] = jnp.concatenate([thr, t1[0:1], t2[0:1], 1.0 / zsum, jnp.zeros((SUBLANES - 4, tn), F32)], axis=0)


def _peer_gate(ht, wq_t, subkeys, tn):
    d, n = ht.shape
    hp, _, nk, half = subkeys.shape
    arr = jax.ShapeDtypeStruct((hp, nk, n), F32)
    blk = pl.BlockSpec((1, nk, tn), lambda i, hh: (hh, 0, i))
    return pl.pallas_call(
        _peer_gate_kernel,
        out_shape=[arr, arr, jax.ShapeDtypeStruct((hp, SUBLANES, n), F32)],
        grid=(n // tn, hp),
        in_specs=[
            pl.BlockSpec((d, tn), lambda i, hh: (0, i)),
            pl.BlockSpec((1, 2 * half, d), lambda i, hh: (hh, 0, 0)),
            pl.BlockSpec((1, 2, nk, half), lambda i, hh: (hh, 0, 0, 0)),
        ],
        out_specs=[blk, blk, pl.BlockSpec((1, SUBLANES, tn), lambda i, hh: (hh, 0, i))],
        compiler_params=_params("parallel", "arbitrary"),
    )(ht, wq_t, subkeys)


def _gelu_exact(x):
    return 0.5 * x * (1.0 + lax.erf(x * (2.0 ** -0.5)))


def _peer_dense_kernel(ht_ref, u_ref, vt_ref, s1_ref, s2_ref, st_ref, o_ref, *, rows_per_step):
    e = pl.program_id(1)
    hp, nk, tn = s2_ref.shape

    @pl.when(e == 0)
    def _():
        o_ref[...] = jnp.zeros_like(o_ref)

    act = _gelu_exact(jnp.dot(u_ref[...], ht_ref[...], preferred_element_type=F32))
    per_group = SUBLANES // rows_per_step
    base = pl.multiple_of((e // per_group) * SUBLANES, SUBLANES)
    sub = e % per_group
    gates = [jnp.zeros((nk, tn), F32) for _ in range(rows_per_step)]
    for hh in range(hp):
        st = st_ref[hh]
        thr, max1 = st[STAT_THR:STAT_THR + 1], st[STAT_MAX1:STAT_MAX1 + 1]
        s2 = s2_ref[hh]
        e2 = jnp.exp(s2 - st[STAT_MAX2:STAT_MAX2 + 1]) * st[STAT_INVZ:STAT_INVZ + 1]
        grp = s1_ref[hh, pl.ds(base, SUBLANES), :]
        s1 = grp[0:rows_per_step]
        for o in range(1, per_group):
            s1 = jnp.where(sub == o, grp[o * rows_per_step:(o + 1) * rows_per_step], s1)
        e1 = jnp.exp(s1 - max1)
        for ii in range(rows_per_step):
            val = s1[ii:ii + 1] + s2
            gates[ii] = gates[ii] + jnp.where(val >= thr, e1[ii:ii + 1] * e2, 0.0)
    coef = jnp.concatenate([(gates[ii] * act[ii * nk:(ii + 1) * nk]).astype(BF16) for ii in range(rows_per_step)],
                           axis=0)
    o_ref[...] += jnp.dot(vt_ref[...], coef, preferred_element_type=F32)


def _peer_dense(ht, u, vt, s1, s2, stats, tn, te):
    d, n = ht.shape
    hp, nk, _ = s1.shape
    ne = u.shape[0]
    once = dict(pipeline_mode=pl.Buffered(1))
    full = pl.BlockSpec((hp, nk, tn), lambda i, e: (0, 0, i), **once)
    return pl.pallas_call(
        functools.partial(_peer_dense_kernel, rows_per_step=te // nk),
        out_shape=jax.ShapeDtypeStruct((d, n), F32),
        grid=(n // tn, ne // te),
        in_specs=[
            pl.BlockSpec((d, tn), lambda i, e: (0, i), **once),
            pl.BlockSpec((te, d), lambda i, e: (e, 0)),
            pl.BlockSpec((d, te), lambda i, e: (0, e)),
            full, full,
            pl.BlockSpec((hp, SUBLANES, tn), lambda i, e: (0, 0, i), **once),
        ],
        out_specs=pl.BlockSpec((d, tn), lambda i, e: (0, i)),
        compiler_params=_params("parallel", "arbitrary"),
    )(ht, u, vt, s1, s2, stats)


def _residual_t_kernel(x_ref, yt_ref, o_ref):
    o_ref[...] = x_ref[...] + yt_ref[...].T


def _add_transposed(x, yt, tm):
    n, d = x.shape
    return pl.pallas_call(
        _residual_t_kernel,
        out_shape=jax.ShapeDtypeStruct((n, d), F32),
        grid=(n // tm,),
        in_specs=[pl.BlockSpec((tm, d), lambda i: (i, 0)), pl.BlockSpec((d, tm), lambda i: (0, i))],
        out_specs=pl.BlockSpec((tm, d), lambda i: (i, 0)),
        compiler_params=_params("parallel"),
    )(x, yt)


def kernel(x_prompt, x_sample, cache_fox_k, cache_fox_v, cache_fox_logf, cache_sb_k, cache_sb_v, state_rwkv, state_rwkv_shift, page_table, norm_mix, w_in, b_forget, fox_q_norm, fox_k_norm, sb_q_norm, sb_k_norm, rwkv_mu, rwkv_w0, rwkv_w2, rwkv_a0, rwkv_a2, rwkv_g2, rwkv_k_k, rwkv_k_a, rwkv_r_k, rwkv_ln_w, rwkv_ln_b, w_out, norm_ffn, peer_wq, peer_subkeys, peer_u, peer_v):
    b, t, d = x_prompt.shape
    bs = x_sample.shape[0]
    depth = w_in.shape[0]
    hq = d // HEAD_DIM
    hkv = hq // GQA
    dkv = hkv * HEAD_DIM
    rd, ri, rg = rwkv_w2.shape[1], rwkv_a2.shape[1], rwkv_g2.shape[1]
    n_prompt = b * t
    tr = _pick(t, (256, 128))
    n = _round_up(n_prompt + bs, tr)
    tm = _pick(n, (768, 512, 384, 256, 128))
    tm_in = _pick(n, (1408, 768, 512, 384, 256, 128))
    tn_in = 512 if d % 2048 == 0 else LANES
    lay = _Layout(d, hq, rd, ri, rg, tn_in)
    zoff = {k: lay.off[k] - lay.z_lo for k in ("r", "k", "v", "zw", "za", "zg")}
    rdp, rip, rgp = _round_up(rd, LANES), _round_up(ri, LANES), _round_up(rg, LANES)
    tn_peer = _pick(n, (768, 256, 128))
    te_peer = 2 * peer_subkeys.shape[3]

    def tile_heads(g, count):
        return jnp.tile(g, count)

    def pad_rows(a, rows):
        return jnp.concatenate([a, jnp.zeros((rows - a.shape[0],) + a.shape[1:], a.dtype)], axis=0)

    def zr_original_order(rows):
        return jnp.concatenate([rows[..., zoff[k]:zoff[k] + w] for k, w in
                                (("r", d), ("k", d), ("v", d), ("zw", rd), ("za", ri), ("zg", rg))], axis=-1)

    def to_z_layout(a):
        parts, o, src = [], 0, 0
        for k, w in (("r", d), ("k", d), ("v", d), ("zw", rd), ("za", ri), ("zg", rg)):
            if zoff[k] > o:
                parts.append(jnp.zeros(a.shape[:-1] + (zoff[k] - o,), a.dtype))
            parts.append(a[..., src:src + w])
            o, src = zoff[k] + w, src + w
        if lay.z_width > o:
            parts.append(jnp.zeros(a.shape[:-1] + (lay.z_width - o,), a.dtype))
        return jnp.concatenate(parts, axis=-1)

    x = pad_rows(jnp.concatenate([x_prompt.reshape(n_prompt, d), x_sample.reshape(bs, d)], axis=0), n)
    n_pool, page = cache_fox_logf.shape[1], cache_fox_logf.shape[2]
    lf_cache = jnp.swapaxes(cache_fox_logf, 2, 3).reshape(depth * n_pool, hq, page)
    lf_cache = _pad_heads(lf_cache, hkv).reshape(depth, n_pool, hkv * DEC_ROWS, page)

    prompt_states, sample_states = [], []
    for l in range(depth):
        h = _rmsnorm(x, norm_mix[l], tm)
        w_l = lay.scatter_cols(w_in[l], lay.ncol).astype(BF16)
        colp = jnp.zeros((lay.n_in,), F32)
        for name, g, cnt in (("fq", fox_q_norm[l], hq), ("fk", fox_k_norm[l], hkv), ("sq", sb_q_norm[l], hq),
                             ("sk", sb_k_norm[l], hkv)):
            s0, w0 = lay.src[name]
            colp = colp.at[s0:s0 + w0].set(tile_heads(g, cnt))
        s0, w0 = lay.src["ff"]
        colp = colp.at[s0:s0 + w0].set(b_forget[l])
        colp = lay.scatter_cols(colp.reshape(1, -1), lay.ncol)
        p = _inproj(h, w_l, colp, jnp.asarray(lay.modes), tm_in, tn_in)

        def cols(name, width, rows=slice(None)):
            return p[rows, lay.off[name]:lay.off[name] + width]

        c = _cumsum_time(p, b, t, lay.off["ff"] // LANES)[:, :hq].reshape(b, t, hq)
        tq = _pick(t, (256, 128))
        c_t = jnp.swapaxes(c, 1, 2)
        cq = c_t.reshape(b, hq, t, 1)
        ck = c_t.reshape(b, hq, t // tq, 1, tq)
        o_fox_p = _prompt_attention(p, lay, b, t, d, cq, ck)
        o_sb_p = _prompt_attention(p, lay, b, t, d)

        srow = slice(n_prompt, n_prompt + bs)
        scale = HEAD_DIM ** -0.5
        fq_s = _pad_heads((cols("fq", d, srow) * scale).reshape(bs, hq, HEAD_DIM), hkv)
        sq_s = _pad_heads((cols("sq", d, srow) * scale).reshape(bs, hq, HEAD_DIM), hkv)
        fk_s = cols("fk", dkv, srow).reshape(bs, hkv, HEAD_DIM)
        fv_s = cols("fv", dkv, srow).reshape(bs, hkv, HEAD_DIM)
        lf_s = cols("ff", hq, srow)
        o_fox_s = _decode_attention(l, page_table, fq_s, cache_fox_k, cache_fox_v, lf_cache, fk_s, fv_s,
                                    _pad_heads(lf_s.reshape(bs, hq, 1), hkv))
        o_sb_s = _decode_attention(l, page_table, sq_s, cache_sb_k, cache_sb_v)
        o_fox = pad_rows(jnp.concatenate([o_fox_p, _unpad_heads(o_fox_s, hkv)], axis=0), n)
        o_sb = pad_rows(jnp.concatenate([o_sb_p, _unpad_heads(o_sb_s, hkv)], axis=0), n)

        state_pad = pad_rows(to_z_layout(state_rwkv_shift[l]), tr)
        zs = _token_shift(p, lay, state_pad, to_z_layout(rwkv_mu[l]).reshape(1, -1), t, n_prompt, tr)
        pad_k = lambda w2, rp: pad_rows(w2, rp).astype(BF16)
        vec = lambda a: a.reshape(1, d)
        dec, cl, kmod, na, nb, gg = _rwkv_prep(zs, zoff, pad_k(rwkv_w2[l], rdp), pad_k(rwkv_a2[l], rip),
                                               pad_k(rwkv_g2[l], rgp), vec(rwkv_w0[l]), vec(rwkv_a0[l]),
                                               vec(rwkv_k_k[l]), vec(rwkv_k_a[l]), d, tr)
        y_p, s_p = _rwkv_scan_prompt(zs, zoff["r"] // LANES, zoff["v"] // LANES, cl, kmod, na, nb, b, t, d)
        one = lambda a: a[srow].reshape(bs, 1, d)
        y_s, s_s = _rwkv_scan_sample(one(zs[:, zoff["r"]:zoff["r"] + d]), one(dec), one(kmod),
                                     one(zs[:, zoff["v"]:zoff["v"] + d]), one(na), one(nb), state_rwkv[l])
        y = pad_rows(jnp.concatenate([y_p, y_s.reshape(bs, d)], axis=0), n)
        o_rwkv = _rwkv_post(y, zs, zoff, kmod, gg, vec(rwkv_r_k[l].reshape(-1)), vec(rwkv_ln_w[l]),
                            vec(rwkv_ln_b[l]), d, tr)

        merged = _merge(p, lay, o_fox, o_rwkv, o_sb, d, tr)
        x = _outproj(merged, w_out[l].astype(BF16), x, tm)

        h2t = _rmsnorm(x, norm_ffn[l], tm, transposed=True)
        wq_t = peer_wq[l].T.reshape(-1, peer_wq.shape[-1] // peer_subkeys.shape[1], d).astype(BF16)
        s1, s2, stats = _peer_gate(h2t, wq_t, peer_subkeys[l].astype(BF16), tn_peer)
        yt = _peer_dense(h2t, peer_u[l].astype(BF16), peer_v[l].T.astype(BF16), s1, s2, stats, tn_peer, te_peer)
        x = _add_transposed(x, yt, tr)

        zr_rows = p[:, lay.z_lo:lay.z_lo + lay.z_width]
        last = jnp.arange(b) * t + (t - 1)
        prompt_states.append((
            cols("fk", dkv, slice(0, n_prompt)).reshape(b, t, hkv, HEAD_DIM),
            cols("fv", dkv, slice(0, n_prompt)).reshape(b, t, hkv, HEAD_DIM),
            cols("ff", hq, slice(0, n_prompt)).reshape(b, t, hq),
            cols("sk", dkv, slice(0, n_prompt)).reshape(b, t, hkv, HEAD_DIM),
            cols("sv", dkv, slice(0, n_prompt)).reshape(b, t, hkv, HEAD_DIM),
            s_p, zr_original_order(zr_rows[last])))
        sample_states.append((
            fk_s.reshape(bs, 1, hkv, HEAD_DIM), fv_s.reshape(bs, 1, hkv, HEAD_DIM), lf_s.reshape(bs, 1, hq),
            cols("sk", dkv, srow).reshape(bs, 1, hkv, HEAD_DIM), cols("sv", dkv, srow).reshape(bs, 1, hkv, HEAD_DIM),
            s_s, zr_original_order(zr_rows[srow])))

    stack = lambda states, i: jnp.stack([s[i] for s in states])
    y_prompt = x[:n_prompt].reshape(b, t, d)
    y_sample = x[n_prompt:n_prompt + bs].reshape(bs, 1, d)
    return ((y_prompt, y_sample) + tuple(stack(prompt_states, i) for i in range(7))
            + tuple(stack(sample_states, i) for i in range(7)))
```

```python
import functools

import numpy as np
import jax
import jax.numpy as jnp
from jax import lax
from jax.experimental import pallas as pl
from jax.experimental.pallas import tpu as pltpu

F32 = jnp.float32
BF16 = jnp.bfloat16
HIGHEST = lax.Precision.HIGHEST

NORM_EPS = 1e-6
RWKV_GN_EPS = 64e-5
LANES = 128
SUBLANES = 8
HEAD_DIM = 128
RWKV_HEAD = 64
CHUNK = 64
GQA = 4
PEER_TOPK = 16
VMEM_LIMIT_BYTES = 56 * 1024 * 1024
PEER_VMEM_LIMIT_BYTES = 60 * 1024 * 1024

MODE_ID, MODE_RMS, MODE_SIGMOID, MODE_LOGSIG = 0, 1, 2, 3


def _params(*sem):
    return pltpu.CompilerParams(dimension_semantics=sem, vmem_limit_bytes=VMEM_LIMIT_BYTES)


def _round_up(n, m):
    return (n + m - 1) // m * m


def _pick(n, cands):
    for c in cands:
        if c <= n and n % c == 0:
            return c
    raise ValueError(f"no tile for {n} in {cands}")


def _sigmoid(x):
    return 1.0 / (1.0 + jnp.exp(-x))


def _dot_by_01(x, m01, m01_first=False):
    hi = x.astype(BF16)
    rest = x - hi.astype(F32)
    mid = rest.astype(BF16)
    lo = (rest - mid.astype(F32)).astype(BF16)
    mm = (lambda piece: jnp.dot(m01, piece, preferred_element_type=F32)) if m01_first else (
        lambda piece: jnp.dot(piece, m01, preferred_element_type=F32))
    return mm(hi) + mm(mid) + mm(lo)


def _log_sigmoid(x):
    return jnp.minimum(x, 0.0) - jnp.log1p(jnp.exp(-jnp.abs(x)))


def _log_sigmoid_bulk(x):
    return jnp.minimum(x, 0.0) - jnp.log(1.0 + jnp.exp(-jnp.abs(x)))


def _rmsnorm_kernel(x_ref, g_ref, o_ref, *, transposed):
    x = x_ref[...]
    ms = jnp.mean(x * x, axis=-1, keepdims=True)
    y = x * lax.rsqrt(ms + NORM_EPS) * g_ref[...]
    o_ref[...] = (y.T if transposed else y).astype(o_ref.dtype)


def _rmsnorm(x, g, tm, transposed=False):
    n, d = x.shape
    return pl.pallas_call(
        functools.partial(_rmsnorm_kernel, transposed=transposed),
        out_shape=jax.ShapeDtypeStruct((d, n) if transposed else (n, d), BF16),
        grid=(n // tm,),
        in_specs=[pl.BlockSpec((tm, d), lambda i: (i, 0)), pl.BlockSpec((1, d), lambda i: (0, 0))],
        out_specs=pl.BlockSpec((d, tm), lambda i: (0, i)) if transposed else pl.BlockSpec((tm, d), lambda i: (i, 0)),
        compiler_params=_params("parallel"),
    )(x, g.reshape(1, d))


def _inproj_kernel(modes_ref, h_ref, w_ref, cp_ref, o_ref):
    acc = jnp.dot(h_ref[...], w_ref[...], preferred_element_type=F32)
    mode = modes_ref[pl.program_id(1)]

    @pl.when(mode == MODE_ID)
    def _():
        o_ref[...] = acc

    @pl.when(mode == MODE_RMS)
    def _():
        for c in range(acc.shape[1] // HEAD_DIM):
            sl = slice(c * HEAD_DIM, (c + 1) * HEAD_DIM)
            xs = acc[:, sl]
            ms = jnp.mean(xs * xs, axis=-1, keepdims=True)
            o_ref[:, sl] = xs * lax.rsqrt(ms + NORM_EPS) * cp_ref[:, sl]

    @pl.when(mode == MODE_SIGMOID)
    def _():
        o_ref[...] = _sigmoid(acc)

    @pl.when(mode == MODE_LOGSIG)
    def _():
        o_ref[...] = _log_sigmoid(acc + cp_ref[...])


def _inproj(h, w, colp, modes, tm, tn):
    n, d = h.shape
    ncol = w.shape[1]
    return pl.pallas_call(
        _inproj_kernel,
        out_shape=jax.ShapeDtypeStruct((n, ncol), F32),
        grid_spec=pltpu.PrefetchScalarGridSpec(
            num_scalar_prefetch=1,
            grid=(n // tm, ncol // tn),
            in_specs=[
                pl.BlockSpec((tm, d), lambda i, j, m: (i, 0)),
                pl.BlockSpec((d, tn), lambda i, j, m: (0, j)),
                pl.BlockSpec((1, tn), lambda i, j, m: (0, j)),
            ],
            out_specs=pl.BlockSpec((tm, tn), lambda i, j, m: (i, j)),
        ),
        compiler_params=_params("parallel", "parallel"),
    )(modes, h, w, colp)


class _Layout:
    def __init__(self, d, hq, rd, ri, rg, tn):
        dkv = d // GQA
        zw = 3 * d + rd + ri + rg
        src = {}
        o = 0
        for name, width in (("fq", d), ("fk", dkv), ("fv", dkv), ("ff", hq), ("sq", d), ("sk", dkv), ("sv", dkv)):
            src[name] = (o, width)
            o += width
        for name, width in (("r", d), ("k", d), ("v", d), ("zw", rd), ("za", ri), ("zg", rg)):
            src[name] = (o, width)
            o += width
        src["gates"] = (o, 3 * d)
        self.n_in = o + 3 * d
        self.zr_width = zw
        order = (("fq", MODE_RMS), ("sq", MODE_RMS), ("fk", MODE_RMS), ("fv", MODE_ID), ("sk", MODE_RMS),
                 ("sv", MODE_ID), ("r", MODE_ID), ("k", MODE_ID), ("v", MODE_ID), ("zw", MODE_ID),
                 ("za", MODE_ID), ("zg", MODE_ID), ("gates", MODE_SIGMOID), ("ff", MODE_LOGSIG))
        self.src = src
        self.off = {}
        self.pieces = []
        seg_modes = []
        o, prev = 0, None
        for name, mode in order:
            o = _round_up(o, tn if mode != prev else _round_up(min(src[name][1], 4 * LANES), LANES))
            self.off[name] = o
            seg_modes.append((o, mode))
            self.pieces.append((o, src[name][0], src[name][1]))
            o += src[name][1]
            prev = mode
        self.ncol = _round_up(o, tn)
        self.z_lo = self.off["r"]
        self.z_width = _round_up(self.off["zg"] + rg, LANES) - self.z_lo
        modes = np.zeros((self.ncol // tn,), np.int32)
        for t in range(self.ncol // tn):
            for so, m in seg_modes:
                if so <= t * tn:
                    modes[t] = m
        self.modes = modes

    def scatter_cols(self, a, width):
        parts, o = [], 0
        for dst, s, w in self.pieces:
            if dst > o:
                parts.append(jnp.zeros(a.shape[:-1] + (dst - o,), a.dtype))
            parts.append(a[..., s:s + w])
            o = dst + w
        if width > o:
            parts.append(jnp.zeros(a.shape[:-1] + (width - o,), a.dtype))
        return jnp.concatenate(parts, axis=-1)


def _cumsum_kernel(x_ref, o_ref, *, cb):
    t = x_ref.shape[0]
    row = lax.broadcasted_iota(jnp.int32, (cb, cb), 0)
    col = lax.broadcasted_iota(jnp.int32, (cb, cb), 1)
    tri = (row >= col).astype(F32)
    carry = jnp.zeros((1, x_ref.shape[1]), F32)
    for i in range(t // cb):
        c = jnp.dot(tri, x_ref[i * cb:(i + 1) * cb, :], precision=HIGHEST, preferred_element_type=F32) + carry
        o_ref[i * cb:(i + 1) * cb, :] = c
        carry = c[cb - 1:cb, :]


def _cumsum_time(p, b, t, col_block):
    cb = _pick(t, (256, 128, 64, 32, 16, 8))
    return pl.pallas_call(
        functools.partial(_cumsum_kernel, cb=cb),
        out_shape=jax.ShapeDtypeStruct((b * t, LANES), F32),
        grid=(b,),
        in_specs=[pl.BlockSpec((t, LANES), lambda i: (i, col_block))],
        out_specs=pl.BlockSpec((t, LANES), lambda i: (i, 0)),
        compiler_params=_params("parallel"),
    )(p)


def _stack_heads(q_ref, scale):
    tq = q_ref.shape[0]
    q = q_ref[...] * scale
    return jnp.concatenate([q[:, h * HEAD_DIM:(h + 1) * HEAD_DIM] for h in range(GQA)], axis=0).astype(BF16)


def _unstack_heads(o, o_ref):
    tq = o_ref.shape[0]
    for h in range(GQA):
        o_ref[:, h * HEAD_DIM:(h + 1) * HEAD_DIM] = o[h * tq:(h + 1) * tq, :]


def _fox_prompt_kernel(q_ref, k_ref, v_ref, cq_ref, ck_ref, o_ref, *, tq):
    qi = pl.program_id(2)
    rows = GQA * tq
    qs = _stack_heads(q_ref, HEAD_DIM ** -0.5)
    cq = cq_ref[0].reshape(rows, 1)

    def scores(ki):
        k = k_ref[pl.ds(pl.multiple_of(ki * tq, tq), tq), :].astype(BF16)
        s = lax.dot_general(qs, k, (((1,), (1,)), ((), ())), preferred_element_type=F32)
        ck = ck_ref[0, :, ki]
        return ((s + cq).reshape(GQA, tq, tq) - ck).reshape(rows, tq)

    def update(ki, s, carry):
        m, l, acc = carry
        m_new = jnp.maximum(m, jnp.max(s, axis=-1, keepdims=True))
        alpha = jnp.exp(m - m_new)
        p = jnp.exp(s - m_new)
        v = v_ref[pl.ds(pl.multiple_of(ki * tq, tq), tq), :].astype(BF16)
        acc = acc * alpha + jnp.dot(p.astype(BF16), v, preferred_element_type=F32)
        return m_new, l * alpha + jnp.sum(p, axis=-1, keepdims=True), acc

    init = (jnp.full((rows, 1), -jnp.inf, F32), jnp.zeros((rows, 1), F32), jnp.zeros((rows, HEAD_DIM), F32))
    carry = lax.fori_loop(0, qi, lambda ki, c: update(ki, scores(ki), c), init)
    r = lax.broadcasted_iota(jnp.int32, (GQA, tq, tq), 1)
    c = lax.broadcasted_iota(jnp.int32, (GQA, tq, tq), 2)
    s = jnp.where((c <= r).reshape(rows, tq), scores(qi), -jnp.inf)
    m, l, acc = update(qi, s, carry)
    _unstack_heads(acc / l, o_ref)


def _sb_prompt_kernel(q_ref, k_ref, v_ref, o_ref, *, tq):
    qi = pl.program_id(2)
    rows = GQA * tq
    qs = _stack_heads(q_ref, HEAD_DIM ** -0.5)
    jr = lax.broadcasted_iota(jnp.int32, (tq, tq), 0)
    sc = lax.broadcasted_iota(jnp.int32, (tq, tq), 1)
    later = (jr > sc).astype(BF16)

    def block(ki, carry, mask):
        run, acc = carry
        k = k_ref[pl.ds(pl.multiple_of(ki * tq, tq), tq), :].astype(BF16)
        v = v_ref[pl.ds(pl.multiple_of(ki * tq, tq), tq), :].astype(BF16)
        z = lax.dot_general(qs, k, (((1,), (1,)), ((), ())), preferred_element_type=F32)
        log_keep = _log_sigmoid_bulk(-z)
        if mask is not None:
            log_keep = jnp.where(mask, log_keep, 0.0)
        after = _dot_by_01(log_keep, later) + run
        w = jnp.exp(z + log_keep + after)
        if mask is not None:
            w = jnp.where(mask, w, 0.0)
        acc = acc + jnp.dot(w.astype(BF16), v, preferred_element_type=F32)
        return run + jnp.sum(log_keep, axis=-1, keepdims=True), acc

    r3 = lax.broadcasted_iota(jnp.int32, (GQA, tq, tq), 1)
    c3 = lax.broadcasted_iota(jnp.int32, (GQA, tq, tq), 2)
    carry = block(qi, (jnp.zeros((rows, 1), F32), jnp.zeros((rows, HEAD_DIM), F32)), (c3 < r3).reshape(rows, tq))
    run, acc = lax.fori_loop(0, qi, lambda it, c: block(qi - 1 - it, c, None), carry)
    _unstack_heads(acc, o_ref)


def _prompt_attention(p, lay, b, t, d, cq=None, ck=None):
    hkv = d // (GQA * HEAD_DIM)
    tq = _pick(t, (256, 128))
    nq = t // tq
    gw = GQA * HEAD_DIM
    fox = cq is not None
    qn, kn, vn = ("fq", "fk", "fv") if fox else ("sq", "sk", "sv")
    qb, kb, vb = lay.off[qn] // gw, lay.off[kn] // HEAD_DIM, lay.off[vn] // HEAD_DIM
    in_specs = [
        pl.BlockSpec((tq, gw), lambda bi, g, qi: (bi * nq + qi, qb + g)),
        pl.BlockSpec((t, HEAD_DIM), lambda bi, g, qi: (bi, kb + g)),
        pl.BlockSpec((t, HEAD_DIM), lambda bi, g, qi: (bi, vb + g)),
    ]
    args = [p, p, p]
    if fox:
        in_specs += [
            pl.BlockSpec((1, GQA, tq, 1), lambda bi, g, qi: (bi, g, qi, 0)),
            pl.BlockSpec((1, GQA, nq, 1, tq), lambda bi, g, qi: (bi, g, 0, 0, 0)),
        ]
        args += [cq, ck]
        body = functools.partial(_fox_prompt_kernel, tq=tq)
    else:
        body = functools.partial(_sb_prompt_kernel, tq=tq)
    return pl.pallas_call(
        body,
        out_shape=jax.ShapeDtypeStruct((p.shape[0], d), F32),
        grid=(b, hkv, nq),
        in_specs=in_specs,
        out_specs=pl.BlockSpec((tq, gw), lambda bi, g, qi: (bi * nq + qi, g)),
        compiler_params=_params("parallel", "parallel", "arbitrary"),
    )(*args)


DEC_ROWS = 2 * SUBLANES


def _decode_kernel(pt_ref, q_ref, kc_ref, vc_ref, *rest, fox, hkv):
    if fox:
        lf_ref, knew_ref, vnew_ref, lfnew_ref, o_ref, m_ref, l_ref, run_ref, acc_ref = rest
    else:
        o_ref, run_ref, acc_ref = rest
    pg = pl.program_id(1)
    npg = pl.num_programs(1)
    page = kc_ref.shape[2]
    rows = hkv * DEC_ROWS

    @pl.when(pg == 0)
    def _():
        if fox:
            for g in range(hkv):
                sl = slice(g * DEC_ROWS, (g + 1) * DEC_ROWS)
                qg = q_ref[0, sl, :].astype(BF16).astype(F32)
                kg = knew_ref[0, g:g + 1, :].astype(BF16).astype(F32)
                m_ref[sl, :] = jnp.sum(qg * kg, axis=-1, keepdims=True)
                acc_ref[sl, :] = jnp.broadcast_to(vnew_ref[0, g:g + 1, :].astype(BF16).astype(F32), (DEC_ROWS, HEAD_DIM))
            l_ref[...] = jnp.ones_like(l_ref)
            run_ref[...] = lfnew_ref[0]
        else:
            run_ref[...] = jnp.zeros_like(run_ref)
            acc_ref[...] = jnp.zeros_like(acc_ref)

    jr = lax.broadcasted_iota(jnp.int32, (page, page), 0)
    sc = lax.broadcasted_iota(jnp.int32, (page, page), 1)
    later = (jr > sc).astype(BF16)
    k_heads = pltpu.einshape("khd->hkd", kc_ref[0, 0])
    v_heads = pltpu.einshape("khd->hkd", vc_ref[0, 0])
    z = jnp.concatenate(
        [lax.dot_general(q_ref[0, g * DEC_ROWS:(g + 1) * DEC_ROWS, :].astype(BF16), k_heads[g].astype(BF16),
                         (((1,), (1,)), ((), ())), preferred_element_type=F32) for g in range(hkv)], axis=0)
    run = run_ref[...]
    if fox:
        lf = lf_ref[0, 0]
        s = z + run + _dot_by_01(lf, later)
        m = m_ref[...]
        m_new = jnp.maximum(m, jnp.max(s, axis=-1, keepdims=True))
        alpha = jnp.exp(m - m_new)
        w = jnp.exp(s - m_new)
        l_ref[...] = l_ref[...] * alpha + jnp.sum(w, axis=-1, keepdims=True)
        m_ref[...] = m_new
        run_ref[...] = run + jnp.sum(lf, axis=-1, keepdims=True)
    else:
        log_keep = _log_sigmoid_bulk(-z)
        after = _dot_by_01(log_keep, later) + run
        w = jnp.exp(z + log_keep + after)
        alpha = None
        run_ref[...] = run + jnp.sum(log_keep, axis=-1, keepdims=True)
    wb = w.astype(BF16)
    for g in range(hkv):
        sl = slice(g * DEC_ROWS, (g + 1) * DEC_ROWS)
        pv = jnp.dot(wb[sl, :], v_heads[g].astype(BF16), preferred_element_type=F32)
        if fox:
            acc_ref[sl, :] = acc_ref[sl, :] * alpha[sl, :] + pv
        else:
            acc_ref[sl, :] = acc_ref[sl, :] + pv

    @pl.when(pg == npg - 1)
    def _():
        if fox:
            o_ref[0] = acc_ref[...] / l_ref[...]
        else:
            o_ref[0] = acc_ref[...]


def _decode_attention(layer, page_table, q, kcache, vcache, lf_cache=None, knew=None, vnew=None, lfnew=None):
    bs, npg = page_table.shape
    _, _, page, hkv, _ = kcache.shape
    rows = hkv * DEC_ROWS
    fox = lf_cache is not None
    cache_spec = pl.BlockSpec((1, 1, page, hkv, HEAD_DIM), lambda b, p, pt: (layer, pt[b, npg - 1 - p], 0, 0, 0))
    in_specs = [pl.BlockSpec((1, rows, HEAD_DIM), lambda b, p, pt: (b, 0, 0)), cache_spec, cache_spec]
    args = [q, kcache, vcache]
    scratch = [pltpu.VMEM((rows, 1), F32), pltpu.VMEM((rows, HEAD_DIM), F32)]
    if fox:
        in_specs += [
            pl.BlockSpec((1, 1, rows, page), lambda b, p, pt: (layer, pt[b, npg - 1 - p], 0, 0)),
            pl.BlockSpec((1, hkv, HEAD_DIM), lambda b, p, pt: (b, 0, 0)),
            pl.BlockSpec((1, hkv, HEAD_DIM), lambda b, p, pt: (b, 0, 0)),
            pl.BlockSpec((1, rows, 1), lambda b, p, pt: (b, 0, 0)),
        ]
        args += [lf_cache, knew, vnew, lfnew]
        scratch = [pltpu.VMEM((rows, 1), F32), pltpu.VMEM((rows, 1), F32)] + scratch
    return pl.pallas_call(
        functools.partial(_decode_kernel, fox=fox, hkv=hkv),
        out_shape=jax.ShapeDtypeStruct((bs, rows, HEAD_DIM), F32),
        grid_spec=pltpu.PrefetchScalarGridSpec(
            num_scalar_prefetch=1,
            grid=(bs, npg),
            in_specs=in_specs,
            out_specs=pl.BlockSpec((1, rows, HEAD_DIM), lambda b, p, pt: (b, 0, 0)),
            scratch_shapes=scratch,
        ),
        compiler_params=_params("parallel", "arbitrary"),
    )(page_table, *args)


def _pad_heads(a, hkv):
    bs, _, x = a.shape
    a = a.reshape(bs, hkv, GQA, x)
    a = jnp.concatenate([a, jnp.zeros((bs, hkv, DEC_ROWS - GQA, x), a.dtype)], axis=2)
    return a.reshape(bs, hkv * DEC_ROWS, x)


def _unpad_heads(a, hkv):
    bs, _, x = a.shape
    return a.reshape(bs, hkv, DEC_ROWS, x)[:, :, :GQA].reshape(bs, hkv * GQA * x)


def _shift_kernel(z_ref, prev_ref, st_ref, mu_ref, o_ref, *, tiles_per_seq, sample_tile):
    i = pl.program_id(0)
    z = z_ref[...]
    first = (i % tiles_per_seq) == 0
    row0 = jnp.where(first, 0.0, prev_ref[SUBLANES - 1:SUBLANES, :])
    rid = lax.broadcasted_iota(jnp.int32, z.shape, 0)
    zp = jnp.where(rid == 0, row0, pltpu.roll(z, 1, 0))
    zp = jnp.where(i == sample_tile, st_ref[...], zp)
    o_ref[...] = z + (zp - z) * mu_ref[...]


def _token_shift(p, lay, state_pad, mu, t, n_prompt, tr):
    n = p.shape[0]
    zp = lay.z_width
    tc = _pick(zp, tuple(c for c in (1024, 896, 768, 640, 512, 384, 256, 128) if lay.z_lo % c == 0))
    cb = lay.z_lo // tc
    per8 = tr // SUBLANES
    return pl.pallas_call(
        functools.partial(_shift_kernel, tiles_per_seq=t // tr, sample_tile=n_prompt // tr),
        out_shape=jax.ShapeDtypeStruct((n, zp), F32),
        grid=(n // tr, zp // tc),
        in_specs=[
            pl.BlockSpec((tr, tc), lambda i, c: (i, cb + c)),
            pl.BlockSpec((SUBLANES, tc), lambda i, c: (jnp.maximum(i * per8 - 1, 0), cb + c)),
            pl.BlockSpec((tr, tc), lambda i, c: (0, c)),
            pl.BlockSpec((1, tc), lambda i, c: (0, c)),
        ],
        out_specs=pl.BlockSpec((tr, tc), lambda i, c: (i, c)),
        compiler_params=_params("parallel", "parallel"),
    )(p, p, state_pad, mu)


def _head_sum_matrix(width):
    a = lax.broadcasted_iota(jnp.int32, (width, width), 0) // RWKV_HEAD
    b = lax.broadcasted_iota(jnp.int32, (width, width), 1) // RWKV_HEAD
    return (a == b).astype(BF16)


def _rwkv_prep_kernel(k_ref, zw_ref, za_ref, zg_ref, w2_ref, a2_ref, g2_ref, w0_ref, a0_ref, kk_ref, ka_ref,
                      dec_ref, cl_ref, kmod_ref, na_ref, nb_ref, g_ref):
    k = k_ref[...]
    dw = jnp.dot(jnp.tanh(zw_ref[...]).astype(BF16), w2_ref[...], preferred_element_type=F32)
    x = w0_ref[...] + dw
    w_log = _log_sigmoid(x) - 0.5
    log_dec = -jnp.exp(w_log)
    dec_ref[...] = jnp.exp(log_dec)
    tok = lax.broadcasted_iota(jnp.int32, (CHUNK, CHUNK), 0)
    src = lax.broadcasted_iota(jnp.int32, (CHUNK, CHUNK), 1)
    upto = (src <= tok).astype(BF16)
    for c in range(k.shape[0] // CHUNK):
        rows = slice(c * CHUNK, (c + 1) * CHUNK)
        cl_ref[rows, :] = _dot_by_01(log_dec[rows, :], upto, m01_first=True)
    a = _sigmoid(a0_ref[...] + jnp.dot(za_ref[...].astype(BF16), a2_ref[...], preferred_element_type=F32))
    g_ref[...] = jnp.dot(_sigmoid(zg_ref[...]).astype(BF16), g2_ref[...], preferred_element_type=F32)
    kk = k * kk_ref[...]
    hs = _head_sum_matrix(LANES)
    for c in range(k.shape[1] // LANES):
        sl = slice(c * LANES, (c + 1) * LANES)
        kc = kk[:, sl]
        ss = _dot_by_01(kc * kc, hs)
        kn = kc / jnp.maximum(jnp.sqrt(ss), 1e-12)
        na_ref[:, sl] = -kn
        nb_ref[:, sl] = kn * a[:, sl]
    kmod_ref[...] = k * (1.0 + (a - 1.0) * ka_ref[...])


def _rwkv_prep(zs, zoff, w2, a2, g2, w0, a0, k_k, k_a, d, tr):
    n = zs.shape[0]
    tc = _pick(d, (512, 256, 128))
    rdp, rip, rgp = w2.shape[0], a2.shape[0], g2.shape[0]
    row = lambda i, c: (i, c)
    vec = pl.BlockSpec((1, tc), lambda i, c: (0, c))
    out = jax.ShapeDtypeStruct((n, d), F32)
    return pl.pallas_call(
        _rwkv_prep_kernel,
        out_shape=[out] * 6,
        grid=(n // tr, d // tc),
        in_specs=[
            pl.BlockSpec((tr, tc), lambda i, c: (i, zoff["k"] // tc + c)),
            pl.BlockSpec((tr, rdp), lambda i, c: (i, zoff["zw"] // rdp)),
            pl.BlockSpec((tr, rip), lambda i, c: (i, zoff["za"] // rip)),
            pl.BlockSpec((tr, rgp), lambda i, c: (i, zoff["zg"] // rgp)),
            pl.BlockSpec((rdp, tc), lambda i, c: (0, c)),
            pl.BlockSpec((rip, tc), lambda i, c: (0, c)),
            pl.BlockSpec((rgp, tc), lambda i, c: (0, c)),
            vec, vec, vec, vec,
        ],
        out_specs=[pl.BlockSpec((tr, tc), row)] * 6,
        compiler_params=_params("parallel", "parallel"),
    )(zs, zs, zs, zs, w2, a2, g2, w0, a0, k_k, k_a)


def _scan_step(state, r, w, k, v, a, b, lo, eye0, eye1):
    def half_sums(x):
        s0 = jnp.sum(jnp.where(lo, x, 0.0), axis=1, keepdims=True)
        s1 = jnp.sum(jnp.where(lo, 0.0, x), axis=1, keepdims=True)
        return s0, s1

    sa0, sa1 = half_sums(state * a)
    vb = jnp.broadcast_to(v, state.shape)
    v0 = jnp.sum(jnp.where(eye0, vb, 0.0), axis=1, keepdims=True)
    v1 = jnp.sum(jnp.where(eye1, vb, 0.0), axis=1, keepdims=True)
    state = state * w + jnp.where(lo, sa0, sa1) * b + jnp.where(lo, v0, v1) * k
    y0, y1 = half_sums(state * r)
    y = jnp.sum(jnp.where(eye0, y0, jnp.where(eye1, y1, 0.0)), axis=0, keepdims=True)
    return state, y


def _scan_masks():
    shape = (RWKV_HEAD, LANES)
    i = lax.broadcasted_iota(jnp.int32, shape, 0)
    c = lax.broadcasted_iota(jnp.int32, shape, 1)
    return c < RWKV_HEAD, c == i, c == i + RWKV_HEAD


def _load_state(s_ref, p):
    return jnp.concatenate([s_ref[0, 2 * p], s_ref[0, 2 * p + 1]], axis=-1)


def _store_state(s_ref, p, state):
    s_ref[0, 2 * p] = state[:, :RWKV_HEAD]
    s_ref[0, 2 * p + 1] = state[:, RWKV_HEAD:]


def _mm(a, b):
    return jnp.dot(a.astype(BF16), b.astype(BF16), preferred_element_type=F32)


def _mm_nt(a, b):
    return lax.dot_general(a.astype(BF16), b.astype(BF16), (((1,), (1,)), ((), ())), preferred_element_type=F32)


def _mm_tn(a, b):
    return lax.dot_general(a.astype(BF16), b.astype(BF16), (((0,), (0,)), ((), ())), preferred_element_type=F32)


def _scan_chunk(sd, r, cl, k, v, a, b, lo, strict, incl, same_head):
    def blockdiag(x):
        return jnp.concatenate([jnp.where(lo, x, 0.0), jnp.where(lo, 0.0, x)], axis=0)

    pairs = range(len(sd))
    first = lax.broadcasted_iota(jnp.int32, cl[0].shape, 0) == 0
    g = [jnp.exp(cl[p]) for p in pairs]
    g_prev = [jnp.exp(jnp.where(first, 0.0, pltpu.roll(cl[p], 1, 0))) for p in pairs]
    g_inv = [jnp.exp(-cl[p]) for p in pairs]
    bt = [b[p] * g_inv[p] for p in pairs]
    kt = [k[p] * g_inv[p] for p in pairs]
    lhs = [jnp.concatenate([a[p] * g_prev[p], r[p] * g[p]], axis=0) for p in pairs]
    keys = [jnp.concatenate([jnp.where(lo, bt[p], 0.0), jnp.where(lo, 0.0, bt[p]),
                             jnp.where(lo, kt[p], 0.0), jnp.where(lo, 0.0, kt[p])], axis=0) for p in pairs]
    gram = [_mm_nt(lhs[p], keys[p]) for p in pairs]
    from_state = [_mm_nt(lhs[p], sd[p]) for p in pairs]
    nk = [jnp.where(strict, gram[p][:CHUNK, :LANES], 0.0) for p in pairs]
    vd = [blockdiag(v[p]) for p in pairs]
    x = [from_state[p][:CHUNK] + _mm(jnp.where(strict, gram[p][:CHUNK, LANES:], 0.0), vd[p]) for p in pairs]
    steps = CHUNK.bit_length() - 1
    for it in range(steps):
        x = [x[p] + _mm(nk[p], blockdiag(x[p])) for p in pairs]
        if it + 1 < steps:
            nk = [_mm(nk[p], blockdiag(nk[p])) for p in pairs]
    r_bk = [jnp.concatenate([jnp.where(incl, gram[p][CHUNK:, :LANES], 0.0),
                             jnp.where(incl, gram[p][CHUNK:, LANES:], 0.0)], axis=1) for p in pairs]
    y = [from_state[p][CHUNK:] + _mm(r_bk[p], jnp.concatenate([blockdiag(x[p]), vd[p]], axis=0)) for p in pairs]
    g_last = [g[p][CHUNK - 1:CHUNK, :] for p in pairs]
    update = [_mm_tn(jnp.concatenate([x[p], v[p]], axis=0),
                     jnp.concatenate([bt[p] * g_last[p], kt[p] * g_last[p]], axis=0)) for p in pairs]
    return [sd[p] * g_last[p] + jnp.where(same_head, update[p], 0.0) for p in pairs], y


def _scan_prompt_kernel(r_ref, cl_ref, k_ref, v_ref, a_ref, b_ref, y_ref, sout_ref, sd_ref, *, pairs, nchunk):
    tc = pl.program_id(2)

    @pl.when(tc == 0)
    def _():
        sd_ref[...] = jnp.zeros_like(sd_ref)

    lane = lax.broadcasted_iota(jnp.int32, (CHUNK, LANES), 1)
    tok = lax.broadcasted_iota(jnp.int32, (CHUNK, LANES), 0)
    lo = lane < RWKV_HEAD
    src = lane % RWKV_HEAD
    strict, incl = src < tok, src <= tok
    hr = lax.broadcasted_iota(jnp.int32, (LANES, LANES), 0) // RWKV_HEAD
    hc = lax.broadcasted_iota(jnp.int32, (LANES, LANES), 1) // RWKV_HEAD
    same_head = hr == hc

    def chunk(ci, carry):
        rows = pl.ds(pl.multiple_of(ci * CHUNK, CHUNK), CHUNK)
        cols = [slice(p * LANES, (p + 1) * LANES) for p in range(pairs)]
        take = lambda ref: [ref[rows, sl] for sl in cols]
        sd, y = _scan_chunk([sd_ref[p] for p in range(pairs)], take(r_ref), take(cl_ref), take(k_ref), take(v_ref),
                            take(a_ref), take(b_ref), lo, strict, incl, same_head)
        for p in range(pairs):
            sd_ref[p] = sd[p]
            y_ref[rows, cols[p]] = y[p]
        return carry

    lax.fori_loop(0, nchunk, chunk, 0)

    @pl.when(tc == pl.num_programs(2) - 1)
    def _():
        for p in range(pairs):
            sd = sd_ref[p]
            sout_ref[0, 2 * p] = sd[:RWKV_HEAD, :RWKV_HEAD]
            sout_ref[0, 2 * p + 1] = sd[RWKV_HEAD:, RWKV_HEAD:]


def _scan_sample_kernel(r_ref, w_ref, k_ref, v_ref, a_ref, b_ref, s0_ref, y_ref, sout_ref, *, pairs):
    lo, eye0, eye1 = _scan_masks()
    for p in range(pairs):
        sl = slice(p * LANES, (p + 1) * LANES)
        row = lambda ref: ref[0, :, sl]
        state, y = _scan_step(_load_state(s0_ref, p), row(r_ref), row(w_ref), row(k_ref), row(v_ref), row(a_ref),
                              row(b_ref), lo, eye0, eye1)
        _store_state(sout_ref, p, state)
        y_ref[0, :, sl] = y


def _rwkv_scan_prompt(r_src, r_cb, v_cb, cl, kmod, na, nb, b, t, d):
    pairs = _pick(d // LANES, (8, 4, 2, 1))
    wcol = pairs * LANES
    tchunk = _pick(t, (256, 128))
    nt = t // tchunk
    nh = d // RWKV_HEAD
    rowmap = lambda bi, g, tc: (bi * nt + tc, g)
    blk = pl.BlockSpec((tchunk, wcol), rowmap)
    return pl.pallas_call(
        functools.partial(_scan_prompt_kernel, pairs=pairs, nchunk=tchunk // CHUNK),
        out_shape=[jax.ShapeDtypeStruct((cl.shape[0], d), F32), jax.ShapeDtypeStruct((b, nh, RWKV_HEAD, RWKV_HEAD), F32)],
        grid=(b, d // wcol, nt),
        in_specs=[
            pl.BlockSpec((tchunk, wcol), lambda bi, g, tc: (bi * nt + tc, r_cb // pairs + g)),
            blk, blk,
            pl.BlockSpec((tchunk, wcol), lambda bi, g, tc: (bi * nt + tc, v_cb // pairs + g)),
            blk, blk,
        ],
        out_specs=[blk, pl.BlockSpec((1, 2 * pairs, RWKV_HEAD, RWKV_HEAD), lambda bi, g, tc: (bi, g, 0, 0))],
        scratch_shapes=[pltpu.VMEM((pairs, LANES, LANES), F32)],
        compiler_params=_params("parallel", "parallel", "arbitrary"),
    )(r_src, cl, kmod, r_src, na, nb)


def _rwkv_scan_sample(r, dec, kmod, v, na, nb, s0):
    bs, _, d = r.shape
    pairs = _pick(d // LANES, (4, 2, 1))
    wcol = pairs * LANES
    blk = pl.BlockSpec((1, 1, wcol), lambda bi, g: (bi, 0, g))
    sblk = pl.BlockSpec((1, 2 * pairs, RWKV_HEAD, RWKV_HEAD), lambda bi, g: (bi, g, 0, 0))
    return pl.pallas_call(
        functools.partial(_scan_sample_kernel, pairs=pairs),
        out_shape=[jax.ShapeDtypeStruct((bs, 1, d), F32), jax.ShapeDtypeStruct(s0.shape, F32)],
        grid=(bs, d // wcol),
        in_specs=[blk] * 6 + [sblk],
        out_specs=[blk, sblk],
        compiler_params=_params("parallel", "parallel"),
    )(r, dec, kmod, v, na, nb, s0)


def _rwkv_post_kernel(y_ref, r_ref, k_ref, v_ref, g_ref, rk_ref, lw_ref, lb_ref, o_ref):
    hs = _head_sum_matrix(LANES)
    inv = 1.0 / RWKV_HEAD
    for c in range(y_ref.shape[1] // LANES):
        sl = slice(c * LANES, (c + 1) * LANES)
        y = y_ref[:, sl]
        mu = _dot_by_01(y, hs) * inv
        yc = y - mu
        var = _dot_by_01(yc * yc, hs) * inv
        yn = yc * lax.rsqrt(var + RWKV_GN_EPS) * lw_ref[:, sl] + lb_ref[:, sl]
        v = v_ref[:, sl]
        bonus = _dot_by_01(r_ref[:, sl] * k_ref[:, sl] * rk_ref[:, sl], hs) * v
        o_ref[:, sl] = (yn + bonus) * g_ref[:, sl]


def _rwkv_post(y, zs, zoff, kmod, g, r_k, ln_w, ln_b, d, tr):
    n = y.shape[0]
    tc = _pick(d, (512, 256, 128))
    blk = pl.BlockSpec((tr, tc), lambda i, c: (i, c))
    vec = pl.BlockSpec((1, tc), lambda i, c: (0, c))
    return pl.pallas_call(
        _rwkv_post_kernel,
        out_shape=jax.ShapeDtypeStruct((n, d), F32),
        grid=(n // tr, d // tc),
        in_specs=[
            blk,
            pl.BlockSpec((tr, tc), lambda i, c: (i, zoff["r"] // tc + c)),
            blk,
            pl.BlockSpec((tr, tc), lambda i, c: (i, zoff["v"] // tc + c)),
            blk, vec, vec, vec,
        ],
        out_specs=blk,
        compiler_params=_params("parallel", "parallel"),
    )(y, zs, kmod, zs, g, r_k, ln_w, ln_b)


def _merge_kernel(gf_ref, gr_ref, gs_ref, of_ref, or_ref, os_ref, o_ref):
    o_ref[...] = (gf_ref[...] * of_ref[...] + gr_ref[...] * or_ref[...] + gs_ref[...] * os_ref[...]).astype(BF16)


def _merge(p, lay, o_fox, o_rwkv, o_sb, d, tr):
    n = p.shape[0]
    tc = _pick(d, tuple(c for c in (1024, 512, 256, 128) if lay.off["gates"] % c == 0))
    gb = lay.off["gates"] // tc
    per = d // tc
    blk = pl.BlockSpec((tr, tc), lambda i, c: (i, c))
    gate = lambda which: pl.BlockSpec((tr, tc), lambda i, c: (i, gb + which * per + c))
    return pl.pallas_call(
        _merge_kernel,
        out_shape=jax.ShapeDtypeStruct((n, d), BF16),
        grid=(n // tr, d // tc),
        in_specs=[gate(0), gate(1), gate(2), blk, blk, blk],
        out_specs=blk,
        compiler_params=_params("parallel", "parallel"),
    )(p, p, p, o_fox, o_rwkv, o_sb)


def _outproj_kernel(m_ref, w_ref, x_ref, o_ref):
    o_ref[...] = x_ref[...] + jnp.dot(m_ref[...], w_ref[...], preferred_element_type=F32)


def _outproj(merged, w, x, tm):
    n, d = merged.shape
    tn = _pick(w.shape[1], (512, 256, 128))
    return pl.pallas_call(
        _outproj_kernel,
        out_shape=jax.ShapeDtypeStruct((n, w.shape[1]), F32),
        grid=(w.shape[1] // tn, n // tm),
        in_specs=[
            pl.BlockSpec((tm, d), lambda j, i: (i, 0)),
            pl.BlockSpec((d, tn), lambda j, i: (0, j)),
            pl.BlockSpec((tm, tn), lambda j, i: (i, j)),
        ],
        out_specs=pl.BlockSpec((tm, tn), lambda j, i: (i, j)),
        compiler_params=_params("parallel", "parallel"),
    )(merged, w, x)


def _top16_rows(s):
    nk = s.shape[0]
    idx = lax.broadcasted_iota(jnp.int32, s.shape, 0).astype(F32)
    work = s
    kept = jnp.full(s.shape, -jnp.inf, F32)
    tops = []
    for _ in range(PEER_TOPK):
        m = jnp.max(work, axis=0, keepdims=True)
        first = jnp.min(jnp.where(work == m, idx, float(nk)), axis=0, keepdims=True)
        hit = idx == first
        kept = jnp.where(hit, s, kept)
        work = jnp.where(hit, -jnp.inf, work)
        tops.append(m)
    return jnp.concatenate(tops, axis=0), kept


def _kth_largest(c, kth):
    n = c.shape[0]
    idx = lax.broadcasted_iota(jnp.int32, c.shape, 0).astype(F32)
    m = None
    for _ in range(kth):
        m = jnp.max(c, axis=0, keepdims=True)
        first = jnp.min(jnp.where(c == m, idx, float(n)), axis=0, keepdims=True)
        c = jnp.where(idx == first, -jnp.inf, c)
    return m


STAT_THR, STAT_MAX, STAT_INVZ = 0, 1, 2


def _peer_gate_kernel(ht_ref, wq_ref, sk_ref, s1_ref, s2_ref, st_ref):
    qt = jnp.dot(wq_ref[0], ht_ref[...], preferred_element_type=F32)
    half = qt.shape[0] // 2
    sc = [jnp.dot(sk_ref[0, p], qt[p * half:(p + 1) * half].astype(BF16), preferred_element_type=F32)
          for p in range(2)]
    t1, s1m = _top16_rows(sc[0])
    t2, s2m = _top16_rows(sc[1])
    tn = t1.shape[1]
    cand = (t1[:, None, :] + t2[None, :, :]).reshape(PEER_TOPK * PEER_TOPK, tn)
    thr = _kth_largest(cand, PEER_TOPK)
    top = t1[0:1] + t2[0:1]
    zsum = jnp.sum(jnp.where(cand >= thr, jnp.exp(cand - top), 0.0), axis=0, keepdims=True)
    s1_ref[0] = s1m
    s2_ref[0] = s2m
    st_ref[0] = jnp.concatenate([thr, top, 1.0 / zsum, jnp.zeros((SUBLANES - 3, tn), F32)], axis=0)


def _peer_gate(ht, wq_t, subkeys, tn):
    d, n = ht.shape
    hp, _, nk, half = subkeys.shape
    arr = jax.ShapeDtypeStruct((hp, nk, n), F32)
    blk = pl.BlockSpec((1, nk, tn), lambda i, hh: (hh, 0, i))
    return pl.pallas_call(
        _peer_gate_kernel,
        out_shape=[arr, arr, jax.ShapeDtypeStruct((hp, SUBLANES, n), F32)],
        grid=(n // tn, hp),
        in_specs=[
            pl.BlockSpec((d, tn), lambda i, hh: (0, i)),
            pl.BlockSpec((1, 2 * half, d), lambda i, hh: (hh, 0, 0)),
            pl.BlockSpec((1, 2, nk, half), lambda i, hh: (hh, 0, 0, 0)),
        ],
        out_specs=[blk, blk, pl.BlockSpec((1, SUBLANES, tn), lambda i, hh: (hh, 0, i))],
        compiler_params=_params("parallel", "arbitrary"),
    )(ht, wq_t, subkeys)


def _gelu_exact(x):
    return 0.5 * x * (1.0 + lax.erf(x * (2.0 ** -0.5)))


def _peer_dense_kernel(ht_ref, u_ref, vt_ref, s1_ref, s2_ref, st_ref, o_ref, coef_ref, *, rows_per_step, lane_tile):
    e = pl.program_id(1)
    hp, nk, tn = s2_ref.shape

    @pl.when(e == 0)
    def _():
        o_ref[...] = jnp.zeros_like(o_ref)

    per_group = SUBLANES // rows_per_step
    base = pl.multiple_of((e // per_group) * SUBLANES, SUBLANES)
    sub = e % per_group
    for lt in range(tn // lane_tile):
        ls = slice(lt * lane_tile, (lt + 1) * lane_tile)
        act = _gelu_exact(jnp.dot(u_ref[...], ht_ref[:, ls], preferred_element_type=F32))
        for ii in range(rows_per_step):
            gate = jnp.zeros((nk, lane_tile), F32)
            for hh in range(hp):
                st = st_ref[hh, :, ls]
                grp = s1_ref[hh, pl.ds(base, SUBLANES), ls]
                s1 = grp[ii:ii + 1]
                for o in range(1, per_group):
                    s1 = jnp.where(sub == o, grp[o * rows_per_step + ii:o * rows_per_step + ii + 1], s1)
                val = s1 + s2_ref[hh, :, ls]
                weight = jnp.exp(val - st[STAT_MAX:STAT_MAX + 1]) * st[STAT_INVZ:STAT_INVZ + 1]
                gate = gate + jnp.where(val >= st[STAT_THR:STAT_THR + 1], weight, 0.0)
            coef_ref[ii * nk:(ii + 1) * nk, ls] = (gate * act[ii * nk:(ii + 1) * nk]).astype(BF16)
    o_ref[...] += jnp.dot(vt_ref[0], coef_ref[...], preferred_element_type=F32)


def _peer_dense(ht, u, vt, s1, s2, stats, tn):
    d, n = ht.shape
    hp, nk, _ = s1.shape
    ne = u.shape[0]
    te = vt.shape[2]
    once = dict(pipeline_mode=pl.Buffered(1))
    full = pl.BlockSpec((hp, nk, tn), lambda i, e: (0, 0, i), **once)
    return pl.pallas_call(
        functools.partial(_peer_dense_kernel, rows_per_step=te // nk, lane_tile=_pick(tn, (256, 128))),
        out_shape=jax.ShapeDtypeStruct((d, n), F32),
        grid=(n // tn, ne // te),
        in_specs=[
            pl.BlockSpec((d, tn), lambda i, e: (0, i), **once),
            pl.BlockSpec((te, d), lambda i, e: (e, 0)),
            pl.BlockSpec((1, d, te), lambda i, e: (e, 0, 0)),
            full, full,
            pl.BlockSpec((hp, SUBLANES, tn), lambda i, e: (0, 0, i), **once),
        ],
        out_specs=pl.BlockSpec((d, tn), lambda i, e: (0, i)),
        scratch_shapes=[pltpu.VMEM((te, tn), BF16)],
        compiler_params=pltpu.CompilerParams(dimension_semantics=("parallel", "arbitrary"),
                                             vmem_limit_bytes=PEER_VMEM_LIMIT_BYTES),
    )(ht, u, vt, s1, s2, stats)


def _residual_t_kernel(x_ref, yt_ref, o_ref):
    o_ref[...] = x_ref[...] + yt_ref[...].T


def _add_transposed(x, yt, tm):
    n, d = x.shape
    return pl.pallas_call(
        _residual_t_kernel,
        out_shape=jax.ShapeDtypeStruct((n, d), F32),
        grid=(n // tm,),
        in_specs=[pl.BlockSpec((tm, d), lambda i: (i, 0)), pl.BlockSpec((d, tm), lambda i: (0, i))],
        out_specs=pl.BlockSpec((tm, d), lambda i: (i, 0)),
        compiler_params=_params("parallel"),
    )(x, yt)


def kernel(x_prompt, x_sample, cache_fox_k, cache_fox_v, cache_fox_logf, cache_sb_k, cache_sb_v, state_rwkv, state_rwkv_shift, page_table, norm_mix, w_in, b_forget, fox_q_norm, fox_k_norm, sb_q_norm, sb_k_norm, rwkv_mu, rwkv_w0, rwkv_w2, rwkv_a0, rwkv_a2, rwkv_g2, rwkv_k_k, rwkv_k_a, rwkv_r_k, rwkv_ln_w, rwkv_ln_b, w_out, norm_ffn, peer_wq, peer_subkeys, peer_u, peer_v):
    b, t, d = x_prompt.shape
    bs = x_sample.shape[0]
    depth = w_in.shape[0]
    hq = d // HEAD_DIM
    hkv = hq // GQA
    dkv = hkv * HEAD_DIM
    rd, ri, rg = rwkv_w2.shape[1], rwkv_a2.shape[1], rwkv_g2.shape[1]
    n_prompt = b * t
    tr = _pick(t, (256, 128))
    n = _round_up(n_prompt + bs, tr)
    tm = _pick(n, (768, 512, 384, 256, 128))
    tm_in = _pick(n, (1408, 768, 512, 384, 256, 128))
    tn_in = 512 if d % 2048 == 0 else LANES
    lay = _Layout(d, hq, rd, ri, rg, tn_in)
    zoff = {k: lay.off[k] - lay.z_lo for k in ("r", "k", "v", "zw", "za", "zg")}
    rdp, rip, rgp = _round_up(rd, LANES), _round_up(ri, LANES), _round_up(rg, LANES)
    tn_peer = _pick(n, (768, 256, 128))
    te_peer = 4 * peer_subkeys.shape[3]

    def tile_heads(g, count):
        return jnp.tile(g, count)

    def pad_rows(a, rows):
        return jnp.concatenate([a, jnp.zeros((rows - a.shape[0],) + a.shape[1:], a.dtype)], axis=0)

    def with_tail(full, sample_rows):
        return full.at[n_prompt:].set(pad_rows(sample_rows, n - n_prompt))

    def one_row(a):
        return a.reshape(bs, 1, d)

    def zr_original_order(rows):
        return jnp.concatenate([rows[..., zoff[k]:zoff[k] + w] for k, w in
                                (("r", d), ("k", d), ("v", d), ("zw", rd), ("za", ri), ("zg", rg))], axis=-1)

    def to_z_layout(a):
        parts, o, src = [], 0, 0
        for k, w in (("r", d), ("k", d), ("v", d), ("zw", rd), ("za", ri), ("zg", rg)):
            if zoff[k] > o:
                parts.append(jnp.zeros(a.shape[:-1] + (zoff[k] - o,), a.dtype))
            parts.append(a[..., src:src + w])
            o, src = zoff[k] + w, src + w
        if lay.z_width > o:
            parts.append(jnp.zeros(a.shape[:-1] + (lay.z_width - o,), a.dtype))
        return jnp.concatenate(parts, axis=-1)

    x = pad_rows(jnp.concatenate([x_prompt.reshape(n_prompt, d), x_sample.reshape(bs, d)], axis=0), n)
    n_pool, page = cache_fox_logf.shape[1], cache_fox_logf.shape[2]
    lf_cache = jnp.swapaxes(cache_fox_logf, 2, 3).reshape(depth * n_pool, hq, page)
    lf_cache = _pad_heads(lf_cache, hkv).reshape(depth, n_pool, hkv * DEC_ROWS, page)

    prompt_states, sample_states = [], []
    for l in range(depth):
        h = _rmsnorm(x, norm_mix[l], tm)
        w_l = lay.scatter_cols(w_in[l], lay.ncol).astype(BF16)
        colp = jnp.zeros((lay.n_in,), F32)
        for name, g, cnt in (("fq", fox_q_norm[l], hq), ("fk", fox_k_norm[l], hkv), ("sq", sb_q_norm[l], hq),
                             ("sk", sb_k_norm[l], hkv)):
            s0, w0 = lay.src[name]
            colp = colp.at[s0:s0 + w0].set(tile_heads(g, cnt))
        s0, w0 = lay.src["ff"]
        colp = colp.at[s0:s0 + w0].set(b_forget[l])
        colp = lay.scatter_cols(colp.reshape(1, -1), lay.ncol)
        p = _inproj(h, w_l, colp, jnp.asarray(lay.modes), tm_in, tn_in)

        def cols(name, width, rows=slice(None)):
            return p[rows, lay.off[name]:lay.off[name] + width]

        c = _cumsum_time(p, b, t, lay.off["ff"] // LANES)[:, :hq].reshape(b, t, hq)
        tq = _pick(t, (256, 128))
        c_t = jnp.swapaxes(c, 1, 2)
        cq = c_t.reshape(b, hq, t, 1)
        ck = c_t.reshape(b, hq, t // tq, 1, tq)
        o_fox_p = _prompt_attention(p, lay, b, t, d, cq, ck)
        o_sb_p = _prompt_attention(p, lay, b, t, d)

        srow = slice(n_prompt, n_prompt + bs)
        scale = HEAD_DIM ** -0.5
        fq_s = _pad_heads((cols("fq", d, srow) * scale).reshape(bs, hq, HEAD_DIM), hkv)
        sq_s = _pad_heads((cols("sq", d, srow) * scale).reshape(bs, hq, HEAD_DIM), hkv)
        fk_s = cols("fk", dkv, srow).reshape(bs, hkv, HEAD_DIM)
        fv_s = cols("fv", dkv, srow).reshape(bs, hkv, HEAD_DIM)
        lf_s = cols("ff", hq, srow)
        o_fox_s = _decode_attention(l, page_table, fq_s, cache_fox_k, cache_fox_v, lf_cache, fk_s, fv_s,
                                    _pad_heads(lf_s.reshape(bs, hq, 1), hkv))
        o_sb_s = _decode_attention(l, page_table, sq_s, cache_sb_k, cache_sb_v)
        o_fox = with_tail(o_fox_p, _unpad_heads(o_fox_s, hkv))
        o_sb = with_tail(o_sb_p, _unpad_heads(o_sb_s, hkv))

        state_pad = pad_rows(to_z_layout(state_rwkv_shift[l]), tr)
        zs = _token_shift(p, lay, state_pad, to_z_layout(rwkv_mu[l]).reshape(1, -1), t, n_prompt, tr)
        pad_k = lambda w2, rp: pad_rows(w2, rp).astype(BF16)
        vec = lambda a: a.reshape(1, d)
        dec, cl, kmod, na, nb, gg = _rwkv_prep(zs, zoff, pad_k(rwkv_w2[l], rdp), pad_k(rwkv_a2[l], rip),
                                               pad_k(rwkv_g2[l], rgp), vec(rwkv_w0[l]), vec(rwkv_a0[l]),
                                               vec(rwkv_k_k[l]), vec(rwkv_k_a[l]), d, tr)
        y_p, s_p = _rwkv_scan_prompt(zs, zoff["r"] // LANES, zoff["v"] // LANES, cl, kmod, na, nb, b, t, d)
        one = lambda a: a[srow].reshape(bs, 1, d)
        zs_s = zs[srow]
        y_s, s_s = _rwkv_scan_sample(one_row(zs_s[:, zoff["r"]:zoff["r"] + d]), one(dec), one(kmod),
                                     one_row(zs_s[:, zoff["v"]:zoff["v"] + d]), one(na), one(nb), state_rwkv[l])
        y = with_tail(y_p, y_s.reshape(bs, d))
        o_rwkv = _rwkv_post(y, zs, zoff, kmod, gg, vec(rwkv_r_k[l].reshape(-1)), vec(rwkv_ln_w[l]),
                            vec(rwkv_ln_b[l]), d, tr)

        merged = _merge(p, lay, o_fox, o_rwkv, o_sb, d, tr)
        x = _outproj(merged, w_out[l].astype(BF16), x, tm)

        h2t = _rmsnorm(x, norm_ffn[l], tm, transposed=True)
        wq_t = peer_wq[l].T.reshape(-1, peer_wq.shape[-1] // peer_subkeys.shape[1], d).astype(BF16)
        s1, s2, stats = _peer_gate(h2t, wq_t, peer_subkeys[l].astype(BF16), tn_peer)
        vt = jnp.swapaxes(peer_v[l].reshape(-1, te_peer, d), 1, 2).astype(BF16)
        yt = _peer_dense(h2t, peer_u[l].astype(BF16), vt, s1, s2, stats, tn_peer)
        x = _add_transposed(x, yt, tr)

        last = jnp.arange(b) * t + (t - 1)
        zr_window = lambda rows: zr_original_order(p[rows][:, lay.z_lo:lay.z_lo + lay.z_width])
        prompt_states.append((
            cols("fk", dkv, slice(0, n_prompt)).reshape(b, t, hkv, HEAD_DIM),
            cols("fv", dkv, slice(0, n_prompt)).reshape(b, t, hkv, HEAD_DIM),
            cols("ff", hq, slice(0, n_prompt)).reshape(b, t, hq),
            cols("sk", dkv, slice(0, n_prompt)).reshape(b, t, hkv, HEAD_DIM),
            cols("sv", dkv, slice(0, n_prompt)).reshape(b, t, hkv, HEAD_DIM),
            s_p, zr_window(last)))
        sample_states.append((
            fk_s.reshape(bs, 1, hkv, HEAD_DIM), fv_s.reshape(bs, 1, hkv, HEAD_DIM), lf_s.reshape(bs, 1, hq),
            cols("sk", dkv, srow).reshape(bs, 1, hkv, HEAD_DIM), cols("sv", dkv, srow).reshape(bs, 1, hkv, HEAD_DIM),
            s_s, zr_window(srow)))

    stack = lambda states, i: jnp.stack([s[i] for s in states])
    y_prompt = x[:n_prompt].reshape(b, t, d)
    y_sample = x[n_prompt:n_prompt + bs].reshape(bs, 1, d)
    return ((y_prompt, y_sample) + tuple(stack(prompt_states, i) for i in range(7))
            + tuple(stack(sample_states, i) for i in range(7)))
```

```python
import functools

import numpy as np
import jax
import jax.numpy as jnp
from jax import lax
from jax.experimental import pallas as pl
from jax.experimental.pallas import tpu as pltpu

F32 = jnp.float32
BF16 = jnp.bfloat16
HIGHEST = lax.Precision.HIGHEST

NORM_EPS = 1e-6
RWKV_GN_EPS = 64e-5
LANES = 128
SUBLANES = 8
HEAD_DIM = 128
RWKV_HEAD = 64
CHUNK = 64
GQA = 4
FOX_Q_TILES = (512, 256, 128)
SB_Q_TILES = (256, 128)
PEER_TOPK = 16
VMEM_LIMIT_BYTES = 56 * 1024 * 1024
PEER_VMEM_LIMIT_BYTES = 60 * 1024 * 1024

MODE_ID, MODE_RMS, MODE_SIGMOID, MODE_LOGSIG = 0, 1, 2, 3


def _params(*sem):
    return pltpu.CompilerParams(dimension_semantics=sem, vmem_limit_bytes=VMEM_LIMIT_BYTES)


def _round_up(n, m):
    return (n + m - 1) // m * m


def _pick(n, cands):
    for c in cands:
        if c <= n and n % c == 0:
            return c
    raise ValueError(f"no tile for {n} in {cands}")


def _sigmoid(x):
    return 1.0 / (1.0 + jnp.exp(-x))


def _dot_by_01(x, m01, m01_first=False, pieces=3):
    mm = (lambda piece: jnp.dot(m01, piece, preferred_element_type=F32)) if m01_first else (
        lambda piece: jnp.dot(piece, m01, preferred_element_type=F32))
    out, rest = None, x
    for _ in range(pieces):
        piece = rest.astype(BF16)
        rest = rest - piece.astype(F32)
        out = mm(piece) if out is None else out + mm(piece)
    return out


def _log_sigmoid(x):
    return jnp.minimum(x, 0.0) - jnp.log1p(jnp.exp(-jnp.abs(x)))


def _log_sigmoid_bulk(x):
    return jnp.minimum(x, 0.0) - jnp.log(1.0 + jnp.exp(-jnp.abs(x)))


def _rmsnorm_kernel(x_ref, g_ref, o_ref, *, transposed):
    x = x_ref[...]
    ms = jnp.mean(x * x, axis=-1, keepdims=True)
    y = x * lax.rsqrt(ms + NORM_EPS) * g_ref[...]
    o_ref[...] = (y.T if transposed else y).astype(o_ref.dtype)


def _rmsnorm(x, g, tm, transposed=False):
    n, d = x.shape
    return pl.pallas_call(
        functools.partial(_rmsnorm_kernel, transposed=transposed),
        out_shape=jax.ShapeDtypeStruct((d, n) if transposed else (n, d), BF16),
        grid=(n // tm,),
        in_specs=[pl.BlockSpec((tm, d), lambda i: (i, 0)), pl.BlockSpec((1, d), lambda i: (0, 0))],
        out_specs=pl.BlockSpec((d, tm), lambda i: (0, i)) if transposed else pl.BlockSpec((tm, d), lambda i: (i, 0)),
        compiler_params=_params("parallel"),
    )(x, g.reshape(1, d))


def _inproj_kernel(modes_ref, h_ref, w_ref, cp_ref, o_ref):
    acc = jnp.dot(h_ref[...], w_ref[...], preferred_element_type=F32)
    mode = modes_ref[pl.program_id(1)]

    @pl.when(mode == MODE_ID)
    def _():
        o_ref[...] = acc

    @pl.when(mode == MODE_RMS)
    def _():
        for c in range(acc.shape[1] // HEAD_DIM):
            sl = slice(c * HEAD_DIM, (c + 1) * HEAD_DIM)
            xs = acc[:, sl]
            ms = jnp.mean(xs * xs, axis=-1, keepdims=True)
            o_ref[:, sl] = xs * lax.rsqrt(ms + NORM_EPS) * cp_ref[:, sl]

    @pl.when(mode == MODE_SIGMOID)
    def _():
        o_ref[...] = _sigmoid(acc)

    @pl.when(mode == MODE_LOGSIG)
    def _():
        o_ref[...] = _log_sigmoid(acc + cp_ref[...])


def _inproj(h, w, colp, modes, tm, tn):
    n, d = h.shape
    ncol = w.shape[1]
    return pl.pallas_call(
        _inproj_kernel,
        out_shape=jax.ShapeDtypeStruct((n, ncol), F32),
        grid_spec=pltpu.PrefetchScalarGridSpec(
            num_scalar_prefetch=1,
            grid=(n // tm, ncol // tn),
            in_specs=[
                pl.BlockSpec((tm, d), lambda i, j, m: (i, 0)),
                pl.BlockSpec((d, tn), lambda i, j, m: (0, j)),
                pl.BlockSpec((1, tn), lambda i, j, m: (0, j)),
            ],
            out_specs=pl.BlockSpec((tm, tn), lambda i, j, m: (i, j)),
        ),
        compiler_params=_params("parallel", "parallel"),
    )(modes, h, w, colp)


class _Layout:
    def __init__(self, d, hq, rd, ri, rg, tn):
        dkv = d // GQA
        zw = 3 * d + rd + ri + rg
        src = {}
        o = 0
        for name, width in (("fq", d), ("fk", dkv), ("fv", dkv), ("ff", hq), ("sq", d), ("sk", dkv), ("sv", dkv)):
            src[name] = (o, width)
            o += width
        for name, width in (("r", d), ("k", d), ("v", d), ("zw", rd), ("za", ri), ("zg", rg)):
            src[name] = (o, width)
            o += width
        src["gates"] = (o, 3 * d)
        self.n_in = o + 3 * d
        self.zr_width = zw
        order = (("fq", MODE_RMS), ("sq", MODE_RMS), ("fk", MODE_RMS), ("fv", MODE_ID), ("sk", MODE_RMS),
                 ("sv", MODE_ID), ("r", MODE_ID), ("k", MODE_ID), ("v", MODE_ID), ("zw", MODE_ID),
                 ("za", MODE_ID), ("zg", MODE_ID), ("gates", MODE_SIGMOID), ("ff", MODE_LOGSIG))
        self.src = src
        self.off = {}
        self.pieces = []
        seg_modes = []
        o, prev = 0, None
        for name, mode in order:
            o = _round_up(o, tn if mode != prev else _round_up(min(src[name][1], 4 * LANES), LANES))
            self.off[name] = o
            seg_modes.append((o, mode))
            self.pieces.append((o, src[name][0], src[name][1]))
            o += src[name][1]
            prev = mode
        self.ncol = _round_up(o, tn)
        self.z_lo = self.off["r"]
        self.z_width = _round_up(self.off["zg"] + rg, LANES) - self.z_lo
        modes = np.zeros((self.ncol // tn,), np.int32)
        for t in range(self.ncol // tn):
            for so, m in seg_modes:
                if so <= t * tn:
                    modes[t] = m
        self.modes = modes

    def scatter_cols(self, a, width):
        parts, o = [], 0
        for dst, s, w in self.pieces:
            if dst > o:
                parts.append(jnp.zeros(a.shape[:-1] + (dst - o,), a.dtype))
            parts.append(a[..., s:s + w])
            o = dst + w
        if width > o:
            parts.append(jnp.zeros(a.shape[:-1] + (width - o,), a.dtype))
        return jnp.concatenate(parts, axis=-1)


def _cumsum_kernel(x_ref, o_ref, *, cb):
    t = x_ref.shape[0]
    row = lax.broadcasted_iota(jnp.int32, (cb, cb), 0)
    col = lax.broadcasted_iota(jnp.int32, (cb, cb), 1)
    tri = (row >= col).astype(F32)
    carry = jnp.zeros((1, x_ref.shape[1]), F32)
    for i in range(t // cb):
        c = jnp.dot(tri, x_ref[i * cb:(i + 1) * cb, :], precision=HIGHEST, preferred_element_type=F32) + carry
        o_ref[i * cb:(i + 1) * cb, :] = c
        carry = c[cb - 1:cb, :]


def _cumsum_time(p, b, t, col_block):
    cb = _pick(t, (256, 128, 64, 32, 16, 8))
    return pl.pallas_call(
        functools.partial(_cumsum_kernel, cb=cb),
        out_shape=jax.ShapeDtypeStruct((b * t, LANES), F32),
        grid=(b,),
        in_specs=[pl.BlockSpec((t, LANES), lambda i: (i, col_block))],
        out_specs=pl.BlockSpec((t, LANES), lambda i: (i, 0)),
        compiler_params=_params("parallel"),
    )(p)


def _stack_heads(q_ref, scale):
    tq = q_ref.shape[0]
    q = q_ref[...] * scale
    return jnp.concatenate([q[:, h * HEAD_DIM:(h + 1) * HEAD_DIM] for h in range(GQA)], axis=0).astype(BF16)


def _unstack_heads(o, o_ref):
    tq = o_ref.shape[0]
    for h in range(GQA):
        o_ref[:, h * HEAD_DIM:(h + 1) * HEAD_DIM] = o[h * tq:(h + 1) * tq, :]


def _fox_prompt_kernel(q_ref, k_ref, v_ref, cq_ref, ck_ref, o_ref, *, tq):
    qi = pl.program_id(2)
    rows = GQA * tq
    qs = _stack_heads(q_ref, HEAD_DIM ** -0.5)
    cq = cq_ref[0].reshape(rows, 1)

    def scores(ki):
        k = k_ref[pl.ds(pl.multiple_of(ki * tq, tq), tq), :].astype(BF16)
        s = lax.dot_general(qs, k, (((1,), (1,)), ((), ())), preferred_element_type=F32)
        ck = ck_ref[0, :, ki]
        return ((s + cq).reshape(GQA, tq, tq) - ck).reshape(rows, tq)

    def update(ki, s, carry):
        m, l, acc = carry
        m_new = jnp.maximum(m, jnp.max(s, axis=-1, keepdims=True))
        alpha = jnp.exp(m - m_new)
        p = jnp.exp(s - m_new)
        v = v_ref[pl.ds(pl.multiple_of(ki * tq, tq), tq), :].astype(BF16)
        acc = acc * alpha + jnp.dot(p.astype(BF16), v, preferred_element_type=F32)
        return m_new, l * alpha + jnp.sum(p, axis=-1, keepdims=True), acc

    init = (jnp.full((rows, 1), -jnp.inf, F32), jnp.zeros((rows, 1), F32), jnp.zeros((rows, HEAD_DIM), F32))
    carry = lax.fori_loop(0, qi, lambda ki, c: update(ki, scores(ki), c), init)
    r = lax.broadcasted_iota(jnp.int32, (GQA, tq, tq), 1)
    c = lax.broadcasted_iota(jnp.int32, (GQA, tq, tq), 2)
    s = jnp.where((c <= r).reshape(rows, tq), scores(qi), -jnp.inf)
    m, l, acc = update(qi, s, carry)
    _unstack_heads(acc / l, o_ref)


def _sb_prompt_kernel(q_ref, k_ref, v_ref, o_ref, *, tq):
    qi = pl.program_id(2)
    rows = GQA * tq
    qs = _stack_heads(q_ref, HEAD_DIM ** -0.5)
    jr = lax.broadcasted_iota(jnp.int32, (tq, tq), 0)
    sc = lax.broadcasted_iota(jnp.int32, (tq, tq), 1)
    later = (jr > sc).astype(BF16)

    def block(ki, carry, mask):
        run, acc = carry
        k = k_ref[pl.ds(pl.multiple_of(ki * tq, tq), tq), :].astype(BF16)
        v = v_ref[pl.ds(pl.multiple_of(ki * tq, tq), tq), :].astype(BF16)
        z = lax.dot_general(qs, k, (((1,), (1,)), ((), ())), preferred_element_type=F32)
        log_keep = _log_sigmoid_bulk(-z)
        if mask is not None:
            log_keep = jnp.where(mask, log_keep, 0.0)
        after = _dot_by_01(log_keep, later, pieces=2) + run
        w = jnp.exp(z + log_keep + after)
        if mask is not None:
            w = jnp.where(mask, w, 0.0)
        acc = acc + jnp.dot(w.astype(BF16), v, preferred_element_type=F32)
        return run + jnp.sum(log_keep, axis=-1, keepdims=True), acc

    r3 = lax.broadcasted_iota(jnp.int32, (GQA, tq, tq), 1)
    c3 = lax.broadcasted_iota(jnp.int32, (GQA, tq, tq), 2)
    carry = block(qi, (jnp.zeros((rows, 1), F32), jnp.zeros((rows, HEAD_DIM), F32)), (c3 < r3).reshape(rows, tq))
    run, acc = lax.fori_loop(0, qi, lambda it, c: block(qi - 1 - it, c, None), carry)
    _unstack_heads(acc, o_ref)


def _prompt_attention(p, lay, b, t, d, cq=None, ck=None):
    hkv = d // (GQA * HEAD_DIM)
    fox = cq is not None
    tq = ck.shape[-1] if fox else _pick(t, SB_Q_TILES)
    nq = t // tq
    gw = GQA * HEAD_DIM
    qn, kn, vn = ("fq", "fk", "fv") if fox else ("sq", "sk", "sv")
    qb, kb, vb = lay.off[qn] // gw, lay.off[kn] // HEAD_DIM, lay.off[vn] // HEAD_DIM
    in_specs = [
        pl.BlockSpec((tq, gw), lambda bi, g, qi: (bi * nq + qi, qb + g)),
        pl.BlockSpec((t, HEAD_DIM), lambda bi, g, qi: (bi, kb + g)),
        pl.BlockSpec((t, HEAD_DIM), lambda bi, g, qi: (bi, vb + g)),
    ]
    args = [p, p, p]
    if fox:
        in_specs += [
            pl.BlockSpec((1, GQA, tq, 1), lambda bi, g, qi: (bi, g, qi, 0)),
            pl.BlockSpec((1, GQA, nq, 1, tq), lambda bi, g, qi: (bi, g, 0, 0, 0)),
        ]
        args += [cq, ck]
        body = functools.partial(_fox_prompt_kernel, tq=tq)
    else:
        body = functools.partial(_sb_prompt_kernel, tq=tq)
    return pl.pallas_call(
        body,
        out_shape=jax.ShapeDtypeStruct((p.shape[0], d), F32),
        grid=(b, hkv, nq),
        in_specs=in_specs,
        out_specs=pl.BlockSpec((tq, gw), lambda bi, g, qi: (bi * nq + qi, g)),
        compiler_params=_params("parallel", "parallel", "arbitrary"),
    )(*args)


DEC_ROWS = 2 * SUBLANES


def _decode_kernel(pt_ref, q_ref, kc_ref, vc_ref, *rest, fox, hkv):
    if fox:
        lf_ref, knew_ref, vnew_ref, lfnew_ref, o_ref, m_ref, l_ref, run_ref, acc_ref = rest
    else:
        o_ref, run_ref, acc_ref = rest
    pg = pl.program_id(1)
    npg = pl.num_programs(1)
    page = kc_ref.shape[2]
    rows = hkv * DEC_ROWS

    @pl.when(pg == 0)
    def _():
        if fox:
            for g in range(hkv):
                sl = slice(g * DEC_ROWS, (g + 1) * DEC_ROWS)
                qg = q_ref[0, sl, :].astype(BF16).astype(F32)
                kg = knew_ref[0, g:g + 1, :].astype(BF16).astype(F32)
                m_ref[sl, :] = jnp.sum(qg * kg, axis=-1, keepdims=True)
                acc_ref[sl, :] = jnp.broadcast_to(vnew_ref[0, g:g + 1, :].astype(BF16).astype(F32), (DEC_ROWS, HEAD_DIM))
            l_ref[...] = jnp.ones_like(l_ref)
            run_ref[...] = lfnew_ref[0]
        else:
            run_ref[...] = jnp.zeros_like(run_ref)
            acc_ref[...] = jnp.zeros_like(acc_ref)

    jr = lax.broadcasted_iota(jnp.int32, (page, page), 0)
    sc = lax.broadcasted_iota(jnp.int32, (page, page), 1)
    later = (jr > sc).astype(BF16)
    k_heads = pltpu.einshape("khd->hkd", kc_ref[0, 0])
    v_heads = pltpu.einshape("khd->hkd", vc_ref[0, 0])
    z = jnp.concatenate(
        [lax.dot_general(q_ref[0, g * DEC_ROWS:(g + 1) * DEC_ROWS, :].astype(BF16), k_heads[g].astype(BF16),
                         (((1,), (1,)), ((), ())), preferred_element_type=F32) for g in range(hkv)], axis=0)
    run = run_ref[...]
    if fox:
        lf = lf_ref[0, 0]
        s = z + run + _dot_by_01(lf, later)
        m = m_ref[...]
        m_new = jnp.maximum(m, jnp.max(s, axis=-1, keepdims=True))
        alpha = jnp.exp(m - m_new)
        w = jnp.exp(s - m_new)
        l_ref[...] = l_ref[...] * alpha + jnp.sum(w, axis=-1, keepdims=True)
        m_ref[...] = m_new
        run_ref[...] = run + jnp.sum(lf, axis=-1, keepdims=True)
    else:
        log_keep = _log_sigmoid_bulk(-z)
        after = _dot_by_01(log_keep, later) + run
        w = jnp.exp(z + log_keep + after)
        alpha = None
        run_ref[...] = run + jnp.sum(log_keep, axis=-1, keepdims=True)
    wb = w.astype(BF16)
    for g in range(hkv):
        sl = slice(g * DEC_ROWS, (g + 1) * DEC_ROWS)
        pv = jnp.dot(wb[sl, :], v_heads[g].astype(BF16), preferred_element_type=F32)
        if fox:
            acc_ref[sl, :] = acc_ref[sl, :] * alpha[sl, :] + pv
        else:
            acc_ref[sl, :] = acc_ref[sl, :] + pv

    @pl.when(pg == npg - 1)
    def _():
        if fox:
            o_ref[0] = acc_ref[...] / l_ref[...]
        else:
            o_ref[0] = acc_ref[...]


def _decode_attention(layer, page_table, q, kcache, vcache, lf_cache=None, knew=None, vnew=None, lfnew=None):
    bs, npg = page_table.shape
    _, _, page, hkv, _ = kcache.shape
    rows = hkv * DEC_ROWS
    fox = lf_cache is not None
    cache_spec = pl.BlockSpec((1, 1, page, hkv, HEAD_DIM), lambda b, p, pt: (layer, pt[b, npg - 1 - p], 0, 0, 0))
    in_specs = [pl.BlockSpec((1, rows, HEAD_DIM), lambda b, p, pt: (b, 0, 0)), cache_spec, cache_spec]
    args = [q, kcache, vcache]
    scratch = [pltpu.VMEM((rows, 1), F32), pltpu.VMEM((rows, HEAD_DIM), F32)]
    if fox:
        in_specs += [
            pl.BlockSpec((1, 1, rows, page), lambda b, p, pt: (layer, pt[b, npg - 1 - p], 0, 0)),
            pl.BlockSpec((1, hkv, HEAD_DIM), lambda b, p, pt: (b, 0, 0)),
            pl.BlockSpec((1, hkv, HEAD_DIM), lambda b, p, pt: (b, 0, 0)),
            pl.BlockSpec((1, rows, 1), lambda b, p, pt: (b, 0, 0)),
        ]
        args += [lf_cache, knew, vnew, lfnew]
        scratch = [pltpu.VMEM((rows, 1), F32), pltpu.VMEM((rows, 1), F32)] + scratch
    return pl.pallas_call(
        functools.partial(_decode_kernel, fox=fox, hkv=hkv),
        out_shape=jax.ShapeDtypeStruct((bs, rows, HEAD_DIM), F32),
        grid_spec=pltpu.PrefetchScalarGridSpec(
            num_scalar_prefetch=1,
            grid=(bs, npg),
            in_specs=in_specs,
            out_specs=pl.BlockSpec((1, rows, HEAD_DIM), lambda b, p, pt: (b, 0, 0)),
            scratch_shapes=scratch,
        ),
        compiler_params=_params("parallel", "arbitrary"),
    )(page_table, *args)


def _pad_heads(a, hkv):
    bs, _, x = a.shape
    a = a.reshape(bs, hkv, GQA, x)
    a = jnp.concatenate([a, jnp.zeros((bs, hkv, DEC_ROWS - GQA, x), a.dtype)], axis=2)
    return a.reshape(bs, hkv * DEC_ROWS, x)


def _unpad_heads(a, hkv):
    bs, _, x = a.shape
    return a.reshape(bs, hkv, DEC_ROWS, x)[:, :, :GQA].reshape(bs, hkv * GQA * x)


def _shift_kernel(z_ref, prev_ref, st_ref, mu_ref, o_ref, *, tiles_per_seq, sample_tile):
    i = pl.program_id(0)
    z = z_ref[...]
    first = (i % tiles_per_seq) == 0
    row0 = jnp.where(first, 0.0, prev_ref[SUBLANES - 1:SUBLANES, :])
    rid = lax.broadcasted_iota(jnp.int32, z.shape, 0)
    zp = jnp.where(rid == 0, row0, pltpu.roll(z, 1, 0))
    zp = jnp.where(i == sample_tile, st_ref[...], zp)
    o_ref[...] = z + (zp - z) * mu_ref[...]


def _token_shift(p, lay, state_pad, mu, t, n_prompt, tr):
    n = p.shape[0]
    zp = lay.z_width
    tc = _pick(zp, tuple(c for c in (1024, 896, 768, 640, 512, 384, 256, 128) if lay.z_lo % c == 0))
    cb = lay.z_lo // tc
    per8 = tr // SUBLANES
    return pl.pallas_call(
        functools.partial(_shift_kernel, tiles_per_seq=t // tr, sample_tile=n_prompt // tr),
        out_shape=jax.ShapeDtypeStruct((n, zp), F32),
        grid=(n // tr, zp // tc),
        in_specs=[
            pl.BlockSpec((tr, tc), lambda i, c: (i, cb + c)),
            pl.BlockSpec((SUBLANES, tc), lambda i, c: (jnp.maximum(i * per8 - 1, 0), cb + c)),
            pl.BlockSpec((tr, tc), lambda i, c: (0, c)),
            pl.BlockSpec((1, tc), lambda i, c: (0, c)),
        ],
        out_specs=pl.BlockSpec((tr, tc), lambda i, c: (i, c)),
        compiler_params=_params("parallel", "parallel"),
    )(p, p, state_pad, mu)


def _head_sum_matrix(width):
    a = lax.broadcasted_iota(jnp.int32, (width, width), 0) // RWKV_HEAD
    b = lax.broadcasted_iota(jnp.int32, (width, width), 1) // RWKV_HEAD
    return (a == b).astype(BF16)


def _rwkv_prep_kernel(k_ref, zw_ref, za_ref, zg_ref, w2_ref, a2_ref, g2_ref, w0_ref, a0_ref, kk_ref, ka_ref,
                      dec_ref, cl_ref, kmod_ref, na_ref, nb_ref, g_ref):
    k = k_ref[...]
    dw = jnp.dot(jnp.tanh(zw_ref[...]).astype(BF16), w2_ref[...], preferred_element_type=F32)
    x = w0_ref[...] + dw
    w_log = _log_sigmoid(x) - 0.5
    log_dec = -jnp.exp(w_log)
    dec_ref[...] = jnp.exp(log_dec)
    tok = lax.broadcasted_iota(jnp.int32, (CHUNK, CHUNK), 0)
    src = lax.broadcasted_iota(jnp.int32, (CHUNK, CHUNK), 1)
    upto = (src <= tok).astype(BF16)
    for c in range(k.shape[0] // CHUNK):
        rows = slice(c * CHUNK, (c + 1) * CHUNK)
        cl_ref[rows, :] = _dot_by_01(log_dec[rows, :], upto, m01_first=True)
    a = _sigmoid(a0_ref[...] + jnp.dot(za_ref[...].astype(BF16), a2_ref[...], preferred_element_type=F32))
    g_ref[...] = jnp.dot(_sigmoid(zg_ref[...]).astype(BF16), g2_ref[...], preferred_element_type=F32)
    kk = k * kk_ref[...]
    hs = _head_sum_matrix(LANES)
    for c in range(k.shape[1] // LANES):
        sl = slice(c * LANES, (c + 1) * LANES)
        kc = kk[:, sl]
        ss = _dot_by_01(kc * kc, hs)
        kn = kc / jnp.maximum(jnp.sqrt(ss), 1e-12)
        na_ref[:, sl] = -kn
        nb_ref[:, sl] = kn * a[:, sl]
    kmod_ref[...] = k * (1.0 + (a - 1.0) * ka_ref[...])


def _rwkv_prep(zs, zoff, w2, a2, g2, w0, a0, k_k, k_a, d, tr):
    n = zs.shape[0]
    tc = _pick(d, (512, 256, 128))
    rdp, rip, rgp = w2.shape[0], a2.shape[0], g2.shape[0]
    row = lambda i, c: (i, c)
    vec = pl.BlockSpec((1, tc), lambda i, c: (0, c))
    out = jax.ShapeDtypeStruct((n, d), F32)
    return pl.pallas_call(
        _rwkv_prep_kernel,
        out_shape=[out] * 6,
        grid=(n // tr, d // tc),
        in_specs=[
            pl.BlockSpec((tr, tc), lambda i, c: (i, zoff["k"] // tc + c)),
            pl.BlockSpec((tr, rdp), lambda i, c: (i, zoff["zw"] // rdp)),
            pl.BlockSpec((tr, rip), lambda i, c: (i, zoff["za"] // rip)),
            pl.BlockSpec((tr, rgp), lambda i, c: (i, zoff["zg"] // rgp)),
            pl.BlockSpec((rdp, tc), lambda i, c: (0, c)),
            pl.BlockSpec((rip, tc), lambda i, c: (0, c)),
            pl.BlockSpec((rgp, tc), lambda i, c: (0, c)),
            vec, vec, vec, vec,
        ],
        out_specs=[pl.BlockSpec((tr, tc), row)] * 6,
        compiler_params=_params("parallel", "parallel"),
    )(zs, zs, zs, zs, w2, a2, g2, w0, a0, k_k, k_a)


def _scan_step(state, r, w, k, v, a, b, lo, eye0, eye1):
    def half_sums(x):
        s0 = jnp.sum(jnp.where(lo, x, 0.0), axis=1, keepdims=True)
        s1 = jnp.sum(jnp.where(lo, 0.0, x), axis=1, keepdims=True)
        return s0, s1

    sa0, sa1 = half_sums(state * a)
    vb = jnp.broadcast_to(v, state.shape)
    v0 = jnp.sum(jnp.where(eye0, vb, 0.0), axis=1, keepdims=True)
    v1 = jnp.sum(jnp.where(eye1, vb, 0.0), axis=1, keepdims=True)
    state = state * w + jnp.where(lo, sa0, sa1) * b + jnp.where(lo, v0, v1) * k
    y0, y1 = half_sums(state * r)
    y = jnp.sum(jnp.where(eye0, y0, jnp.where(eye1, y1, 0.0)), axis=0, keepdims=True)
    return state, y


def _scan_masks():
    shape = (RWKV_HEAD, LANES)
    i = lax.broadcasted_iota(jnp.int32, shape, 0)
    c = lax.broadcasted_iota(jnp.int32, shape, 1)
    return c < RWKV_HEAD, c == i, c == i + RWKV_HEAD


def _load_state(s_ref, p):
    return jnp.concatenate([s_ref[0, 2 * p], s_ref[0, 2 * p + 1]], axis=-1)


def _store_state(s_ref, p, state):
    s_ref[0, 2 * p] = state[:, :RWKV_HEAD]
    s_ref[0, 2 * p + 1] = state[:, RWKV_HEAD:]


def _mm(a, b):
    return jnp.dot(a.astype(BF16), b.astype(BF16), preferred_element_type=F32)


def _mm_nt(a, b):
    return lax.dot_general(a.astype(BF16), b.astype(BF16), (((1,), (1,)), ((), ())), preferred_element_type=F32)


def _mm_tn(a, b):
    return lax.dot_general(a.astype(BF16), b.astype(BF16), (((0,), (0,)), ((), ())), preferred_element_type=F32)


def _scan_chunk(sd, r, cl, k, v, a, b, lo, strict, incl, same_head):
    def blockdiag(x):
        return jnp.concatenate([jnp.where(lo, x, 0.0), jnp.where(lo, 0.0, x)], axis=0)

    pairs = range(len(sd))
    first = lax.broadcasted_iota(jnp.int32, cl[0].shape, 0) == 0
    g = [jnp.exp(cl[p]) for p in pairs]
    g_prev = [jnp.exp(jnp.where(first, 0.0, pltpu.roll(cl[p], 1, 0))) for p in pairs]
    g_inv = [jnp.exp(-cl[p]) for p in pairs]
    bt = [b[p] * g_inv[p] for p in pairs]
    kt = [k[p] * g_inv[p] for p in pairs]
    lhs = [jnp.concatenate([a[p] * g_prev[p], r[p] * g[p]], axis=0) for p in pairs]
    keys = [jnp.concatenate([jnp.where(lo, bt[p], 0.0), jnp.where(lo, 0.0, bt[p]),
                             jnp.where(lo, kt[p], 0.0), jnp.where(lo, 0.0, kt[p])], axis=0) for p in pairs]
    gram = [_mm_nt(lhs[p], keys[p]) for p in pairs]
    from_state = [_mm_nt(lhs[p], sd[p]) for p in pairs]
    nk = [jnp.where(strict, gram[p][:CHUNK, :LANES], 0.0) for p in pairs]
    vd = [blockdiag(v[p]) for p in pairs]
    x = [from_state[p][:CHUNK] + _mm(jnp.where(strict, gram[p][:CHUNK, LANES:], 0.0), vd[p]) for p in pairs]
    steps = CHUNK.bit_length() - 1
    for it in range(steps):
        x = [x[p] + _mm(nk[p], blockdiag(x[p])) for p in pairs]
        if it + 1 < steps:
            nk = [_mm(nk[p], blockdiag(nk[p])) for p in pairs]
    r_bk = [jnp.concatenate([jnp.where(incl, gram[p][CHUNK:, :LANES], 0.0),
                             jnp.where(incl, gram[p][CHUNK:, LANES:], 0.0)], axis=1) for p in pairs]
    y = [from_state[p][CHUNK:] + _mm(r_bk[p], jnp.concatenate([blockdiag(x[p]), vd[p]], axis=0)) for p in pairs]
    g_last = [g[p][CHUNK - 1:CHUNK, :] for p in pairs]
    update = [_mm_tn(jnp.concatenate([x[p], v[p]], axis=0),
                     jnp.concatenate([bt[p] * g_last[p], kt[p] * g_last[p]], axis=0)) for p in pairs]
    return [sd[p] * g_last[p] + jnp.where(same_head, update[p], 0.0) for p in pairs], y


def _scan_prompt_kernel(r_ref, cl_ref, k_ref, v_ref, a_ref, b_ref, y_ref, sout_ref, sd_ref, *, pairs, nchunk):
    tc = pl.program_id(2)

    @pl.when(tc == 0)
    def _():
        sd_ref[...] = jnp.zeros_like(sd_ref)

    lane = lax.broadcasted_iota(jnp.int32, (CHUNK, LANES), 1)
    tok = lax.broadcasted_iota(jnp.int32, (CHUNK, LANES), 0)
    lo = lane < RWKV_HEAD
    src = lane % RWKV_HEAD
    strict, incl = src < tok, src <= tok
    hr = lax.broadcasted_iota(jnp.int32, (LANES, LANES), 0) // RWKV_HEAD
    hc = lax.broadcasted_iota(jnp.int32, (LANES, LANES), 1) // RWKV_HEAD
    same_head = hr == hc

    def chunk(ci, carry):
        rows = pl.ds(pl.multiple_of(ci * CHUNK, CHUNK), CHUNK)
        cols = [slice(p * LANES, (p + 1) * LANES) for p in range(pairs)]
        take = lambda ref: [ref[rows, sl] for sl in cols]
        sd, y = _scan_chunk([sd_ref[p] for p in range(pairs)], take(r_ref), take(cl_ref), take(k_ref), take(v_ref),
                            take(a_ref), take(b_ref), lo, strict, incl, same_head)
        for p in range(pairs):
            sd_ref[p] = sd[p]
            y_ref[rows, cols[p]] = y[p]
        return carry

    lax.fori_loop(0, nchunk, chunk, 0)

    @pl.when(tc == pl.num_programs(2) - 1)
    def _():
        for p in range(pairs):
            sd = sd_ref[p]
            sout_ref[0, 2 * p] = sd[:RWKV_HEAD, :RWKV_HEAD]
            sout_ref[0, 2 * p + 1] = sd[RWKV_HEAD:, RWKV_HEAD:]


def _scan_sample_kernel(r_ref, w_ref, k_ref, v_ref, a_ref, b_ref, s0_ref, y_ref, sout_ref, *, pairs):
    lo, eye0, eye1 = _scan_masks()
    for p in range(pairs):
        sl = slice(p * LANES, (p + 1) * LANES)
        row = lambda ref: ref[0, :, sl]
        state, y = _scan_step(_load_state(s0_ref, p), row(r_ref), row(w_ref), row(k_ref), row(v_ref), row(a_ref),
                              row(b_ref), lo, eye0, eye1)
        _store_state(sout_ref, p, state)
        y_ref[0, :, sl] = y


def _rwkv_scan_prompt(r_src, r_cb, v_cb, cl, kmod, na, nb, b, t, d):
    pairs = _pick(d // LANES, (8, 4, 2, 1))
    wcol = pairs * LANES
    tchunk = _pick(t, (256, 128))
    nt = t // tchunk
    nh = d // RWKV_HEAD
    rowmap = lambda bi, g, tc: (bi * nt + tc, g)
    blk = pl.BlockSpec((tchunk, wcol), rowmap)
    return pl.pallas_call(
        functools.partial(_scan_prompt_kernel, pairs=pairs, nchunk=tchunk // CHUNK),
        out_shape=[jax.ShapeDtypeStruct((cl.shape[0], d), F32), jax.ShapeDtypeStruct((b, nh, RWKV_HEAD, RWKV_HEAD), F32)],
        grid=(b, d // wcol, nt),
        in_specs=[
            pl.BlockSpec((tchunk, wcol), lambda bi, g, tc: (bi * nt + tc, r_cb // pairs + g)),
            blk, blk,
            pl.BlockSpec((tchunk, wcol), lambda bi, g, tc: (bi * nt + tc, v_cb // pairs + g)),
            blk, blk,
        ],
        out_specs=[blk, pl.BlockSpec((1, 2 * pairs, RWKV_HEAD, RWKV_HEAD), lambda bi, g, tc: (bi, g, 0, 0))],
        scratch_shapes=[pltpu.VMEM((pairs, LANES, LANES), F32)],
        compiler_params=_params("parallel", "parallel", "arbitrary"),
    )(r_src, cl, kmod, r_src, na, nb)


def _rwkv_scan_sample(r, dec, kmod, v, na, nb, s0):
    bs, _, d = r.shape
    pairs = _pick(d // LANES, (4, 2, 1))
    wcol = pairs * LANES
    blk = pl.BlockSpec((1, 1, wcol), lambda bi, g: (bi, 0, g))
    sblk = pl.BlockSpec((1, 2 * pairs, RWKV_HEAD, RWKV_HEAD), lambda bi, g: (bi, g, 0, 0))
    return pl.pallas_call(
        functools.partial(_scan_sample_kernel, pairs=pairs),
        out_shape=[jax.ShapeDtypeStruct((bs, 1, d), F32), jax.ShapeDtypeStruct(s0.shape, F32)],
        grid=(bs, d // wcol),
        in_specs=[blk] * 6 + [sblk],
        out_specs=[blk, sblk],
        compiler_params=_params("parallel", "parallel"),
    )(r, dec, kmod, v, na, nb, s0)


def _rwkv_post_kernel(y_ref, r_ref, k_ref, v_ref, g_ref, rk_ref, lw_ref, lb_ref, gf_ref, gr_ref, gs_ref, of_ref, os_ref,
                      o_ref):
    hs = _head_sum_matrix(LANES)
    inv = 1.0 / RWKV_HEAD
    for c in range(y_ref.shape[1] // LANES):
        sl = slice(c * LANES, (c + 1) * LANES)
        y = y_ref[:, sl]
        mu = _dot_by_01(y, hs) * inv
        yc = y - mu
        var = _dot_by_01(yc * yc, hs) * inv
        yn = yc * lax.rsqrt(var + RWKV_GN_EPS) * lw_ref[:, sl] + lb_ref[:, sl]
        v = v_ref[:, sl]
        bonus = _dot_by_01(r_ref[:, sl] * k_ref[:, sl] * rk_ref[:, sl], hs) * v
        o_rwkv = (yn + bonus) * g_ref[:, sl]
        o_ref[:, sl] = (gf_ref[:, sl] * of_ref[:, sl] + gr_ref[:, sl] * o_rwkv + gs_ref[:, sl] * os_ref[:, sl]).astype(BF16)


def _rwkv_post_merge(y, zs, zoff, kmod, g, r_k, ln_w, ln_b, p, lay, o_fox, o_sb, d, tr):
    n = y.shape[0]
    tc = _pick(d, tuple(c for c in (512, 256, 128) if lay.off["gates"] % c == 0))
    gb = lay.off["gates"] // tc
    per = d // tc
    blk = pl.BlockSpec((tr, tc), lambda i, c: (i, c))
    vec = pl.BlockSpec((1, tc), lambda i, c: (0, c))
    gate = lambda which: pl.BlockSpec((tr, tc), lambda i, c: (i, gb + which * per + c))
    return pl.pallas_call(
        _rwkv_post_kernel,
        out_shape=jax.ShapeDtypeStruct((n, d), BF16),
        grid=(n // tr, d // tc),
        in_specs=[
            blk,
            pl.BlockSpec((tr, tc), lambda i, c: (i, zoff["r"] // tc + c)),
            blk,
            pl.BlockSpec((tr, tc), lambda i, c: (i, zoff["v"] // tc + c)),
            blk, vec, vec, vec,
            gate(0), gate(1), gate(2), blk, blk,
        ],
        out_specs=blk,
        compiler_params=_params("parallel", "parallel"),
    )(y, zs, kmod, zs, g, r_k, ln_w, ln_b, p, p, p, o_fox, o_sb)


def _outproj_kernel(m_ref, w_ref, x_ref, o_ref):
    o_ref[...] = x_ref[...] + jnp.dot(m_ref[...], w_ref[...], preferred_element_type=F32)


def _outproj(merged, w, x, tm):
    n, d = merged.shape
    tn = _pick(w.shape[1], (512, 256, 128))
    return pl.pallas_call(
        _outproj_kernel,
        out_shape=jax.ShapeDtypeStruct((n, w.shape[1]), F32),
        grid=(w.shape[1] // tn, n // tm),
        in_specs=[
            pl.BlockSpec((tm, d), lambda j, i: (i, 0)),
            pl.BlockSpec((d, tn), lambda j, i: (0, j)),
            pl.BlockSpec((tm, tn), lambda j, i: (i, j)),
        ],
        out_specs=pl.BlockSpec((tm, tn), lambda j, i: (i, j)),
        compiler_params=_params("parallel", "parallel"),
    )(merged, w, x)


def _top16_rows(s):
    nk = s.shape[0]
    idx = lax.broadcasted_iota(jnp.int32, s.shape, 0).astype(F32)
    work = s
    kept = jnp.full(s.shape, -jnp.inf, F32)
    tops, where = [], []
    for _ in range(PEER_TOPK):
        m = jnp.max(work, axis=0, keepdims=True)
        first = jnp.min(jnp.where(work == m, idx, float(nk)), axis=0, keepdims=True)
        hit = idx == first
        kept = jnp.where(hit, s, kept)
        work = jnp.where(hit, -jnp.inf, work)
        tops.append(m)
        where.append(first)
    return jnp.concatenate(tops, axis=0), jnp.concatenate(where, axis=0), kept


def _kth_largest(c, order, kth):
    m = first = None
    for _ in range(kth):
        m = jnp.max(c, axis=0, keepdims=True)
        first = jnp.min(jnp.where(c == m, order, jnp.inf), axis=0, keepdims=True)
        c = jnp.where(order == first, -jnp.inf, c)
    return m, first


STAT_THR, STAT_MAX, STAT_INVZ, STAT_S1, STAT_I, STAT_S2, STAT_J, STAT_TIED = range(8)


def _peer_gate_kernel(ht_ref, wq_ref, sk_ref, s1_ref, s2_ref, st_ref):
    qt = jnp.dot(wq_ref[0], ht_ref[...], preferred_element_type=F32)
    half = qt.shape[0] // 2
    sc = [jnp.dot(sk_ref[0, p], qt[p * half:(p + 1) * half].astype(BF16), preferred_element_type=F32)
          for p in range(2)]
    t1, i1, s1m = _top16_rows(sc[0])
    t2, i2, s2m = _top16_rows(sc[1])
    tn = t1.shape[1]
    pairs = [(a, b) for a in range(PEER_TOPK) for b in range(PEER_TOPK) if (a + 1) * (b + 1) <= PEER_TOPK]
    pad = -len(pairs) % SUBLANES
    cand = jnp.concatenate([t1[a:a + 1] + t2[b:b + 1] for a, b in pairs] + [jnp.full((pad, tn), -jnp.inf, F32)], axis=0)
    flat = jnp.concatenate([jnp.full((1, tn), float(a * PEER_TOPK + b), F32) for a, b in pairs]
                           + [jnp.full((pad, tn), float(PEER_TOPK * PEER_TOPK), F32)], axis=0)
    thr, last = _kth_largest(cand, flat, PEER_TOPK)
    kept = jnp.logical_or(cand > thr, jnp.logical_and(cand == thr, flat <= last))
    top = t1[0:1] + t2[0:1]
    zsum = jnp.sum(jnp.where(kept, jnp.exp(cand - top), 0.0), axis=0, keepdims=True)
    rank = lax.broadcasted_iota(jnp.int32, t1.shape, 0).astype(F32)
    a_last = jnp.floor(last * (1.0 / PEER_TOPK))
    b_last = last - a_last * PEER_TOPK
    pick = lambda table, r: jnp.sum(jnp.where(rank == r, table, 0.0), axis=0, keepdims=True)
    grid = (t1[:, None, :] + t2[None, :, :]).reshape(PEER_TOPK * PEER_TOPK, tn)
    tied = (jnp.sum(jnp.where(grid == thr, 1.0, 0.0), axis=0, keepdims=True)
            - jnp.sum(jnp.where(jnp.logical_and(cand == thr, kept), 1.0, 0.0), axis=0, keepdims=True))
    s1_ref[0] = s1m
    s2_ref[0] = s2m
    st_ref[0] = jnp.concatenate([thr, top, 1.0 / zsum, pick(t1, a_last), pick(i1, a_last), pick(t2, b_last),
                                 pick(i2, b_last), tied], axis=0)


def _peer_gate(ht, wq_t, subkeys, tn):
    d, n = ht.shape
    hp, _, nk, half = subkeys.shape
    arr = jax.ShapeDtypeStruct((hp, nk, n), F32)
    blk = pl.BlockSpec((1, nk, tn), lambda i, hh: (hh, 0, i))
    return pl.pallas_call(
        _peer_gate_kernel,
        out_shape=[arr, arr, jax.ShapeDtypeStruct((hp, SUBLANES, n), F32)],
        grid=(n // tn, hp),
        in_specs=[
            pl.BlockSpec((d, tn), lambda i, hh: (0, i)),
            pl.BlockSpec((1, 2 * half, d), lambda i, hh: (hh, 0, 0)),
            pl.BlockSpec((1, 2, nk, half), lambda i, hh: (hh, 0, 0, 0)),
        ],
        out_specs=[blk, blk, pl.BlockSpec((1, SUBLANES, tn), lambda i, hh: (hh, 0, i))],
        compiler_params=_params("parallel", "arbitrary"),
    )(ht, wq_t, subkeys)


def _gelu_exact(x):
    return 0.5 * x * (1.0 + lax.erf(x * (2.0 ** -0.5)))


def _peer_dense_kernel(tie_ref, ht_ref, u_ref, vt_ref, s1_ref, s2_ref, st_ref, o_ref, coef_ref,
                       *, rows_per_step, lane_tile):
    e = pl.program_id(1)
    hp, nk, tn = s2_ref.shape

    @pl.when(e == 0)
    def _():
        o_ref[...] = jnp.zeros_like(o_ref)

    per_group = SUBLANES // rows_per_step
    base = pl.multiple_of((e // per_group) * SUBLANES, SUBLANES)
    sub = e % per_group

    def fill_coef(break_ties):
        for lt in range(tn // lane_tile):
            ls = slice(lt * lane_tile, (lt + 1) * lane_tile)
            act = _gelu_exact(jnp.dot(u_ref[...], ht_ref[:, ls], preferred_element_type=F32))
            for ii in range(rows_per_step):
                gate = jnp.zeros((nk, lane_tile), F32)
                for hh in range(hp):
                    st = st_ref[hh, :, ls]
                    row = lambda r: st[r:r + 1]
                    grp = s1_ref[hh, pl.ds(base, SUBLANES), ls]
                    s1 = grp[ii:ii + 1]
                    for o in range(1, per_group):
                        s1 = jnp.where(sub == o, grp[o * rows_per_step + ii:o * rows_per_step + ii + 1], s1)
                    s2 = s2_ref[hh, :, ls]
                    val = s1 + s2
                    if break_ties:
                        i = (e * rows_per_step + ii).astype(F32)
                        j = lax.broadcasted_iota(jnp.int32, s2.shape, 0).astype(F32)
                        same_i = i == row(STAT_I)
                        earlier = jnp.logical_or(s1 > row(STAT_S1), jnp.logical_and(s1 == row(STAT_S1), i < row(STAT_I)))
                        within = jnp.logical_or(s2 > row(STAT_S2), jnp.logical_and(s2 == row(STAT_S2), j <= row(STAT_J)))
                        tie_ok = jnp.logical_or(earlier, jnp.logical_and(same_i, within))
                        keep = jnp.logical_or(val > row(STAT_THR), jnp.logical_and(val == row(STAT_THR), tie_ok))
                    else:
                        keep = val >= row(STAT_THR)
                    weight = jnp.exp(val - row(STAT_MAX)) * row(STAT_INVZ)
                    gate = gate + jnp.where(keep, weight, 0.0)
                coef_ref[ii * nk:(ii + 1) * nk, ls] = (gate * act[ii * nk:(ii + 1) * nk]).astype(BF16)

    tied = tie_ref[pl.program_id(0)] != 0

    @pl.when(jnp.logical_not(tied))
    def _():
        fill_coef(False)

    @pl.when(tied)
    def _():
        fill_coef(True)

    o_ref[...] += jnp.dot(vt_ref[0], coef_ref[...], preferred_element_type=F32)


def _peer_dense(ht, u, vt, s1, s2, stats, tn):
    d, n = ht.shape
    hp, nk, _ = s1.shape
    ne = u.shape[0]
    te = vt.shape[2]
    once = dict(pipeline_mode=pl.Buffered(1))
    full = pl.BlockSpec((hp, nk, tn), lambda i, e, tie: (0, 0, i), **once)
    tie_flags = (jnp.max(stats[:, STAT_TIED, :].reshape(hp, n // tn, tn), axis=(0, 2)) > 0).astype(jnp.int32)
    return pl.pallas_call(
        functools.partial(_peer_dense_kernel, rows_per_step=te // nk, lane_tile=_pick(tn, (256, 128))),
        out_shape=jax.ShapeDtypeStruct((d, n), F32),
        grid_spec=pltpu.PrefetchScalarGridSpec(
            num_scalar_prefetch=1,
            grid=(n // tn, ne // te),
            in_specs=[
                pl.BlockSpec((d, tn), lambda i, e, tie: (0, i), **once),
                pl.BlockSpec((te, d), lambda i, e, tie: (e, 0)),
                pl.BlockSpec((1, d, te), lambda i, e, tie: (e, 0, 0)),
                full, full,
                pl.BlockSpec((hp, SUBLANES, tn), lambda i, e, tie: (0, 0, i), **once),
            ],
            out_specs=pl.BlockSpec((d, tn), lambda i, e, tie: (0, i)),
            scratch_shapes=[pltpu.VMEM((te, tn), BF16)],
        ),
        compiler_params=pltpu.CompilerParams(dimension_semantics=("parallel", "arbitrary"),
                                             vmem_limit_bytes=PEER_VMEM_LIMIT_BYTES),
    )(tie_flags, ht, u, vt, s1, s2, stats)


def _residual_t_kernel(x_ref, yt_ref, o_ref):
    o_ref[...] = x_ref[...] + yt_ref[...].T


def _add_transposed(x, yt, tm):
    n, d = x.shape
    return pl.pallas_call(
        _residual_t_kernel,
        out_shape=jax.ShapeDtypeStruct((n, d), F32),
        grid=(n // tm,),
        in_specs=[pl.BlockSpec((tm, d), lambda i: (i, 0)), pl.BlockSpec((d, tm), lambda i: (0, i))],
        out_specs=pl.BlockSpec((tm, d), lambda i: (i, 0)),
        compiler_params=_params("parallel"),
    )(x, yt)


def kernel(x_prompt, x_sample, cache_fox_k, cache_fox_v, cache_fox_logf, cache_sb_k, cache_sb_v, state_rwkv, state_rwkv_shift, page_table, norm_mix, w_in, b_forget, fox_q_norm, fox_k_norm, sb_q_norm, sb_k_norm, rwkv_mu, rwkv_w0, rwkv_w2, rwkv_a0, rwkv_a2, rwkv_g2, rwkv_k_k, rwkv_k_a, rwkv_r_k, rwkv_ln_w, rwkv_ln_b, w_out, norm_ffn, peer_wq, peer_subkeys, peer_u, peer_v):
    b, t, d = x_prompt.shape
    bs = x_sample.shape[0]
    depth = w_in.shape[0]
    hq = d // HEAD_DIM
    hkv = hq // GQA
    dkv = hkv * HEAD_DIM
    rd, ri, rg = rwkv_w2.shape[1], rwkv_a2.shape[1], rwkv_g2.shape[1]
    n_prompt = b * t
    tr = _pick(t, (256, 128))
    n = _round_up(n_prompt + bs, tr)
    tm = _pick(n, (768, 512, 384, 256, 128))
    tm_in = _pick(n, (1408, 768, 512, 384, 256, 128))
    tn_in = 512 if d % 2048 == 0 else LANES
    lay = _Layout(d, hq, rd, ri, rg, tn_in)
    zoff = {k: lay.off[k] - lay.z_lo for k in ("r", "k", "v", "zw", "za", "zg")}
    rdp, rip, rgp = _round_up(rd, LANES), _round_up(ri, LANES), _round_up(rg, LANES)
    tn_peer = _pick(n, (768, 256, 128))
    te_peer = 4 * peer_subkeys.shape[3]

    def tile_heads(g, count):
        return jnp.tile(g, count)

    def pad_rows(a, rows):
        return jnp.concatenate([a, jnp.zeros((rows - a.shape[0],) + a.shape[1:], a.dtype)], axis=0)

    def with_tail(full, sample_rows):
        return full.at[n_prompt:].set(pad_rows(sample_rows, n - n_prompt))

    def one_row(a):
        return a.reshape(bs, 1, d)

    def zr_original_order(rows):
        return jnp.concatenate([rows[..., zoff[k]:zoff[k] + w] for k, w in
                                (("r", d), ("k", d), ("v", d), ("zw", rd), ("za", ri), ("zg", rg))], axis=-1)

    def to_z_layout(a):
        parts, o, src = [], 0, 0
        for k, w in (("r", d), ("k", d), ("v", d), ("zw", rd), ("za", ri), ("zg", rg)):
            if zoff[k] > o:
                parts.append(jnp.zeros(a.shape[:-1] + (zoff[k] - o,), a.dtype))
            parts.append(a[..., src:src + w])
            o, src = zoff[k] + w, src + w
        if lay.z_width > o:
            parts.append(jnp.zeros(a.shape[:-1] + (lay.z_width - o,), a.dtype))
        return jnp.concatenate(parts, axis=-1)

    x = pad_rows(jnp.concatenate([x_prompt.reshape(n_prompt, d), x_sample.reshape(bs, d)], axis=0), n)
    n_pool, page = cache_fox_logf.shape[1], cache_fox_logf.shape[2]
    lf_cache = jnp.swapaxes(cache_fox_logf, 2, 3).reshape(depth * n_pool, hq, page)
    lf_cache = _pad_heads(lf_cache, hkv).reshape(depth, n_pool, hkv * DEC_ROWS, page)

    prompt_states, sample_states = [], []
    for l in range(depth):
        h = _rmsnorm(x, norm_mix[l], tm)
        w_l = lay.scatter_cols(w_in[l], lay.ncol).astype(BF16)
        colp = jnp.zeros((lay.n_in,), F32)
        for name, g, cnt in (("fq", fox_q_norm[l], hq), ("fk", fox_k_norm[l], hkv), ("sq", sb_q_norm[l], hq),
                             ("sk", sb_k_norm[l], hkv)):
            s0, w0 = lay.src[name]
            colp = colp.at[s0:s0 + w0].set(tile_heads(g, cnt))
        s0, w0 = lay.src["ff"]
        colp = colp.at[s0:s0 + w0].set(b_forget[l])
        colp = lay.scatter_cols(colp.reshape(1, -1), lay.ncol)
        p = _inproj(h, w_l, colp, jnp.asarray(lay.modes), tm_in, tn_in)

        def cols(name, width, rows=slice(None)):
            return p[rows, lay.off[name]:lay.off[name] + width]

        c = _cumsum_time(p, b, t, lay.off["ff"] // LANES)[:, :hq].reshape(b, t, hq)
        tq = _pick(t, FOX_Q_TILES)
        c_t = jnp.swapaxes(c, 1, 2)
        cq = c_t.reshape(b, hq, t, 1)
        ck = c_t.reshape(b, hq, t // tq, 1, tq)
        o_fox_p = _prompt_attention(p, lay, b, t, d, cq, ck)
        o_sb_p = _prompt_attention(p, lay, b, t, d)

        srow = slice(n_prompt, n_prompt + bs)
        scale = HEAD_DIM ** -0.5
        fq_s = _pad_heads((cols("fq", d, srow) * scale).reshape(bs, hq, HEAD_DIM), hkv)
        sq_s = _pad_heads((cols("sq", d, srow) * scale).reshape(bs, hq, HEAD_DIM), hkv)
        fk_s = cols("fk", dkv, srow).reshape(bs, hkv, HEAD_DIM)
        fv_s = cols("fv", dkv, srow).reshape(bs, hkv, HEAD_DIM)
        lf_s = cols("ff", hq, srow)
        o_fox_s = _decode_attention(l, page_table, fq_s, cache_fox_k, cache_fox_v, lf_cache, fk_s, fv_s,
                                    _pad_heads(lf_s.reshape(bs, hq, 1), hkv))
        o_sb_s = _decode_attention(l, page_table, sq_s, cache_sb_k, cache_sb_v)
        o_fox = with_tail(o_fox_p, _unpad_heads(o_fox_s, hkv))
        o_sb = with_tail(o_sb_p, _unpad_heads(o_sb_s, hkv))

        state_pad = pad_rows(to_z_layout(state_rwkv_shift[l]), tr)
        zs = _token_shift(p, lay, state_pad, to_z_layout(rwkv_mu[l]).reshape(1, -1), t, n_prompt, tr)
        pad_k = lambda w2, rp: pad_rows(w2, rp).astype(BF16)
        vec = lambda a: a.reshape(1, d)
        dec, cl, kmod, na, nb, gg = _rwkv_prep(zs, zoff, pad_k(rwkv_w2[l], rdp), pad_k(rwkv_a2[l], rip),
                                               pad_k(rwkv_g2[l], rgp), vec(rwkv_w0[l]), vec(rwkv_a0[l]),
                                               vec(rwkv_k_k[l]), vec(rwkv_k_a[l]), d, tr)
        y_p, s_p = _rwkv_scan_prompt(zs, zoff["r"] // LANES, zoff["v"] // LANES, cl, kmod, na, nb, b, t, d)
        one = lambda a: a[srow].reshape(bs, 1, d)
        zs_s = zs[srow]
        y_s, s_s = _rwkv_scan_sample(one_row(zs_s[:, zoff["r"]:zoff["r"] + d]), one(dec), one(kmod),
                                     one_row(zs_s[:, zoff["v"]:zoff["v"] + d]), one(na), one(nb), state_rwkv[l])
        y = with_tail(y_p, y_s.reshape(bs, d))
        merged = _rwkv_post_merge(y, zs, zoff, kmod, gg, vec(rwkv_r_k[l].reshape(-1)), vec(rwkv_ln_w[l]),
                                  vec(rwkv_ln_b[l]), p, lay, o_fox, o_sb, d, tr)
        x = _outproj(merged, w_out[l].astype(BF16), x, tm)

        h2t = _rmsnorm(x, norm_ffn[l], tm, transposed=True)
        wq_t = peer_wq[l].T.reshape(-1, peer_wq.shape[-1] // peer_subkeys.shape[1], d).astype(BF16)
        s1, s2, stats = _peer_gate(h2t, wq_t, peer_subkeys[l].astype(BF16), tn_peer)
        vt = jnp.swapaxes(peer_v[l].reshape(-1, te_peer, d), 1, 2).astype(BF16)
        yt = _peer_dense(h2t, peer_u[l].astype(BF16), vt, s1, s2, stats, tn_peer)
        x = _add_transposed(x, yt, tr)

        last = jnp.arange(b) * t + (t - 1)
        zr_window = lambda rows: zr_original_order(p[rows][:, lay.z_lo:lay.z_lo + lay.z_width])
        prompt_states.append((
            cols("fk", dkv, slice(0, n_prompt)).reshape(b, t, hkv, HEAD_DIM),
            cols("fv", dkv, slice(0, n_prompt)).reshape(b, t, hkv, HEAD_DIM),
            cols("ff", hq, slice(0, n_prompt)).reshape(b, t, hq),
            cols("sk", dkv, slice(0, n_prompt)).reshape(b, t, hkv, HEAD_DIM),
            cols("sv", dkv, slice(0, n_prompt)).reshape(b, t, hkv, HEAD_DIM),
            s_p, zr_window(last)))
        sample_states.append((
            fk_s.reshape(bs, 1, hkv, HEAD_DIM), fv_s.reshape(bs, 1, hkv, HEAD_DIM), lf_s.reshape(bs, 1, hq),
            cols("sk", dkv, srow).reshape(bs, 1, hkv, HEAD_DIM), cols("sv", dkv, srow).reshape(bs, 1, hkv, HEAD_DIM),
            s_s, zr_window(srow)))

    stack = lambda states, i: jnp.stack([s[i] for s in states])
    y_prompt = x[:n_prompt].reshape(b, t, d)
    y_sample = x[n_prompt:n_prompt + bs].reshape(bs, 1, d)
    return ((y_prompt, y_sample) + tuple(stack(prompt_states, i) for i in range(7))
            + tuple(stack(sample_states, i) for i in range(7)))
```

```python
import functools

import numpy as np
import jax
import jax.numpy as jnp
from jax import lax
from jax.experimental import pallas as pl
from jax.experimental.pallas import tpu as pltpu

F32 = jnp.float32
BF16 = jnp.bfloat16
HIGHEST = lax.Precision.HIGHEST

NORM_EPS = 1e-6
RWKV_GN_EPS = 64e-5
LANES = 128
SUBLANES = 8
HEAD_DIM = 128
RWKV_HEAD = 64
CHUNK = 64
GQA = 4
FOX_Q_TILES = (512, 256, 128)
SB_Q_TILES = (256, 128)
PEER_TOPK = 16
VMEM_LIMIT_BYTES = 56 * 1024 * 1024
PEER_VMEM_LIMIT_BYTES = 60 * 1024 * 1024

MODE_ID, MODE_RMS, MODE_SIGMOID, MODE_LOGSIG = 0, 1, 2, 3


def _params(*sem):
    return pltpu.CompilerParams(dimension_semantics=sem, vmem_limit_bytes=VMEM_LIMIT_BYTES)


def _round_up(n, m):
    return (n + m - 1) // m * m


def _pick(n, cands):
    for c in cands:
        if c <= n and n % c == 0:
            return c
    raise ValueError(f"no tile for {n} in {cands}")


def _sigmoid(x):
    return 1.0 / (1.0 + jnp.exp(-x))


def _dot_by_01(x, m01, m01_first=False, pieces=3):
    mm = (lambda piece: jnp.dot(m01, piece, preferred_element_type=F32)) if m01_first else (
        lambda piece: jnp.dot(piece, m01, preferred_element_type=F32))
    out, rest = None, x
    for _ in range(pieces):
        piece = rest.astype(BF16)
        rest = rest - piece.astype(F32)
        out = mm(piece) if out is None else out + mm(piece)
    return out


def _log_sigmoid(x):
    return jnp.minimum(x, 0.0) - jnp.log1p(jnp.exp(-jnp.abs(x)))


def _log_sigmoid_bulk(x):
    return jnp.minimum(x, 0.0) - jnp.log(1.0 + jnp.exp(-jnp.abs(x)))


def _rmsnorm_kernel(x_ref, g_ref, o_ref, *, transposed):
    x = x_ref[...]
    ms = jnp.mean(x * x, axis=-1, keepdims=True)
    y = x * lax.rsqrt(ms + NORM_EPS) * g_ref[...]
    o_ref[...] = (y.T if transposed else y).astype(o_ref.dtype)


def _rmsnorm(x, g, tm, transposed=False):
    n, d = x.shape
    return pl.pallas_call(
        functools.partial(_rmsnorm_kernel, transposed=transposed),
        out_shape=jax.ShapeDtypeStruct((d, n) if transposed else (n, d), BF16),
        grid=(n // tm,),
        in_specs=[pl.BlockSpec((tm, d), lambda i: (i, 0)), pl.BlockSpec((1, d), lambda i: (0, 0))],
        out_specs=pl.BlockSpec((d, tm), lambda i: (0, i)) if transposed else pl.BlockSpec((tm, d), lambda i: (i, 0)),
        compiler_params=_params("parallel"),
    )(x, g.reshape(1, d))


def _inproj_kernel(modes_ref, h_ref, w_ref, cp_ref, o_ref):
    acc = jnp.dot(h_ref[...], w_ref[...], preferred_element_type=F32)
    mode = modes_ref[pl.program_id(1)]

    @pl.when(mode == MODE_ID)
    def _():
        o_ref[...] = acc

    @pl.when(mode == MODE_RMS)
    def _():
        for c in range(acc.shape[1] // HEAD_DIM):
            sl = slice(c * HEAD_DIM, (c + 1) * HEAD_DIM)
            xs = acc[:, sl]
            ms = jnp.mean(xs * xs, axis=-1, keepdims=True)
            o_ref[:, sl] = xs * lax.rsqrt(ms + NORM_EPS) * cp_ref[:, sl]

    @pl.when(mode == MODE_SIGMOID)
    def _():
        o_ref[...] = _sigmoid(acc)

    @pl.when(mode == MODE_LOGSIG)
    def _():
        o_ref[...] = _log_sigmoid(acc + cp_ref[...])


def _inproj(h, w, colp, modes, tm, tn):
    n, d = h.shape
    ncol = w.shape[1]
    return pl.pallas_call(
        _inproj_kernel,
        out_shape=jax.ShapeDtypeStruct((n, ncol), F32),
        grid_spec=pltpu.PrefetchScalarGridSpec(
            num_scalar_prefetch=1,
            grid=(n // tm, ncol // tn),
            in_specs=[
                pl.BlockSpec((tm, d), lambda i, j, m: (i, 0)),
                pl.BlockSpec((d, tn), lambda i, j, m: (0, j)),
                pl.BlockSpec((1, tn), lambda i, j, m: (0, j)),
            ],
            out_specs=pl.BlockSpec((tm, tn), lambda i, j, m: (i, j)),
        ),
        compiler_params=_params("parallel", "parallel"),
    )(modes, h, w, colp)


class _Layout:
    def __init__(self, d, hq, rd, ri, rg, tn):
        dkv = d // GQA
        zw = 3 * d + rd + ri + rg
        src = {}
        o = 0
        for name, width in (("fq", d), ("fk", dkv), ("fv", dkv), ("ff", hq), ("sq", d), ("sk", dkv), ("sv", dkv)):
            src[name] = (o, width)
            o += width
        for name, width in (("r", d), ("k", d), ("v", d), ("zw", rd), ("za", ri), ("zg", rg)):
            src[name] = (o, width)
            o += width
        src["gates"] = (o, 3 * d)
        self.n_in = o + 3 * d
        self.zr_width = zw
        order = (("fq", MODE_RMS), ("sq", MODE_RMS), ("fk", MODE_RMS), ("fv", MODE_ID), ("sk", MODE_RMS),
                 ("sv", MODE_ID), ("r", MODE_ID), ("k", MODE_ID), ("v", MODE_ID), ("zw", MODE_ID),
                 ("za", MODE_ID), ("zg", MODE_ID), ("gates", MODE_SIGMOID), ("ff", MODE_LOGSIG))
        self.src = src
        self.off = {}
        self.pieces = []
        seg_modes = []
        o, prev = 0, None
        for name, mode in order:
            o = _round_up(o, tn if mode != prev else _round_up(min(src[name][1], 4 * LANES), LANES))
            self.off[name] = o
            seg_modes.append((o, mode))
            self.pieces.append((o, src[name][0], src[name][1]))
            o += src[name][1]
            prev = mode
        self.ncol = _round_up(o, tn)
        self.z_lo = self.off["r"]
        self.z_width = _round_up(self.off["zg"] + rg, LANES) - self.z_lo
        modes = np.zeros((self.ncol // tn,), np.int32)
        for t in range(self.ncol // tn):
            for so, m in seg_modes:
                if so <= t * tn:
                    modes[t] = m
        self.modes = modes

    def scatter_cols(self, a, width):
        parts, o = [], 0
        for dst, s, w in self.pieces:
            if dst > o:
                parts.append(jnp.zeros(a.shape[:-1] + (dst - o,), a.dtype))
            parts.append(a[..., s:s + w])
            o = dst + w
        if width > o:
            parts.append(jnp.zeros(a.shape[:-1] + (width - o,), a.dtype))
        return jnp.concatenate(parts, axis=-1)


def _cumsum_kernel(x_ref, o_ref, *, cb):
    t = x_ref.shape[0]
    row = lax.broadcasted_iota(jnp.int32, (cb, cb), 0)
    col = lax.broadcasted_iota(jnp.int32, (cb, cb), 1)
    tri = (row >= col).astype(F32)
    carry = jnp.zeros((1, x_ref.shape[1]), F32)
    for i in range(t // cb):
        c = jnp.dot(tri, x_ref[i * cb:(i + 1) * cb, :], precision=HIGHEST, preferred_element_type=F32) + carry
        o_ref[i * cb:(i + 1) * cb, :] = c
        carry = c[cb - 1:cb, :]


def _cumsum_time(p, b, t, col_block):
    cb = _pick(t, (256, 128, 64, 32, 16, 8))
    return pl.pallas_call(
        functools.partial(_cumsum_kernel, cb=cb),
        out_shape=jax.ShapeDtypeStruct((b * t, LANES), F32),
        grid=(b,),
        in_specs=[pl.BlockSpec((t, LANES), lambda i: (i, col_block))],
        out_specs=pl.BlockSpec((t, LANES), lambda i: (i, 0)),
        compiler_params=_params("parallel"),
    )(p)


def _stack_heads(q_ref, scale):
    tq = q_ref.shape[0]
    q = q_ref[...] * scale
    return jnp.concatenate([q[:, h * HEAD_DIM:(h + 1) * HEAD_DIM] for h in range(GQA)], axis=0).astype(BF16)


def _unstack_heads(o, o_ref):
    tq = o_ref.shape[0]
    for h in range(GQA):
        o_ref[:, h * HEAD_DIM:(h + 1) * HEAD_DIM] = o[h * tq:(h + 1) * tq, :]


def _fox_prompt_kernel(q_ref, k_ref, v_ref, cq_ref, ck_ref, o_ref, *, tq):
    qi = pl.program_id(2)
    rows = GQA * tq
    qs = _stack_heads(q_ref, HEAD_DIM ** -0.5)
    cq = cq_ref[0].reshape(rows, 1)

    def scores(ki):
        k = k_ref[pl.ds(pl.multiple_of(ki * tq, tq), tq), :].astype(BF16)
        s = lax.dot_general(qs, k, (((1,), (1,)), ((), ())), preferred_element_type=F32)
        ck = ck_ref[0, :, ki]
        return ((s + cq).reshape(GQA, tq, tq) - ck).reshape(rows, tq)

    def update(ki, s, carry):
        m, l, acc = carry
        m_new = jnp.maximum(m, jnp.max(s, axis=-1, keepdims=True))
        alpha = jnp.exp(m - m_new)
        p = jnp.exp(s - m_new)
        v = v_ref[pl.ds(pl.multiple_of(ki * tq, tq), tq), :].astype(BF16)
        acc = acc * alpha + jnp.dot(p.astype(BF16), v, preferred_element_type=F32)
        return m_new, l * alpha + jnp.sum(p, axis=-1, keepdims=True), acc

    init = (jnp.full((rows, 1), -jnp.inf, F32), jnp.zeros((rows, 1), F32), jnp.zeros((rows, HEAD_DIM), F32))
    carry = lax.fori_loop(0, qi, lambda ki, c: update(ki, scores(ki), c), init)
    r = lax.broadcasted_iota(jnp.int32, (GQA, tq, tq), 1)
    c = lax.broadcasted_iota(jnp.int32, (GQA, tq, tq), 2)
    s = jnp.where((c <= r).reshape(rows, tq), scores(qi), -jnp.inf)
    m, l, acc = update(qi, s, carry)
    _unstack_heads(acc / l, o_ref)


def _sb_prompt_kernel(q_ref, k_ref, v_ref, o_ref, *, tq):
    qi = pl.program_id(2)
    rows = GQA * tq
    qs = _stack_heads(q_ref, HEAD_DIM ** -0.5)
    jr = lax.broadcasted_iota(jnp.int32, (tq, tq), 0)
    sc = lax.broadcasted_iota(jnp.int32, (tq, tq), 1)
    later = (jr > sc).astype(BF16)

    def block(ki, carry, mask):
        run, acc = carry
        k = k_ref[pl.ds(pl.multiple_of(ki * tq, tq), tq), :].astype(BF16)
        v = v_ref[pl.ds(pl.multiple_of(ki * tq, tq), tq), :].astype(BF16)
        z = lax.dot_general(qs, k, (((1,), (1,)), ((), ())), preferred_element_type=F32)
        log_keep = _log_sigmoid_bulk(-z)
        if mask is not None:
            log_keep = jnp.where(mask, log_keep, 0.0)
        after = _dot_by_01(log_keep, later, pieces=2) + run
        w = jnp.exp(z + log_keep + after)
        if mask is not None:
            w = jnp.where(mask, w, 0.0)
        acc = acc + jnp.dot(w.astype(BF16), v, preferred_element_type=F32)
        return run + jnp.sum(log_keep, axis=-1, keepdims=True), acc

    r3 = lax.broadcasted_iota(jnp.int32, (GQA, tq, tq), 1)
    c3 = lax.broadcasted_iota(jnp.int32, (GQA, tq, tq), 2)
    carry = block(qi, (jnp.zeros((rows, 1), F32), jnp.zeros((rows, HEAD_DIM), F32)), (c3 < r3).reshape(rows, tq))
    run, acc = lax.fori_loop(0, qi, lambda it, c: block(qi - 1 - it, c, None), carry)
    _unstack_heads(acc, o_ref)


def _prompt_attention(p, lay, b, t, d, cq=None, ck=None):
    hkv = d // (GQA * HEAD_DIM)
    fox = cq is not None
    tq = ck.shape[-1] if fox else _pick(t, SB_Q_TILES)
    nq = t // tq
    gw = GQA * HEAD_DIM
    qn, kn, vn = ("fq", "fk", "fv") if fox else ("sq", "sk", "sv")
    qb, kb, vb = lay.off[qn] // gw, lay.off[kn] // HEAD_DIM, lay.off[vn] // HEAD_DIM
    in_specs = [
        pl.BlockSpec((tq, gw), lambda bi, g, qi: (bi * nq + qi, qb + g)),
        pl.BlockSpec((t, HEAD_DIM), lambda bi, g, qi: (bi, kb + g)),
        pl.BlockSpec((t, HEAD_DIM), lambda bi, g, qi: (bi, vb + g)),
    ]
    args = [p, p, p]
    if fox:
        in_specs += [
            pl.BlockSpec((1, GQA, tq, 1), lambda bi, g, qi: (bi, g, qi, 0)),
            pl.BlockSpec((1, GQA, nq, 1, tq), lambda bi, g, qi: (bi, g, 0, 0, 0)),
        ]
        args += [cq, ck]
        body = functools.partial(_fox_prompt_kernel, tq=tq)
    else:
        body = functools.partial(_sb_prompt_kernel, tq=tq)
    return pl.pallas_call(
        body,
        out_shape=jax.ShapeDtypeStruct((p.shape[0], d), F32),
        grid=(b, hkv, nq),
        in_specs=in_specs,
        out_specs=pl.BlockSpec((tq, gw), lambda bi, g, qi: (bi * nq + qi, g)),
        compiler_params=_params("parallel", "parallel", "arbitrary"),
    )(*args)


DEC_ROWS = 2 * SUBLANES


def _decode_kernel(pt_ref, q_ref, *rest, fox, hkv, pp):
    kc_refs, vc_refs, rest = rest[:pp], rest[pp:2 * pp], rest[2 * pp:]
    if fox:
        lf_refs = rest[:pp]
        knew_ref, vnew_ref, lfnew_ref, o_ref, m_ref, l_ref, run_ref, acc_ref = rest[pp:]
    else:
        o_ref, run_ref, acc_ref = rest
    pg = pl.program_id(1)
    npg = pl.num_programs(1)
    page = kc_refs[0].shape[2]
    rows = hkv * DEC_ROWS

    @pl.when(pg == 0)
    def _():
        if fox:
            for g in range(hkv):
                sl = slice(g * DEC_ROWS, (g + 1) * DEC_ROWS)
                qg = q_ref[0, sl, :].astype(BF16).astype(F32)
                kg = knew_ref[0, g:g + 1, :].astype(BF16).astype(F32)
                m_ref[sl, :] = jnp.sum(qg * kg, axis=-1, keepdims=True)
                acc_ref[sl, :] = jnp.broadcast_to(vnew_ref[0, g:g + 1, :].astype(BF16).astype(F32), (DEC_ROWS, HEAD_DIM))
            l_ref[...] = jnp.ones_like(l_ref)
            run_ref[...] = lfnew_ref[0]
        else:
            run_ref[...] = jnp.zeros_like(run_ref)
            acc_ref[...] = jnp.zeros_like(acc_ref)

    jr = lax.broadcasted_iota(jnp.int32, (page, page), 0)
    sc = lax.broadcasted_iota(jnp.int32, (page, page), 1)
    later = (jr > sc).astype(BF16)
    k_pages = [pltpu.einshape("khd->hkd", ref[0, 0]) for ref in kc_refs]
    v_pages = [pltpu.einshape("khd->hkd", ref[0, 0]) for ref in vc_refs]
    k_heads = [jnp.concatenate([kp[g] for kp in k_pages], axis=0).astype(BF16) for g in range(hkv)]
    v_heads = [jnp.concatenate([vp[g] for vp in v_pages], axis=0).astype(BF16) for g in range(hkv)]
    z = jnp.concatenate(
        [lax.dot_general(q_ref[0, g * DEC_ROWS:(g + 1) * DEC_ROWS, :].astype(BF16), k_heads[g],
                         (((1,), (1,)), ((), ())), preferred_element_type=F32) for g in range(hkv)], axis=0)
    run = run_ref[...]
    in_page = lambda x, j: x[:, j * page:(j + 1) * page]

    def later_sums(per_key):
        out, carry = [], run
        for j in range(pp):
            out.append(_dot_by_01(in_page(per_key, j), later) + carry)
            carry = carry + jnp.sum(in_page(per_key, j), axis=-1, keepdims=True)
        return jnp.concatenate(out, axis=1), carry

    if fox:
        bias, run_ref[...] = later_sums(jnp.concatenate([ref[0, 0] for ref in lf_refs], axis=1))
        s = z + bias
        m = m_ref[...]
        m_new = jnp.maximum(m, jnp.max(s, axis=-1, keepdims=True))
        alpha = jnp.exp(m - m_new)
        w = jnp.exp(s - m_new)
        l_ref[...] = l_ref[...] * alpha + jnp.sum(w, axis=-1, keepdims=True)
        m_ref[...] = m_new
    else:
        log_keep = _log_sigmoid_bulk(-z)
        after, run_ref[...] = later_sums(log_keep)
        w = jnp.exp(z + log_keep + after)
        alpha = None
    wb = w.astype(BF16)
    for g in range(hkv):
        sl = slice(g * DEC_ROWS, (g + 1) * DEC_ROWS)
        pv = jnp.dot(wb[sl, :], v_heads[g], preferred_element_type=F32)
        if fox:
            acc_ref[sl, :] = acc_ref[sl, :] * alpha[sl, :] + pv
        else:
            acc_ref[sl, :] = acc_ref[sl, :] + pv

    @pl.when(pg == npg - 1)
    def _():
        if fox:
            o_ref[0] = acc_ref[...] / l_ref[...]
        else:
            o_ref[0] = acc_ref[...]


def _decode_attention(layer, page_table, q, kcache, vcache, lf_cache=None, knew=None, vnew=None, lfnew=None):
    bs, npg = page_table.shape
    _, _, page, hkv, _ = kcache.shape
    rows = hkv * DEC_ROWS
    fox = lf_cache is not None
    pp = _pick(npg, (8, 4, 2, 1))
    nth_page = lambda j: (lambda b, p, pt: (layer, pt[b, npg - 1 - (p * pp + j)], 0, 0, 0))
    cache_specs = [pl.BlockSpec((1, 1, page, hkv, HEAD_DIM), nth_page(j)) for j in range(pp)]
    in_specs = [pl.BlockSpec((1, rows, HEAD_DIM), lambda b, p, pt: (b, 0, 0))] + cache_specs + cache_specs
    args = [q] + [kcache] * pp + [vcache] * pp
    scratch = [pltpu.VMEM((rows, 1), F32), pltpu.VMEM((rows, HEAD_DIM), F32)]
    if fox:
        in_specs += [pl.BlockSpec((1, 1, rows, page), lambda b, p, pt, j=j: (layer, pt[b, npg - 1 - (p * pp + j)], 0, 0))
                     for j in range(pp)]
        in_specs += [
            pl.BlockSpec((1, hkv, HEAD_DIM), lambda b, p, pt: (b, 0, 0)),
            pl.BlockSpec((1, hkv, HEAD_DIM), lambda b, p, pt: (b, 0, 0)),
            pl.BlockSpec((1, rows, 1), lambda b, p, pt: (b, 0, 0)),
        ]
        args += [lf_cache] * pp + [knew, vnew, lfnew]
        scratch = [pltpu.VMEM((rows, 1), F32), pltpu.VMEM((rows, 1), F32)] + scratch
    return pl.pallas_call(
        functools.partial(_decode_kernel, fox=fox, hkv=hkv, pp=pp),
        out_shape=jax.ShapeDtypeStruct((bs, rows, HEAD_DIM), F32),
        grid_spec=pltpu.PrefetchScalarGridSpec(
            num_scalar_prefetch=1,
            grid=(bs, npg // pp),
            in_specs=in_specs,
            out_specs=pl.BlockSpec((1, rows, HEAD_DIM), lambda b, p, pt: (b, 0, 0)),
            scratch_shapes=scratch,
        ),
        compiler_params=_params("parallel", "arbitrary"),
    )(page_table, *args)


def _pad_heads(a, hkv):
    bs, _, x = a.shape
    a = a.reshape(bs, hkv, GQA, x)
    a = jnp.concatenate([a, jnp.zeros((bs, hkv, DEC_ROWS - GQA, x), a.dtype)], axis=2)
    return a.reshape(bs, hkv * DEC_ROWS, x)


def _unpad_heads(a, hkv):
    bs, _, x = a.shape
    return a.reshape(bs, hkv, DEC_ROWS, x)[:, :, :GQA].reshape(bs, hkv * GQA * x)


def _shift_kernel(z_ref, prev_ref, st_ref, mu_ref, o_ref, *, tiles_per_seq, sample_tile):
    i = pl.program_id(0)
    z = z_ref[...]
    first = (i % tiles_per_seq) == 0
    row0 = jnp.where(first, 0.0, prev_ref[SUBLANES - 1:SUBLANES, :])
    rid = lax.broadcasted_iota(jnp.int32, z.shape, 0)
    zp = jnp.where(rid == 0, row0, pltpu.roll(z, 1, 0))
    zp = jnp.where(i == sample_tile, st_ref[...], zp)
    o_ref[...] = z + (zp - z) * mu_ref[...]


def _token_shift(p, lay, state_pad, mu, t, n_prompt, tr):
    n = p.shape[0]
    zp = lay.z_width
    tc = _pick(zp, tuple(c for c in (1024, 896, 768, 640, 512, 384, 256, 128) if lay.z_lo % c == 0))
    cb = lay.z_lo // tc
    per8 = tr // SUBLANES
    return pl.pallas_call(
        functools.partial(_shift_kernel, tiles_per_seq=t // tr, sample_tile=n_prompt // tr),
        out_shape=jax.ShapeDtypeStruct((n, zp), F32),
        grid=(n // tr, zp // tc),
        in_specs=[
            pl.BlockSpec((tr, tc), lambda i, c: (i, cb + c)),
            pl.BlockSpec((SUBLANES, tc), lambda i, c: (jnp.maximum(i * per8 - 1, 0), cb + c)),
            pl.BlockSpec((tr, tc), lambda i, c: (0, c)),
            pl.BlockSpec((1, tc), lambda i, c: (0, c)),
        ],
        out_specs=pl.BlockSpec((tr, tc), lambda i, c: (i, c)),
        compiler_params=_params("parallel", "parallel"),
    )(p, p, state_pad, mu)


def _head_sum_matrix(width):
    a = lax.broadcasted_iota(jnp.int32, (width, width), 0) // RWKV_HEAD
    b = lax.broadcasted_iota(jnp.int32, (width, width), 1) // RWKV_HEAD
    return (a == b).astype(BF16)


def _rwkv_prep_kernel(k_ref, zw_ref, za_ref, zg_ref, w2_ref, a2_ref, g2_ref, w0_ref, a0_ref, kk_ref, ka_ref,
                      dec_ref, cl_ref, kmod_ref, na_ref, nb_ref, g_ref):
    k = k_ref[...]
    dw = jnp.dot(jnp.tanh(zw_ref[...]).astype(BF16), w2_ref[...], preferred_element_type=F32)
    x = w0_ref[...] + dw
    w_log = _log_sigmoid(x) - 0.5
    log_dec = -jnp.exp(w_log)
    dec_ref[...] = jnp.exp(log_dec)
    tok = lax.broadcasted_iota(jnp.int32, (CHUNK, CHUNK), 0)
    src = lax.broadcasted_iota(jnp.int32, (CHUNK, CHUNK), 1)
    upto = (src <= tok).astype(BF16)
    for c in range(k.shape[0] // CHUNK):
        rows = slice(c * CHUNK, (c + 1) * CHUNK)
        cl_ref[rows, :] = _dot_by_01(log_dec[rows, :], upto, m01_first=True)
    a = _sigmoid(a0_ref[...] + jnp.dot(za_ref[...].astype(BF16), a2_ref[...], preferred_element_type=F32))
    g_ref[...] = jnp.dot(_sigmoid(zg_ref[...]).astype(BF16), g2_ref[...], preferred_element_type=F32)
    kk = k * kk_ref[...]
    hs = _head_sum_matrix(LANES)
    for c in range(k.shape[1] // LANES):
        sl = slice(c * LANES, (c + 1) * LANES)
        kc = kk[:, sl]
        ss = _dot_by_01(kc * kc, hs)
        kn = kc / jnp.maximum(jnp.sqrt(ss), 1e-12)
        na_ref[:, sl] = -kn
        nb_ref[:, sl] = kn * a[:, sl]
    kmod_ref[...] = k * (1.0 + (a - 1.0) * ka_ref[...])


def _rwkv_prep(zs, zoff, w2, a2, g2, w0, a0, k_k, k_a, d, tr):
    n = zs.shape[0]
    tc = _pick(d, (512, 256, 128))
    rdp, rip, rgp = w2.shape[0], a2.shape[0], g2.shape[0]
    row = lambda i, c: (i, c)
    vec = pl.BlockSpec((1, tc), lambda i, c: (0, c))
    out = jax.ShapeDtypeStruct((n, d), F32)
    return pl.pallas_call(
        _rwkv_prep_kernel,
        out_shape=[out] * 6,
        grid=(n // tr, d // tc),
        in_specs=[
            pl.BlockSpec((tr, tc), lambda i, c: (i, zoff["k"] // tc + c)),
            pl.BlockSpec((tr, rdp), lambda i, c: (i, zoff["zw"] // rdp)),
            pl.BlockSpec((tr, rip), lambda i, c: (i, zoff["za"] // rip)),
            pl.BlockSpec((tr, rgp), lambda i, c: (i, zoff["zg"] // rgp)),
            pl.BlockSpec((rdp, tc), lambda i, c: (0, c)),
            pl.BlockSpec((rip, tc), lambda i, c: (0, c)),
            pl.BlockSpec((rgp, tc), lambda i, c: (0, c)),
            vec, vec, vec, vec,
        ],
        out_specs=[pl.BlockSpec((tr, tc), row)] * 6,
        compiler_params=_params("parallel", "parallel"),
    )(zs, zs, zs, zs, w2, a2, g2, w0, a0, k_k, k_a)


def _scan_step(state, r, w, k, v, a, b, lo, eye0, eye1):
    def half_sums(x):
        s0 = jnp.sum(jnp.where(lo, x, 0.0), axis=1, keepdims=True)
        s1 = jnp.sum(jnp.where(lo, 0.0, x), axis=1, keepdims=True)
        return s0, s1

    sa0, sa1 = half_sums(state * a)
    vb = jnp.broadcast_to(v, state.shape)
    v0 = jnp.sum(jnp.where(eye0, vb, 0.0), axis=1, keepdims=True)
    v1 = jnp.sum(jnp.where(eye1, vb, 0.0), axis=1, keepdims=True)
    state = state * w + jnp.where(lo, sa0, sa1) * b + jnp.where(lo, v0, v1) * k
    y0, y1 = half_sums(state * r)
    y = jnp.sum(jnp.where(eye0, y0, jnp.where(eye1, y1, 0.0)), axis=0, keepdims=True)
    return state, y


def _scan_masks():
    shape = (RWKV_HEAD, LANES)
    i = lax.broadcasted_iota(jnp.int32, shape, 0)
    c = lax.broadcasted_iota(jnp.int32, shape, 1)
    return c < RWKV_HEAD, c == i, c == i + RWKV_HEAD


def _load_state(s_ref, p):
    return jnp.concatenate([s_ref[0, 2 * p], s_ref[0, 2 * p + 1]], axis=-1)


def _store_state(s_ref, p, state):
    s_ref[0, 2 * p] = state[:, :RWKV_HEAD]
    s_ref[0, 2 * p + 1] = state[:, RWKV_HEAD:]


def _mm(a, b):
    return jnp.dot(a.astype(BF16), b.astype(BF16), preferred_element_type=F32)


def _mm_nt(a, b):
    return lax.dot_general(a.astype(BF16), b.astype(BF16), (((1,), (1,)), ((), ())), preferred_element_type=F32)


def _mm_tn(a, b):
    return lax.dot_general(a.astype(BF16), b.astype(BF16), (((0,), (0,)), ((), ())), preferred_element_type=F32)


def _scan_chunk(sd, r, cl, k, v, a, b, lo, strict, incl, same_head):
    def blockdiag(x):
        return jnp.concatenate([jnp.where(lo, x, 0.0), jnp.where(lo, 0.0, x)], axis=0)

    pairs = range(len(sd))
    first = lax.broadcasted_iota(jnp.int32, cl[0].shape, 0) == 0
    g = [jnp.exp(cl[p]) for p in pairs]
    g_prev = [jnp.exp(jnp.where(first, 0.0, pltpu.roll(cl[p], 1, 0))) for p in pairs]
    g_inv = [jnp.exp(-cl[p]) for p in pairs]
    bt = [b[p] * g_inv[p] for p in pairs]
    kt = [k[p] * g_inv[p] for p in pairs]
    lhs = [jnp.concatenate([a[p] * g_prev[p], r[p] * g[p]], axis=0) for p in pairs]
    keys = [jnp.concatenate([jnp.where(lo, bt[p], 0.0), jnp.where(lo, 0.0, bt[p]),
                             jnp.where(lo, kt[p], 0.0), jnp.where(lo, 0.0, kt[p])], axis=0) for p in pairs]
    gram = [_mm_nt(lhs[p], keys[p]) for p in pairs]
    from_state = [_mm_nt(lhs[p], sd[p]) for p in pairs]
    nk = [jnp.where(strict, gram[p][:CHUNK, :LANES], 0.0) for p in pairs]
    vd = [blockdiag(v[p]) for p in pairs]
    x = [from_state[p][:CHUNK] + _mm(jnp.where(strict, gram[p][:CHUNK, LANES:], 0.0), vd[p]) for p in pairs]
    steps = CHUNK.bit_length() - 1
    for it in range(steps):
        x = [x[p] + _mm(nk[p], blockdiag(x[p])) for p in pairs]
        if it + 1 < steps:
            nk = [_mm(nk[p], blockdiag(nk[p])) for p in pairs]
    r_bk = [jnp.concatenate([jnp.where(incl, gram[p][CHUNK:, :LANES], 0.0),
                             jnp.where(incl, gram[p][CHUNK:, LANES:], 0.0)], axis=1) for p in pairs]
    y = [from_state[p][CHUNK:] + _mm(r_bk[p], jnp.concatenate([blockdiag(x[p]), vd[p]], axis=0)) for p in pairs]
    g_last = [g[p][CHUNK - 1:CHUNK, :] for p in pairs]
    update = [_mm_tn(jnp.concatenate([x[p], v[p]], axis=0),
                     jnp.concatenate([bt[p] * g_last[p], kt[p] * g_last[p]], axis=0)) for p in pairs]
    return [sd[p] * g_last[p] + jnp.where(same_head, update[p], 0.0) for p in pairs], y


def _scan_prompt_kernel(r_ref, cl_ref, k_ref, v_ref, a_ref, b_ref, y_ref, sout_ref, sd_ref, *, pairs, nchunk):
    tc = pl.program_id(2)

    @pl.when(tc == 0)
    def _():
        sd_ref[...] = jnp.zeros_like(sd_ref)

    lane = lax.broadcasted_iota(jnp.int32, (CHUNK, LANES), 1)
    tok = lax.broadcasted_iota(jnp.int32, (CHUNK, LANES), 0)
    lo = lane < RWKV_HEAD
    src = lane % RWKV_HEAD
    strict, incl = src < tok, src <= tok
    hr = lax.broadcasted_iota(jnp.int32, (LANES, LANES), 0) // RWKV_HEAD
    hc = lax.broadcasted_iota(jnp.int32, (LANES, LANES), 1) // RWKV_HEAD
    same_head = hr == hc

    def chunk(ci, carry):
        rows = pl.ds(pl.multiple_of(ci * CHUNK, CHUNK), CHUNK)
        cols = [slice(p * LANES, (p + 1) * LANES) for p in range(pairs)]
        take = lambda ref: [ref[rows, sl] for sl in cols]
        sd, y = _scan_chunk([sd_ref[p] for p in range(pairs)], take(r_ref), take(cl_ref), take(k_ref), take(v_ref),
                            take(a_ref), take(b_ref), lo, strict, incl, same_head)
        for p in range(pairs):
            sd_ref[p] = sd[p]
            y_ref[rows, cols[p]] = y[p]
        return carry

    lax.fori_loop(0, nchunk, chunk, 0)

    @pl.when(tc == pl.num_programs(2) - 1)
    def _():
        for p in range(pairs):
            sd = sd_ref[p]
            sout_ref[0, 2 * p] = sd[:RWKV_HEAD, :RWKV_HEAD]
            sout_ref[0, 2 * p + 1] = sd[RWKV_HEAD:, RWKV_HEAD:]


def _scan_sample_kernel(r_ref, w_ref, k_ref, v_ref, a_ref, b_ref, s0_ref, y_ref, sout_ref, *, pairs):
    lo, eye0, eye1 = _scan_masks()
    for p in range(pairs):
        sl = slice(p * LANES, (p + 1) * LANES)
        row = lambda ref: ref[0, :, sl]
        state, y = _scan_step(_load_state(s0_ref, p), row(r_ref), row(w_ref), row(k_ref), row(v_ref), row(a_ref),
                              row(b_ref), lo, eye0, eye1)
        _store_state(sout_ref, p, state)
        y_ref[0, :, sl] = y


def _rwkv_scan_prompt(r_src, r_cb, v_cb, cl, kmod, na, nb, b, t, d):
    pairs = _pick(d // LANES, (8, 4, 2, 1))
    wcol = pairs * LANES
    tchunk = _pick(t, (256, 128))
    nt = t // tchunk
    nh = d // RWKV_HEAD
    rowmap = lambda bi, g, tc: (bi * nt + tc, g)
    blk = pl.BlockSpec((tchunk, wcol), rowmap)
    return pl.pallas_call(
        functools.partial(_scan_prompt_kernel, pairs=pairs, nchunk=tchunk // CHUNK),
        out_shape=[jax.ShapeDtypeStruct((cl.shape[0], d), F32), jax.ShapeDtypeStruct((b, nh, RWKV_HEAD, RWKV_HEAD), F32)],
        grid=(b, d // wcol, nt),
        in_specs=[
            pl.BlockSpec((tchunk, wcol), lambda bi, g, tc: (bi * nt + tc, r_cb // pairs + g)),
            blk, blk,
            pl.BlockSpec((tchunk, wcol), lambda bi, g, tc: (bi * nt + tc, v_cb // pairs + g)),
            blk, blk,
        ],
        out_specs=[blk, pl.BlockSpec((1, 2 * pairs, RWKV_HEAD, RWKV_HEAD), lambda bi, g, tc: (bi, g, 0, 0))],
        scratch_shapes=[pltpu.VMEM((pairs, LANES, LANES), F32)],
        compiler_params=_params("parallel", "parallel", "arbitrary"),
    )(r_src, cl, kmod, r_src, na, nb)


def _rwkv_scan_sample(r, dec, kmod, v, na, nb, s0):
    bs, _, d = r.shape
    pairs = _pick(d // LANES, (4, 2, 1))
    wcol = pairs * LANES
    blk = pl.BlockSpec((1, 1, wcol), lambda bi, g: (bi, 0, g))
    sblk = pl.BlockSpec((1, 2 * pairs, RWKV_HEAD, RWKV_HEAD), lambda bi, g: (bi, g, 0, 0))
    return pl.pallas_call(
        functools.partial(_scan_sample_kernel, pairs=pairs),
        out_shape=[jax.ShapeDtypeStruct((bs, 1, d), F32), jax.ShapeDtypeStruct(s0.shape, F32)],
        grid=(bs, d // wcol),
        in_specs=[blk] * 6 + [sblk],
        out_specs=[blk, sblk],
        compiler_params=_params("parallel", "parallel"),
    )(r, dec, kmod, v, na, nb, s0)


def _rwkv_post_kernel(y_ref, r_ref, k_ref, v_ref, g_ref, rk_ref, lw_ref, lb_ref, gf_ref, gr_ref, gs_ref, of_ref, os_ref,
                      o_ref):
    hs = _head_sum_matrix(LANES)
    inv = 1.0 / RWKV_HEAD
    for c in range(y_ref.shape[1] // LANES):
        sl = slice(c * LANES, (c + 1) * LANES)
        y = y_ref[:, sl]
        mu = _dot_by_01(y, hs) * inv
        yc = y - mu
        var = _dot_by_01(yc * yc, hs) * inv
        yn = yc * lax.rsqrt(var + RWKV_GN_EPS) * lw_ref[:, sl] + lb_ref[:, sl]
        v = v_ref[:, sl]
        bonus = _dot_by_01(r_ref[:, sl] * k_ref[:, sl] * rk_ref[:, sl], hs) * v
        o_rwkv = (yn + bonus) * g_ref[:, sl]
        o_ref[:, sl] = (gf_ref[:, sl] * of_ref[:, sl] + gr_ref[:, sl] * o_rwkv + gs_ref[:, sl] * os_ref[:, sl]).astype(BF16)


def _rwkv_post_merge(y, zs, zoff, kmod, g, r_k, ln_w, ln_b, p, lay, o_fox, o_sb, d, tr):
    n = y.shape[0]
    tc = _pick(d, tuple(c for c in (512, 256, 128) if lay.off["gates"] % c == 0))
    gb = lay.off["gates"] // tc
    per = d // tc
    blk = pl.BlockSpec((tr, tc), lambda i, c: (i, c))
    vec = pl.BlockSpec((1, tc), lambda i, c: (0, c))
    gate = lambda which: pl.BlockSpec((tr, tc), lambda i, c: (i, gb + which * per + c))
    return pl.pallas_call(
        _rwkv_post_kernel,
        out_shape=jax.ShapeDtypeStruct((n, d), BF16),
        grid=(n // tr, d // tc),
        in_specs=[
            blk,
            pl.BlockSpec((tr, tc), lambda i, c: (i, zoff["r"] // tc + c)),
            blk,
            pl.BlockSpec((tr, tc), lambda i, c: (i, zoff["v"] // tc + c)),
            blk, vec, vec, vec,
            gate(0), gate(1), gate(2), blk, blk,
        ],
        out_specs=blk,
        compiler_params=_params("parallel", "parallel"),
    )(y, zs, kmod, zs, g, r_k, ln_w, ln_b, p, p, p, o_fox, o_sb)


def _outproj_kernel(m_ref, w_ref, x_ref, o_ref):
    o_ref[...] = x_ref[...] + jnp.dot(m_ref[...], w_ref[...], preferred_element_type=F32)


def _outproj(merged, w, x, tm):
    n, d = merged.shape
    tn = _pick(w.shape[1], (512, 256, 128))
    return pl.pallas_call(
        _outproj_kernel,
        out_shape=jax.ShapeDtypeStruct((n, w.shape[1]), F32),
        grid=(w.shape[1] // tn, n // tm),
        in_specs=[
            pl.BlockSpec((tm, d), lambda j, i: (i, 0)),
            pl.BlockSpec((d, tn), lambda j, i: (0, j)),
            pl.BlockSpec((tm, tn), lambda j, i: (i, j)),
        ],
        out_specs=pl.BlockSpec((tm, tn), lambda j, i: (i, j)),
        compiler_params=_params("parallel", "parallel"),
    )(merged, w, x)


def _top16_rows(s):
    nk = s.shape[0]
    idx = lax.broadcasted_iota(jnp.int32, s.shape, 0).astype(F32)
    work = s
    kept = jnp.full(s.shape, -jnp.inf, F32)
    tops, where = [], []
    for _ in range(PEER_TOPK):
        m = jnp.max(work, axis=0, keepdims=True)
        first = jnp.min(jnp.where(work == m, idx, float(nk)), axis=0, keepdims=True)
        hit = idx == first
        kept = jnp.where(hit, s, kept)
        work = jnp.where(hit, -jnp.inf, work)
        tops.append(m)
        where.append(first)
    return jnp.concatenate(tops, axis=0), jnp.concatenate(where, axis=0), kept


def _kth_largest(c, order, kth):
    m = first = None
    for _ in range(kth):
        m = jnp.max(c, axis=0, keepdims=True)
        first = jnp.min(jnp.where(c == m, order, jnp.inf), axis=0, keepdims=True)
        c = jnp.where(order == first, -jnp.inf, c)
    return m, first


STAT_THR, STAT_MAX, STAT_INVZ, STAT_S1, STAT_I, STAT_S2, STAT_J, STAT_TIED = range(8)


def _peer_gate_kernel(ht_ref, wq_ref, sk_ref, s1_ref, s2_ref, st_ref):
    qt = jnp.dot(wq_ref[0], ht_ref[...], preferred_element_type=F32)
    half = qt.shape[0] // 2
    sc = [jnp.dot(sk_ref[0, p], qt[p * half:(p + 1) * half].astype(BF16), preferred_element_type=F32)
          for p in range(2)]
    t1, i1, s1m = _top16_rows(sc[0])
    t2, i2, s2m = _top16_rows(sc[1])
    tn = t1.shape[1]
    pairs = [(a, b) for a in range(PEER_TOPK) for b in range(PEER_TOPK) if (a + 1) * (b + 1) <= PEER_TOPK]
    pad = -len(pairs) % SUBLANES
    cand = jnp.concatenate([t1[a:a + 1] + t2[b:b + 1] for a, b in pairs] + [jnp.full((pad, tn), -jnp.inf, F32)], axis=0)
    flat = jnp.concatenate([jnp.full((1, tn), float(a * PEER_TOPK + b), F32) for a, b in pairs]
                           + [jnp.full((pad, tn), float(PEER_TOPK * PEER_TOPK), F32)], axis=0)
    thr, last = _kth_largest(cand, flat, PEER_TOPK)
    kept = jnp.logical_or(cand > thr, jnp.logical_and(cand == thr, flat <= last))
    top = t1[0:1] + t2[0:1]
    zsum = jnp.sum(jnp.where(kept, jnp.exp(cand - top), 0.0), axis=0, keepdims=True)
    rank = lax.broadcasted_iota(jnp.int32, t1.shape, 0).astype(F32)
    a_last = jnp.floor(last * (1.0 / PEER_TOPK))
    b_last = last - a_last * PEER_TOPK
    pick = lambda table, r: jnp.sum(jnp.where(rank == r, table, 0.0), axis=0, keepdims=True)
    grid = (t1[:, None, :] + t2[None, :, :]).reshape(PEER_TOPK * PEER_TOPK, tn)
    tied = (jnp.sum(jnp.where(grid == thr, 1.0, 0.0), axis=0, keepdims=True)
            - jnp.sum(jnp.where(jnp.logical_and(cand == thr, kept), 1.0, 0.0), axis=0, keepdims=True))
    s1_ref[0] = s1m
    s2_ref[0] = s2m
    st_ref[0] = jnp.concatenate([thr, top, 1.0 / zsum, pick(t1, a_last), pick(i1, a_last), pick(t2, b_last),
                                 pick(i2, b_last), tied], axis=0)


def _peer_gate(ht, wq_t, subkeys, tn):
    d, n = ht.shape
    hp, _, nk, half = subkeys.shape
    arr = jax.ShapeDtypeStruct((hp, nk, n), F32)
    blk = pl.BlockSpec((1, nk, tn), lambda i, hh: (hh, 0, i))
    return pl.pallas_call(
        _peer_gate_kernel,
        out_shape=[arr, arr, jax.ShapeDtypeStruct((hp, SUBLANES, n), F32)],
        grid=(n // tn, hp),
        in_specs=[
            pl.BlockSpec((d, tn), lambda i, hh: (0, i)),
            pl.BlockSpec((1, 2 * half, d), lambda i, hh: (hh, 0, 0)),
            pl.BlockSpec((1, 2, nk, half), lambda i, hh: (hh, 0, 0, 0)),
        ],
        out_specs=[blk, blk, pl.BlockSpec((1, SUBLANES, tn), lambda i, hh: (hh, 0, i))],
        compiler_params=_params("parallel", "arbitrary"),
    )(ht, wq_t, subkeys)


def _gelu_exact(x):
    return 0.5 * x * (1.0 + lax.erf(x * (2.0 ** -0.5)))


def _peer_dense_kernel(tie_ref, ht_ref, u_ref, vt_ref, s1_ref, s2_ref, st_ref, o_ref, coef_ref,
                       *, rows_per_step, lane_tile):
    e = pl.program_id(1)
    hp, nk, tn = s2_ref.shape

    @pl.when(e == 0)
    def _():
        o_ref[...] = jnp.zeros_like(o_ref)

    per_group = SUBLANES // rows_per_step
    base = pl.multiple_of((e // per_group) * SUBLANES, SUBLANES)
    sub = e % per_group

    def fill_coef(break_ties):
        for lt in range(tn // lane_tile):
            ls = slice(lt * lane_tile, (lt + 1) * lane_tile)
            act = _gelu_exact(jnp.dot(u_ref[...], ht_ref[:, ls], preferred_element_type=F32))
            for ii in range(rows_per_step):
                gate = jnp.zeros((nk, lane_tile), F32)
                for hh in range(hp):
                    st = st_ref[hh, :, ls]
                    row = lambda r: st[r:r + 1]
                    grp = s1_ref[hh, pl.ds(base, SUBLANES), ls]
                    s1 = grp[ii:ii + 1]
                    for o in range(1, per_group):
                        s1 = jnp.where(sub == o, grp[o * rows_per_step + ii:o * rows_per_step + ii + 1], s1)
                    s2 = s2_ref[hh, :, ls]
                    val = s1 + s2
                    if break_ties:
                        i = (e * rows_per_step + ii).astype(F32)
                        j = lax.broadcasted_iota(jnp.int32, s2.shape, 0).astype(F32)
                        same_i = i == row(STAT_I)
                        earlier = jnp.logical_or(s1 > row(STAT_S1), jnp.logical_and(s1 == row(STAT_S1), i < row(STAT_I)))
                        within = jnp.logical_or(s2 > row(STAT_S2), jnp.logical_and(s2 == row(STAT_S2), j <= row(STAT_J)))
                        tie_ok = jnp.logical_or(earlier, jnp.logical_and(same_i, within))
                        keep = jnp.logical_or(val > row(STAT_THR), jnp.logical_and(val == row(STAT_THR), tie_ok))
                    else:
                        keep = val >= row(STAT_THR)
                    weight = jnp.exp(val - row(STAT_MAX)) * row(STAT_INVZ)
                    gate = gate + jnp.where(keep, weight, 0.0)
                coef_ref[ii * nk:(ii + 1) * nk, ls] = (gate * act[ii * nk:(ii + 1) * nk]).astype(BF16)

    tied = tie_ref[pl.program_id(0)] != 0

    @pl.when(jnp.logical_not(tied))
    def _():
        fill_coef(False)
        o_ref[...] += jnp.dot(vt_ref[0], coef_ref[...], preferred_element_type=F32)

    @pl.when(tied)
    def _():
        fill_coef(True)
        o_ref[...] += jnp.dot(vt_ref[0], coef_ref[...], preferred_element_type=F32)


def _peer_dense(ht, u, vt, s1, s2, stats, tn):
    d, n = ht.shape
    hp, nk, _ = s1.shape
    ne = u.shape[0]
    te = vt.shape[2]
    once = dict(pipeline_mode=pl.Buffered(1))
    full = pl.BlockSpec((hp, nk, tn), lambda i, e, tie: (0, 0, i), **once)
    tie_flags = (jnp.max(stats[:, STAT_TIED, :].reshape(hp, n // tn, tn), axis=(0, 2)) > 0).astype(jnp.int32)
    return pl.pallas_call(
        functools.partial(_peer_dense_kernel, rows_per_step=te // nk, lane_tile=_pick(tn, (256, 128))),
        out_shape=jax.ShapeDtypeStruct((d, n), F32),
        grid_spec=pltpu.PrefetchScalarGridSpec(
            num_scalar_prefetch=1,
            grid=(n // tn, ne // te),
            in_specs=[
                pl.BlockSpec((d, tn), lambda i, e, tie: (0, i), **once),
                pl.BlockSpec((te, d), lambda i, e, tie: (e, 0)),
                pl.BlockSpec((1, d, te), lambda i, e, tie: (e, 0, 0)),
                full, full,
                pl.BlockSpec((hp, SUBLANES, tn), lambda i, e, tie: (0, 0, i), **once),
            ],
            out_specs=pl.BlockSpec((d, tn), lambda i, e, tie: (0, i)),
            scratch_shapes=[pltpu.VMEM((te, tn), BF16)],
        ),
        compiler_params=pltpu.CompilerParams(dimension_semantics=("parallel", "arbitrary"),
                                             vmem_limit_bytes=PEER_VMEM_LIMIT_BYTES),
    )(tie_flags, ht, u, vt, s1, s2, stats)


def _residual_t_kernel(x_ref, yt_ref, o_ref):
    o_ref[...] = x_ref[...] + yt_ref[...].T


def _add_transposed(x, yt, tm):
    n, d = x.shape
    return pl.pallas_call(
        _residual_t_kernel,
        out_shape=jax.ShapeDtypeStruct((n, d), F32),
        grid=(n // tm,),
        in_specs=[pl.BlockSpec((tm, d), lambda i: (i, 0)), pl.BlockSpec((d, tm), lambda i: (0, i))],
        out_specs=pl.BlockSpec((tm, d), lambda i: (i, 0)),
        compiler_params=_params("parallel"),
    )(x, yt)


def kernel(x_prompt, x_sample, cache_fox_k, cache_fox_v, cache_fox_logf, cache_sb_k, cache_sb_v, state_rwkv, state_rwkv_shift, page_table, norm_mix, w_in, b_forget, fox_q_norm, fox_k_norm, sb_q_norm, sb_k_norm, rwkv_mu, rwkv_w0, rwkv_w2, rwkv_a0, rwkv_a2, rwkv_g2, rwkv_k_k, rwkv_k_a, rwkv_r_k, rwkv_ln_w, rwkv_ln_b, w_out, norm_ffn, peer_wq, peer_subkeys, peer_u, peer_v):
    b, t, d = x_prompt.shape
    bs = x_sample.shape[0]
    depth = w_in.shape[0]
    hq = d // HEAD_DIM
    hkv = hq // GQA
    dkv = hkv * HEAD_DIM
    rd, ri, rg = rwkv_w2.shape[1], rwkv_a2.shape[1], rwkv_g2.shape[1]
    n_prompt = b * t
    tr = _pick(t, (256, 128))
    n = _round_up(n_prompt + bs, tr)
    tm = _pick(n, (768, 512, 384, 256, 128))
    tm_in = _pick(n, (1408, 768, 512, 384, 256, 128))
    tn_in = 512 if d % 2048 == 0 else LANES
    lay = _Layout(d, hq, rd, ri, rg, tn_in)
    zoff = {k: lay.off[k] - lay.z_lo for k in ("r", "k", "v", "zw", "za", "zg")}
    rdp, rip, rgp = _round_up(rd, LANES), _round_up(ri, LANES), _round_up(rg, LANES)
    tn_peer = _pick(n, (768, 256, 128))
    te_peer = 4 * peer_subkeys.shape[3]

    def tile_heads(g, count):
        return jnp.tile(g, count)

    def pad_rows(a, rows):
        return jnp.concatenate([a, jnp.zeros((rows - a.shape[0],) + a.shape[1:], a.dtype)], axis=0)

    def with_tail(full, sample_rows):
        return full.at[n_prompt:].set(pad_rows(sample_rows, n - n_prompt))

    def one_row(a):
        return a.reshape(bs, 1, d)

    def zr_original_order(rows):
        return jnp.concatenate([rows[..., zoff[k]:zoff[k] + w] for k, w in
                                (("r", d), ("k", d), ("v", d), ("zw", rd), ("za", ri), ("zg", rg))], axis=-1)

    def to_z_layout(a):
        parts, o, src = [], 0, 0
        for k, w in (("r", d), ("k", d), ("v", d), ("zw", rd), ("za", ri), ("zg", rg)):
            if zoff[k] > o:
                parts.append(jnp.zeros(a.shape[:-1] + (zoff[k] - o,), a.dtype))
            parts.append(a[..., src:src + w])
            o, src = zoff[k] + w, src + w
        if lay.z_width > o:
            parts.append(jnp.zeros(a.shape[:-1] + (lay.z_width - o,), a.dtype))
        return jnp.concatenate(parts, axis=-1)

    x = pad_rows(jnp.concatenate([x_prompt.reshape(n_prompt, d), x_sample.reshape(bs, d)], axis=0), n)
    n_pool, page = cache_fox_logf.shape[1], cache_fox_logf.shape[2]
    lf_cache = jnp.swapaxes(cache_fox_logf, 2, 3).reshape(depth * n_pool, hq, page)
    lf_cache = _pad_heads(lf_cache, hkv).reshape(depth, n_pool, hkv * DEC_ROWS, page)

    prompt_states, sample_states = [], []
    for l in range(depth):
        h = _rmsnorm(x, norm_mix[l], tm)
        w_l = lay.scatter_cols(w_in[l], lay.ncol).astype(BF16)
        colp = jnp.zeros((lay.n_in,), F32)
        for name, g, cnt in (("fq", fox_q_norm[l], hq), ("fk", fox_k_norm[l], hkv), ("sq", sb_q_norm[l], hq),
                             ("sk", sb_k_norm[l], hkv)):
            s0, w0 = lay.src[name]
            colp = colp.at[s0:s0 + w0].set(tile_heads(g, cnt))
        s0, w0 = lay.src["ff"]
        colp = colp.at[s0:s0 + w0].set(b_forget[l])
        colp = lay.scatter_cols(colp.reshape(1, -1), lay.ncol)
        p = _inproj(h, w_l, colp, jnp.asarray(lay.modes), tm_in, tn_in)

        def cols(name, width, rows=slice(None)):
            return p[rows, lay.off[name]:lay.off[name] + width]

        c = _cumsum_time(p, b, t, lay.off["ff"] // LANES)[:, :hq].reshape(b, t, hq)
        tq = _pick(t, FOX_Q_TILES)
        c_t = jnp.swapaxes(c, 1, 2)
        cq = c_t.reshape(b, hq, t, 1)
        ck = c_t.reshape(b, hq, t // tq, 1, tq)
        o_fox_p = _prompt_attention(p, lay, b, t, d, cq, ck)
        o_sb_p = _prompt_attention(p, lay, b, t, d)

        srow = slice(n_prompt, n_prompt + bs)
        scale = HEAD_DIM ** -0.5
        fq_s = _pad_heads((cols("fq", d, srow) * scale).reshape(bs, hq, HEAD_DIM), hkv)
        sq_s = _pad_heads((cols("sq", d, srow) * scale).reshape(bs, hq, HEAD_DIM), hkv)
        fk_s = cols("fk", dkv, srow).reshape(bs, hkv, HEAD_DIM)
        fv_s = cols("fv", dkv, srow).reshape(bs, hkv, HEAD_DIM)
        lf_s = cols("ff", hq, srow)
        o_fox_s = _decode_attention(l, page_table, fq_s, cache_fox_k, cache_fox_v, lf_cache, fk_s, fv_s,
                                    _pad_heads(lf_s.reshape(bs, hq, 1), hkv))
        o_sb_s = _decode_attention(l, page_table, sq_s, cache_sb_k, cache_sb_v)
        o_fox = with_tail(o_fox_p, _unpad_heads(o_fox_s, hkv))
        o_sb = with_tail(o_sb_p, _unpad_heads(o_sb_s, hkv))

        state_pad = pad_rows(to_z_layout(state_rwkv_shift[l]), tr)
        zs = _token_shift(p, lay, state_pad, to_z_layout(rwkv_mu[l]).reshape(1, -1), t, n_prompt, tr)
        pad_k = lambda w2, rp: pad_rows(w2, rp).astype(BF16)
        vec = lambda a: a.reshape(1, d)
        dec, cl, kmod, na, nb, gg = _rwkv_prep(zs, zoff, pad_k(rwkv_w2[l], rdp), pad_k(rwkv_a2[l], rip),
                                               pad_k(rwkv_g2[l], rgp), vec(rwkv_w0[l]), vec(rwkv_a0[l]),
                                               vec(rwkv_k_k[l]), vec(rwkv_k_a[l]), d, tr)
        y_p, s_p = _rwkv_scan_prompt(zs, zoff["r"] // LANES, zoff["v"] // LANES, cl, kmod, na, nb, b, t, d)
        one = lambda a: a[srow].reshape(bs, 1, d)
        zs_s = zs[srow]
        y_s, s_s = _rwkv_scan_sample(one_row(zs_s[:, zoff["r"]:zoff["r"] + d]), one(dec), one(kmod),
                                     one_row(zs_s[:, zoff["v"]:zoff["v"] + d]), one(na), one(nb), state_rwkv[l])
        y = with_tail(y_p, y_s.reshape(bs, d))
        merged = _rwkv_post_merge(y, zs, zoff, kmod, gg, vec(rwkv_r_k[l].reshape(-1)), vec(rwkv_ln_w[l]),
                                  vec(rwkv_ln_b[l]), p, lay, o_fox, o_sb, d, tr)
        x = _outproj(merged, w_out[l].astype(BF16), x, tm)

        h2t = _rmsnorm(x, norm_ffn[l], tm, transposed=True)
        wq_t = peer_wq[l].T.reshape(-1, peer_wq.shape[-1] // peer_subkeys.shape[1], d).astype(BF16)
        s1, s2, stats = _peer_gate(h2t, wq_t, peer_subkeys[l].astype(BF16), tn_peer)
        vt = jnp.swapaxes(peer_v[l].reshape(-1, te_peer, d), 1, 2).astype(BF16)
        yt = _peer_dense(h2t, peer_u[l].astype(BF16), vt, s1, s2, stats, tn_peer)
        x = _add_transposed(x, yt, tr)

        last = jnp.arange(b) * t + (t - 1)
        zr_window = lambda rows: zr_original_order(p[rows][:, lay.z_lo:lay.z_lo + lay.z_width])
        prompt_states.append((
            cols("fk", dkv, slice(0, n_prompt)).reshape(b, t, hkv, HEAD_DIM),
            cols("fv", dkv, slice(0, n_prompt)).reshape(b, t, hkv, HEAD_DIM),
            cols("ff", hq, slice(0, n_prompt)).reshape(b, t, hq),
            cols("sk", dkv, slice(0, n_prompt)).reshape(b, t, hkv, HEAD_DIM),
            cols("sv", dkv, slice(0, n_prompt)).reshape(b, t, hkv, HEAD_DIM),
            s_p, zr_window(last)))
        sample_states.append((
            fk_s.reshape(bs, 1, hkv, HEAD_DIM), fv_s.reshape(bs, 1, hkv, HEAD_DIM), lf_s.reshape(bs, 1, hq),
            cols("sk", dkv, srow).reshape(bs, 1, hkv, HEAD_DIM), cols("sv", dkv, srow).reshape(bs, 1, hkv, HEAD_DIM),
            s_s, zr_window(srow)))

    stack = lambda states, i: jnp.stack([s[i] for s in states])
    y_prompt = x[:n_prompt].reshape(b, t, d)
    y_sample = x[n_prompt:n_prompt + bs].reshape(bs, 1, d)
    return ((y_prompt, y_sample) + tuple(stack(prompt_states, i) for i in range(7))
            + tuple(stack(sample_states, i) for i in range(7)))
```

```python
import functools

import numpy as np
import jax
import jax.numpy as jnp
from jax import lax
from jax.experimental import pallas as pl
from jax.experimental.pallas import tpu as pltpu

F32 = jnp.float32
BF16 = jnp.bfloat16
HIGHEST = lax.Precision.HIGHEST

NORM_EPS = 1e-6
RWKV_GN_EPS = 64e-5
LANES = 128
SUBLANES = 8
HEAD_DIM = 128
RWKV_HEAD = 64
CHUNK = 64
GQA = 4
FOX_Q_TILES = (512, 256, 128)
SB_Q_TILES = (256, 128)
PEER_TOPK = 16
VMEM_LIMIT_BYTES = 56 * 1024 * 1024
PEER_VMEM_LIMIT_BYTES = 60 * 1024 * 1024

MODE_ID, MODE_RMS, MODE_SIGMOID, MODE_LOGSIG = 0, 1, 2, 3


def _params(*sem):
    return pltpu.CompilerParams(dimension_semantics=sem, vmem_limit_bytes=VMEM_LIMIT_BYTES)


def _round_up(n, m):
    return (n + m - 1) // m * m


def _pick(n, cands):
    for c in cands:
        if c <= n and n % c == 0:
            return c
    raise ValueError(f"no tile for {n} in {cands}")


def _sigmoid(x):
    return 1.0 / (1.0 + jnp.exp(-x))


def _dot_by_01(x, m01, m01_first=False, pieces=3):
    mm = (lambda piece: jnp.dot(m01, piece, preferred_element_type=F32)) if m01_first else (
        lambda piece: jnp.dot(piece, m01, preferred_element_type=F32))
    out, rest = None, x
    for _ in range(pieces):
        piece = rest.astype(BF16)
        rest = rest - piece.astype(F32)
        out = mm(piece) if out is None else out + mm(piece)
    return out


def _log_sigmoid(x):
    return jnp.minimum(x, 0.0) - jnp.log1p(jnp.exp(-jnp.abs(x)))


def _log_sigmoid_bulk(x):
    return jnp.minimum(x, 0.0) - jnp.log(1.0 + jnp.exp(-jnp.abs(x)))


def _rmsnorm_kernel(x_ref, g_ref, o_ref, *, transposed):
    x = x_ref[...]
    ms = jnp.mean(x * x, axis=-1, keepdims=True)
    y = x * lax.rsqrt(ms + NORM_EPS) * g_ref[...]
    o_ref[...] = (y.T if transposed else y).astype(o_ref.dtype)


def _rmsnorm(x, g, tm, transposed=False):
    n, d = x.shape
    return pl.pallas_call(
        functools.partial(_rmsnorm_kernel, transposed=transposed),
        out_shape=jax.ShapeDtypeStruct((d, n) if transposed else (n, d), BF16),
        grid=(n // tm,),
        in_specs=[pl.BlockSpec((tm, d), lambda i: (i, 0)), pl.BlockSpec((1, d), lambda i: (0, 0))],
        out_specs=pl.BlockSpec((d, tm), lambda i: (0, i)) if transposed else pl.BlockSpec((tm, d), lambda i: (i, 0)),
        compiler_params=_params("parallel"),
    )(x, g.reshape(1, d))


def _inproj_kernel(modes_ref, h_ref, w_ref, cp_ref, o_ref):
    mode = modes_ref[pl.program_id(1)]
    project = lambda: jnp.dot(h_ref[...], w_ref[...], preferred_element_type=F32)

    @pl.when(mode == MODE_ID)
    def _():
        o_ref[...] = project()

    @pl.when(mode == MODE_RMS)
    def _():
        acc = project()
        for c in range(acc.shape[1] // HEAD_DIM):
            sl = slice(c * HEAD_DIM, (c + 1) * HEAD_DIM)
            xs = acc[:, sl]
            ms = jnp.mean(xs * xs, axis=-1, keepdims=True)
            o_ref[:, sl] = xs * lax.rsqrt(ms + NORM_EPS) * cp_ref[:, sl]

    @pl.when(mode == MODE_SIGMOID)
    def _():
        o_ref[...] = _sigmoid(project())

    @pl.when(mode == MODE_LOGSIG)
    def _():
        o_ref[...] = _log_sigmoid(project() + cp_ref[...])


def _inproj(h, w, colp, modes, tm, tn):
    n, d = h.shape
    ncol = w.shape[1]
    return pl.pallas_call(
        _inproj_kernel,
        out_shape=jax.ShapeDtypeStruct((n, ncol), F32),
        grid_spec=pltpu.PrefetchScalarGridSpec(
            num_scalar_prefetch=1,
            grid=(n // tm, ncol // tn),
            in_specs=[
                pl.BlockSpec((tm, d), lambda i, j, m: (i, 0)),
                pl.BlockSpec((d, tn), lambda i, j, m: (0, j)),
                pl.BlockSpec((1, tn), lambda i, j, m: (0, j)),
            ],
            out_specs=pl.BlockSpec((tm, tn), lambda i, j, m: (i, j)),
        ),
        compiler_params=_params("parallel", "parallel"),
    )(modes, h, w, colp)


class _Layout:
    def __init__(self, d, hq, rd, ri, rg, tn):
        dkv = d // GQA
        zw = 3 * d + rd + ri + rg
        src = {}
        o = 0
        for name, width in (("fq", d), ("fk", dkv), ("fv", dkv), ("ff", hq), ("sq", d), ("sk", dkv), ("sv", dkv)):
            src[name] = (o, width)
            o += width
        for name, width in (("r", d), ("k", d), ("v", d), ("zw", rd), ("za", ri), ("zg", rg)):
            src[name] = (o, width)
            o += width
        src["gates"] = (o, 3 * d)
        self.n_in = o + 3 * d
        self.zr_width = zw
        order = (("fq", MODE_RMS), ("sq", MODE_RMS), ("fk", MODE_RMS), ("fv", MODE_ID), ("sk", MODE_RMS),
                 ("sv", MODE_ID), ("r", MODE_ID), ("k", MODE_ID), ("v", MODE_ID), ("zw", MODE_ID),
                 ("za", MODE_ID), ("zg", MODE_ID), ("gates", MODE_SIGMOID), ("ff", MODE_LOGSIG))
        self.src = src
        self.off = {}
        self.pieces = []
        seg_modes = []
        o, prev = 0, None
        for name, mode in order:
            o = _round_up(o, tn if mode != prev else _round_up(min(src[name][1], 4 * LANES), LANES))
            self.off[name] = o
            seg_modes.append((o, mode))
            self.pieces.append((o, src[name][0], src[name][1]))
            o += src[name][1]
            prev = mode
        self.ncol = _round_up(o, tn)
        self.z_lo = self.off["r"]
        self.z_width = _round_up(self.off["zg"] + rg, LANES) - self.z_lo
        modes = np.zeros((self.ncol // tn,), np.int32)
        for t in range(self.ncol // tn):
            for so, m in seg_modes:
                if so <= t * tn:
                    modes[t] = m
        self.modes = modes

    def scatter_cols(self, a, width):
        parts, o = [], 0
        for dst, s, w in self.pieces:
            if dst > o:
                parts.append(jnp.zeros(a.shape[:-1] + (dst - o,), a.dtype))
            parts.append(a[..., s:s + w])
            o = dst + w
        if width > o:
            parts.append(jnp.zeros(a.shape[:-1] + (width - o,), a.dtype))
        return jnp.concatenate(parts, axis=-1)


def _cumsum_kernel(x_ref, o_ref, *, cb):
    t = x_ref.shape[0]
    row = lax.broadcasted_iota(jnp.int32, (cb, cb), 0)
    col = lax.broadcasted_iota(jnp.int32, (cb, cb), 1)
    tri = (row >= col).astype(F32)
    carry = jnp.zeros((1, x_ref.shape[1]), F32)
    for i in range(t // cb):
        c = jnp.dot(tri, x_ref[i * cb:(i + 1) * cb, :], precision=HIGHEST, preferred_element_type=F32) + carry
        o_ref[i * cb:(i + 1) * cb, :] = c
        carry = c[cb - 1:cb, :]


def _cumsum_time(p, b, t, col_block):
    cb = _pick(t, (256, 128, 64, 32, 16, 8))
    return pl.pallas_call(
        functools.partial(_cumsum_kernel, cb=cb),
        out_shape=jax.ShapeDtypeStruct((b * t, LANES), F32),
        grid=(b,),
        in_specs=[pl.BlockSpec((t, LANES), lambda i: (i, col_block))],
        out_specs=pl.BlockSpec((t, LANES), lambda i: (i, 0)),
        compiler_params=_params("parallel"),
    )(p)


def _stack_heads(q_ref, scale):
    tq = q_ref.shape[0]
    q = q_ref[...] * scale
    return jnp.concatenate([q[:, h * HEAD_DIM:(h + 1) * HEAD_DIM] for h in range(GQA)], axis=0).astype(BF16)


def _unstack_heads(o, o_ref):
    tq = o_ref.shape[0]
    for h in range(GQA):
        o_ref[:, h * HEAD_DIM:(h + 1) * HEAD_DIM] = o[h * tq:(h + 1) * tq, :]


def _fox_prompt_kernel(q_ref, k_ref, v_ref, cq_ref, ck_ref, o_ref, *, tq):
    qi = pl.program_id(2)
    rows = GQA * tq
    qs = _stack_heads(q_ref, HEAD_DIM ** -0.5)
    cq = cq_ref[0].reshape(rows, 1)

    def scores(ki):
        k = k_ref[pl.ds(pl.multiple_of(ki * tq, tq), tq), :].astype(BF16)
        s = lax.dot_general(qs, k, (((1,), (1,)), ((), ())), preferred_element_type=F32)
        ck = ck_ref[0, :, ki]
        return ((s + cq).reshape(GQA, tq, tq) - ck).reshape(rows, tq)

    def update(ki, s, carry):
        m, l, acc = carry
        m_new = jnp.maximum(m, jnp.max(s, axis=-1, keepdims=True))
        alpha = jnp.exp(m - m_new)
        p = jnp.exp(s - m_new)
        v = v_ref[pl.ds(pl.multiple_of(ki * tq, tq), tq), :].astype(BF16)
        acc = acc * alpha + jnp.dot(p.astype(BF16), v, preferred_element_type=F32)
        return m_new, l * alpha + jnp.sum(p, axis=-1, keepdims=True), acc

    init = (jnp.full((rows, 1), -jnp.inf, F32), jnp.zeros((rows, 1), F32), jnp.zeros((rows, HEAD_DIM), F32))
    carry = lax.fori_loop(0, qi, lambda ki, c: update(ki, scores(ki), c), init)
    r = lax.broadcasted_iota(jnp.int32, (GQA, tq, tq), 1)
    c = lax.broadcasted_iota(jnp.int32, (GQA, tq, tq), 2)
    s = jnp.where((c <= r).reshape(rows, tq), scores(qi), -jnp.inf)
    m, l, acc = update(qi, s, carry)
    _unstack_heads(acc / l, o_ref)


def _sb_prompt_kernel(q_ref, k_ref, v_ref, o_ref, *, tq):
    qi = pl.program_id(2)
    rows = GQA * tq
    qs = _stack_heads(q_ref, HEAD_DIM ** -0.5)
    jr = lax.broadcasted_iota(jnp.int32, (tq, tq), 0)
    sc = lax.broadcasted_iota(jnp.int32, (tq, tq), 1)
    later = (jr > sc).astype(BF16)

    def block(ki, carry, mask):
        run, acc = carry
        k = k_ref[pl.ds(pl.multiple_of(ki * tq, tq), tq), :].astype(BF16)
        v = v_ref[pl.ds(pl.multiple_of(ki * tq, tq), tq), :].astype(BF16)
        z = lax.dot_general(qs, k, (((1,), (1,)), ((), ())), preferred_element_type=F32)
        log_keep = _log_sigmoid_bulk(-z)
        if mask is not None:
            log_keep = jnp.where(mask, log_keep, 0.0)
        after = _dot_by_01(log_keep, later, pieces=2) + run
        w = jnp.exp(z + log_keep + after)
        if mask is not None:
            w = jnp.where(mask, w, 0.0)
        acc = acc + jnp.dot(w.astype(BF16), v, preferred_element_type=F32)
        return run + jnp.sum(log_keep, axis=-1, keepdims=True), acc

    r3 = lax.broadcasted_iota(jnp.int32, (GQA, tq, tq), 1)
    c3 = lax.broadcasted_iota(jnp.int32, (GQA, tq, tq), 2)
    carry = block(qi, (jnp.zeros((rows, 1), F32), jnp.zeros((rows, HEAD_DIM), F32)), (c3 < r3).reshape(rows, tq))
    run, acc = lax.fori_loop(0, qi, lambda it, c: block(qi - 1 - it, c, None), carry)
    _unstack_heads(acc, o_ref)


def _prompt_attention(p, lay, b, t, d, cq=None, ck=None):
    hkv = d // (GQA * HEAD_DIM)
    fox = cq is not None
    tq = ck.shape[-1] if fox else _pick(t, SB_Q_TILES)
    nq = t // tq
    gw = GQA * HEAD_DIM
    qn, kn, vn = ("fq", "fk", "fv") if fox else ("sq", "sk", "sv")
    qb, kb, vb = lay.off[qn] // gw, lay.off[kn] // HEAD_DIM, lay.off[vn] // HEAD_DIM
    in_specs = [
        pl.BlockSpec((tq, gw), lambda bi, g, qi: (bi * nq + qi, qb + g)),
        pl.BlockSpec((t, HEAD_DIM), lambda bi, g, qi: (bi, kb + g)),
        pl.BlockSpec((t, HEAD_DIM), lambda bi, g, qi: (bi, vb + g)),
    ]
    args = [p, p, p]
    if fox:
        in_specs += [
            pl.BlockSpec((1, GQA, tq, 1), lambda bi, g, qi: (bi, g, qi, 0)),
            pl.BlockSpec((1, GQA, nq, 1, tq), lambda bi, g, qi: (bi, g, 0, 0, 0)),
        ]
        args += [cq, ck]
        body = functools.partial(_fox_prompt_kernel, tq=tq)
    else:
        body = functools.partial(_sb_prompt_kernel, tq=tq)
    return pl.pallas_call(
        body,
        out_shape=jax.ShapeDtypeStruct((p.shape[0], d), F32),
        grid=(b, hkv, nq),
        in_specs=in_specs,
        out_specs=pl.BlockSpec((tq, gw), lambda bi, g, qi: (bi * nq + qi, g)),
        compiler_params=_params("parallel", "parallel", "arbitrary"),
    )(*args)


DEC_ROWS = 2 * SUBLANES


def _decode_kernel(pt_ref, q_ref, *rest, fox, hkv, pp):
    kc_refs, vc_refs, rest = rest[:pp], rest[pp:2 * pp], rest[2 * pp:]
    if fox:
        lf_refs = rest[:pp]
        knew_ref, vnew_ref, lfnew_ref, o_ref, m_ref, l_ref, run_ref, acc_ref = rest[pp:]
    else:
        o_ref, run_ref, acc_ref = rest
    pg = pl.program_id(1)
    npg = pl.num_programs(1)
    page = kc_refs[0].shape[2]
    rows = hkv * DEC_ROWS

    @pl.when(pg == 0)
    def _():
        if fox:
            for g in range(hkv):
                sl = slice(g * DEC_ROWS, (g + 1) * DEC_ROWS)
                qg = q_ref[0, sl, :].astype(BF16).astype(F32)
                kg = knew_ref[0, g:g + 1, :].astype(BF16).astype(F32)
                m_ref[sl, :] = jnp.sum(qg * kg, axis=-1, keepdims=True)
                acc_ref[sl, :] = jnp.broadcast_to(vnew_ref[0, g:g + 1, :].astype(BF16).astype(F32), (DEC_ROWS, HEAD_DIM))
            l_ref[...] = jnp.ones_like(l_ref)
            run_ref[...] = lfnew_ref[0]
        else:
            run_ref[...] = jnp.zeros_like(run_ref)
            acc_ref[...] = jnp.zeros_like(acc_ref)

    jr = lax.broadcasted_iota(jnp.int32, (page, page), 0)
    sc = lax.broadcasted_iota(jnp.int32, (page, page), 1)
    later = (jr > sc).astype(BF16)
    k_pages = [pltpu.einshape("khd->hkd", ref[0, 0]) for ref in kc_refs]
    v_pages = [pltpu.einshape("khd->hkd", ref[0, 0]) for ref in vc_refs]
    k_heads = [jnp.concatenate([kp[g] for kp in k_pages], axis=0).astype(BF16) for g in range(hkv)]
    v_heads = [jnp.concatenate([vp[g] for vp in v_pages], axis=0).astype(BF16) for g in range(hkv)]
    z = jnp.concatenate(
        [lax.dot_general(q_ref[0, g * DEC_ROWS:(g + 1) * DEC_ROWS, :].astype(BF16), k_heads[g],
                         (((1,), (1,)), ((), ())), preferred_element_type=F32) for g in range(hkv)], axis=0)
    run = run_ref[...]
    in_page = lambda x, j: x[:, j * page:(j + 1) * page]

    def later_sums(per_key):
        out, carry = [], run
        for j in range(pp):
            out.append(_dot_by_01(in_page(per_key, j), later) + carry)
            carry = carry + jnp.sum(in_page(per_key, j), axis=-1, keepdims=True)
        return jnp.concatenate(out, axis=1), carry

    if fox:
        bias, run_ref[...] = later_sums(jnp.concatenate([ref[0, 0] for ref in lf_refs], axis=1))
        s = z + bias
        m = m_ref[...]
        m_new = jnp.maximum(m, jnp.max(s, axis=-1, keepdims=True))
        alpha = jnp.exp(m - m_new)
        w = jnp.exp(s - m_new)
        l_ref[...] = l_ref[...] * alpha + jnp.sum(w, axis=-1, keepdims=True)
        m_ref[...] = m_new
    else:
        log_keep = _log_sigmoid_bulk(-z)
        after, run_ref[...] = later_sums(log_keep)
        w = jnp.exp(z + log_keep + after)
        alpha = None
    wb = w.astype(BF16)
    for g in range(hkv):
        sl = slice(g * DEC_ROWS, (g + 1) * DEC_ROWS)
        pv = jnp.dot(wb[sl, :], v_heads[g], preferred_element_type=F32)
        if fox:
            acc_ref[sl, :] = acc_ref[sl, :] * alpha[sl, :] + pv
        else:
            acc_ref[sl, :] = acc_ref[sl, :] + pv

    @pl.when(pg == npg - 1)
    def _():
        if fox:
            o_ref[0] = acc_ref[...] / l_ref[...]
        else:
            o_ref[0] = acc_ref[...]


def _decode_attention(layer, page_table, q, kcache, vcache, lf_cache=None, knew=None, vnew=None, lfnew=None):
    bs, npg = page_table.shape
    _, _, page, hkv, _ = kcache.shape
    rows = hkv * DEC_ROWS
    fox = lf_cache is not None
    pp = _pick(npg, (8, 4, 2, 1))
    nth_page = lambda j: (lambda b, p, pt: (layer, pt[b, npg - 1 - (p * pp + j)], 0, 0, 0))
    cache_specs = [pl.BlockSpec((1, 1, page, hkv, HEAD_DIM), nth_page(j)) for j in range(pp)]
    in_specs = [pl.BlockSpec((1, rows, HEAD_DIM), lambda b, p, pt: (b, 0, 0))] + cache_specs + cache_specs
    args = [q] + [kcache] * pp + [vcache] * pp
    scratch = [pltpu.VMEM((rows, 1), F32), pltpu.VMEM((rows, HEAD_DIM), F32)]
    if fox:
        in_specs += [pl.BlockSpec((1, 1, rows, page), lambda b, p, pt, j=j: (layer, pt[b, npg - 1 - (p * pp + j)], 0, 0))
                     for j in range(pp)]
        in_specs += [
            pl.BlockSpec((1, hkv, HEAD_DIM), lambda b, p, pt: (b, 0, 0)),
            pl.BlockSpec((1, hkv, HEAD_DIM), lambda b, p, pt: (b, 0, 0)),
            pl.BlockSpec((1, rows, 1), lambda b, p, pt: (b, 0, 0)),
        ]
        args += [lf_cache] * pp + [knew, vnew, lfnew]
        scratch = [pltpu.VMEM((rows, 1), F32), pltpu.VMEM((rows, 1), F32)] + scratch
    return pl.pallas_call(
        functools.partial(_decode_kernel, fox=fox, hkv=hkv, pp=pp),
        out_shape=jax.ShapeDtypeStruct((bs, rows, HEAD_DIM), F32),
        grid_spec=pltpu.PrefetchScalarGridSpec(
            num_scalar_prefetch=1,
            grid=(bs, npg // pp),
            in_specs=in_specs,
            out_specs=pl.BlockSpec((1, rows, HEAD_DIM), lambda b, p, pt: (b, 0, 0)),
            scratch_shapes=scratch,
        ),
        compiler_params=_params("parallel", "arbitrary"),
    )(page_table, *args)


def _pad_heads(a, hkv):
    bs, _, x = a.shape
    a = a.reshape(bs, hkv, GQA, x)
    a = jnp.concatenate([a, jnp.zeros((bs, hkv, DEC_ROWS - GQA, x), a.dtype)], axis=2)
    return a.reshape(bs, hkv * DEC_ROWS, x)


def _unpad_heads(a, hkv):
    bs, _, x = a.shape
    return a.reshape(bs, hkv, DEC_ROWS, x)[:, :, :GQA].reshape(bs, hkv * GQA * x)


def _shift_kernel(z_ref, prev_ref, st_ref, mu_ref, o_ref, *, tiles_per_seq, sample_tile):
    i = pl.program_id(0)
    z = z_ref[...]
    first = (i % tiles_per_seq) == 0
    row0 = jnp.where(first, 0.0, prev_ref[SUBLANES - 1:SUBLANES, :])
    rid = lax.broadcasted_iota(jnp.int32, z.shape, 0)
    zp = jnp.where(rid == 0, row0, pltpu.roll(z, 1, 0))
    zp = jnp.where(i == sample_tile, st_ref[...], zp)
    o_ref[...] = z + (zp - z) * mu_ref[...]


def _token_shift(p, lay, state_pad, mu, t, n_prompt, tr):
    n = p.shape[0]
    zp = lay.z_width
    tc = _pick(zp, tuple(c for c in (1024, 896, 768, 640, 512, 384, 256, 128) if lay.z_lo % c == 0))
    cb = lay.z_lo // tc
    per8 = tr // SUBLANES
    return pl.pallas_call(
        functools.partial(_shift_kernel, tiles_per_seq=t // tr, sample_tile=n_prompt // tr),
        out_shape=jax.ShapeDtypeStruct((n, zp), F32),
        grid=(n // tr, zp // tc),
        in_specs=[
            pl.BlockSpec((tr, tc), lambda i, c: (i, cb + c)),
            pl.BlockSpec((SUBLANES, tc), lambda i, c: (jnp.maximum(i * per8 - 1, 0), cb + c)),
            pl.BlockSpec((tr, tc), lambda i, c: (0, c)),
            pl.BlockSpec((1, tc), lambda i, c: (0, c)),
        ],
        out_specs=pl.BlockSpec((tr, tc), lambda i, c: (i, c)),
        compiler_params=_params("parallel", "parallel"),
    )(p, p, state_pad, mu)


def _head_sum_matrix(width):
    a = lax.broadcasted_iota(jnp.int32, (width, width), 0) // RWKV_HEAD
    b = lax.broadcasted_iota(jnp.int32, (width, width), 1) // RWKV_HEAD
    return (a == b).astype(BF16)


def _rwkv_prep_kernel(k_ref, zw_ref, za_ref, zg_ref, w2_ref, a2_ref, g2_ref, w0_ref, a0_ref, kk_ref, ka_ref,
                      dec_ref, cl_ref, kmod_ref, na_ref, nb_ref, g_ref):
    k = k_ref[...]
    dw = jnp.dot(jnp.tanh(zw_ref[...]).astype(BF16), w2_ref[...], preferred_element_type=F32)
    x = w0_ref[...] + dw
    w_log = _log_sigmoid(x) - 0.5
    log_dec = -jnp.exp(w_log)
    dec_ref[...] = jnp.exp(log_dec)
    tok = lax.broadcasted_iota(jnp.int32, (CHUNK, CHUNK), 0)
    src = lax.broadcasted_iota(jnp.int32, (CHUNK, CHUNK), 1)
    upto = (src <= tok).astype(BF16)
    for c in range(k.shape[0] // CHUNK):
        rows = slice(c * CHUNK, (c + 1) * CHUNK)
        cl_ref[rows, :] = _dot_by_01(log_dec[rows, :], upto, m01_first=True)
    a = _sigmoid(a0_ref[...] + jnp.dot(za_ref[...].astype(BF16), a2_ref[...], preferred_element_type=F32))
    g_ref[...] = jnp.dot(_sigmoid(zg_ref[...]).astype(BF16), g2_ref[...], preferred_element_type=F32)
    kk = k * kk_ref[...]
    hs = _head_sum_matrix(LANES)
    for c in range(k.shape[1] // LANES):
        sl = slice(c * LANES, (c + 1) * LANES)
        kc = kk[:, sl]
        ss = _dot_by_01(kc * kc, hs)
        kn = kc / jnp.maximum(jnp.sqrt(ss), 1e-12)
        na_ref[:, sl] = -kn
        nb_ref[:, sl] = kn * a[:, sl]
    kmod_ref[...] = k * (1.0 + (a - 1.0) * ka_ref[...])


def _rwkv_prep(zs, zoff, w2, a2, g2, w0, a0, k_k, k_a, d, tr):
    n = zs.shape[0]
    tc = _pick(d, (512, 256, 128))
    rdp, rip, rgp = w2.shape[0], a2.shape[0], g2.shape[0]
    row = lambda i, c: (i, c)
    vec = pl.BlockSpec((1, tc), lambda i, c: (0, c))
    out = jax.ShapeDtypeStruct((n, d), F32)
    return pl.pallas_call(
        _rwkv_prep_kernel,
        out_shape=[out] * 6,
        grid=(n // tr, d // tc),
        in_specs=[
            pl.BlockSpec((tr, tc), lambda i, c: (i, zoff["k"] // tc + c)),
            pl.BlockSpec((tr, rdp), lambda i, c: (i, zoff["zw"] // rdp)),
            pl.BlockSpec((tr, rip), lambda i, c: (i, zoff["za"] // rip)),
            pl.BlockSpec((tr, rgp), lambda i, c: (i, zoff["zg"] // rgp)),
            pl.BlockSpec((rdp, tc), lambda i, c: (0, c)),
            pl.BlockSpec((rip, tc), lambda i, c: (0, c)),
            pl.BlockSpec((rgp, tc), lambda i, c: (0, c)),
            vec, vec, vec, vec,
        ],
        out_specs=[pl.BlockSpec((tr, tc), row)] * 6,
        compiler_params=_params("parallel", "parallel"),
    )(zs, zs, zs, zs, w2, a2, g2, w0, a0, k_k, k_a)


def _scan_step(state, r, w, k, v, a, b, lo, eye0, eye1):
    def half_sums(x):
        s0 = jnp.sum(jnp.where(lo, x, 0.0), axis=1, keepdims=True)
        s1 = jnp.sum(jnp.where(lo, 0.0, x), axis=1, keepdims=True)
        return s0, s1

    sa0, sa1 = half_sums(state * a)
    vb = jnp.broadcast_to(v, state.shape)
    v0 = jnp.sum(jnp.where(eye0, vb, 0.0), axis=1, keepdims=True)
    v1 = jnp.sum(jnp.where(eye1, vb, 0.0), axis=1, keepdims=True)
    state = state * w + jnp.where(lo, sa0, sa1) * b + jnp.where(lo, v0, v1) * k
    y0, y1 = half_sums(state * r)
    y = jnp.sum(jnp.where(eye0, y0, jnp.where(eye1, y1, 0.0)), axis=0, keepdims=True)
    return state, y


def _scan_masks():
    shape = (RWKV_HEAD, LANES)
    i = lax.broadcasted_iota(jnp.int32, shape, 0)
    c = lax.broadcasted_iota(jnp.int32, shape, 1)
    return c < RWKV_HEAD, c == i, c == i + RWKV_HEAD


def _load_state(s_ref, p):
    return jnp.concatenate([s_ref[0, 2 * p], s_ref[0, 2 * p + 1]], axis=-1)


def _store_state(s_ref, p, state):
    s_ref[0, 2 * p] = state[:, :RWKV_HEAD]
    s_ref[0, 2 * p + 1] = state[:, RWKV_HEAD:]


def _mm(a, b):
    return jnp.dot(a.astype(BF16), b.astype(BF16), preferred_element_type=F32)


def _mm_nt(a, b):
    return lax.dot_general(a.astype(BF16), b.astype(BF16), (((1,), (1,)), ((), ())), preferred_element_type=F32)


def _mm_tn(a, b):
    return lax.dot_general(a.astype(BF16), b.astype(BF16), (((0,), (0,)), ((), ())), preferred_element_type=F32)


def _scan_chunk(sd, r, cl, k, v, a, b, lo, strict, incl, same_head):
    def blockdiag(x):
        return jnp.concatenate([jnp.where(lo, x, 0.0), jnp.where(lo, 0.0, x)], axis=0)

    pairs = range(len(sd))
    first = lax.broadcasted_iota(jnp.int32, cl[0].shape, 0) == 0
    g = [jnp.exp(cl[p]) for p in pairs]
    g_prev = [jnp.exp(jnp.where(first, 0.0, pltpu.roll(cl[p], 1, 0))) for p in pairs]
    g_inv = [jnp.exp(-cl[p]) for p in pairs]
    bt = [b[p] * g_inv[p] for p in pairs]
    kt = [k[p] * g_inv[p] for p in pairs]
    lhs = [jnp.concatenate([a[p] * g_prev[p], r[p] * g[p]], axis=0) for p in pairs]
    keys = [jnp.concatenate([jnp.where(lo, bt[p], 0.0), jnp.where(lo, 0.0, bt[p]),
                             jnp.where(lo, kt[p], 0.0), jnp.where(lo, 0.0, kt[p])], axis=0) for p in pairs]
    gram = [_mm_nt(lhs[p], keys[p]) for p in pairs]
    from_state = [_mm_nt(lhs[p], sd[p]) for p in pairs]
    nk = [jnp.where(strict, gram[p][:CHUNK, :LANES], 0.0) for p in pairs]
    vd = [blockdiag(v[p]) for p in pairs]
    x = [from_state[p][:CHUNK] + _mm(jnp.where(strict, gram[p][:CHUNK, LANES:], 0.0), vd[p]) for p in pairs]
    steps = CHUNK.bit_length() - 1
    for it in range(steps):
        x = [x[p] + _mm(nk[p], blockdiag(x[p])) for p in pairs]
        if it + 1 < steps:
            nk = [_mm(nk[p], blockdiag(nk[p])) for p in pairs]
    r_bk = [jnp.concatenate([jnp.where(incl, gram[p][CHUNK:, :LANES], 0.0),
                             jnp.where(incl, gram[p][CHUNK:, LANES:], 0.0)], axis=1) for p in pairs]
    y = [from_state[p][CHUNK:] + _mm(r_bk[p], jnp.concatenate([blockdiag(x[p]), vd[p]], axis=0)) for p in pairs]
    g_last = [g[p][CHUNK - 1:CHUNK, :] for p in pairs]
    update = [_mm_tn(jnp.concatenate([x[p], v[p]], axis=0),
                     jnp.concatenate([bt[p] * g_last[p], kt[p] * g_last[p]], axis=0)) for p in pairs]
    return [sd[p] * g_last[p] + jnp.where(same_head, update[p], 0.0) for p in pairs], y


def _scan_prompt_kernel(r_ref, cl_ref, k_ref, v_ref, a_ref, b_ref, y_ref, sout_ref, sd_ref, *, pairs, nchunk):
    tc = pl.program_id(2)

    @pl.when(tc == 0)
    def _():
        sd_ref[...] = jnp.zeros_like(sd_ref)

    lane = lax.broadcasted_iota(jnp.int32, (CHUNK, LANES), 1)
    tok = lax.broadcasted_iota(jnp.int32, (CHUNK, LANES), 0)
    lo = lane < RWKV_HEAD
    src = lane % RWKV_HEAD
    strict, incl = src < tok, src <= tok
    hr = lax.broadcasted_iota(jnp.int32, (LANES, LANES), 0) // RWKV_HEAD
    hc = lax.broadcasted_iota(jnp.int32, (LANES, LANES), 1) // RWKV_HEAD
    same_head = hr == hc

    def chunk(ci, carry):
        rows = pl.ds(pl.multiple_of(ci * CHUNK, CHUNK), CHUNK)
        cols = [slice(p * LANES, (p + 1) * LANES) for p in range(pairs)]
        take = lambda ref: [ref[rows, sl] for sl in cols]
        sd, y = _scan_chunk([sd_ref[p] for p in range(pairs)], take(r_ref), take(cl_ref), take(k_ref), take(v_ref),
                            take(a_ref), take(b_ref), lo, strict, incl, same_head)
        for p in range(pairs):
            sd_ref[p] = sd[p]
            y_ref[rows, cols[p]] = y[p]
        return carry

    lax.fori_loop(0, nchunk, chunk, 0)

    @pl.when(tc == pl.num_programs(2) - 1)
    def _():
        for p in range(pairs):
            sd = sd_ref[p]
            sout_ref[0, 2 * p] = sd[:RWKV_HEAD, :RWKV_HEAD]
            sout_ref[0, 2 * p + 1] = sd[RWKV_HEAD:, RWKV_HEAD:]


def _scan_sample_kernel(r_ref, w_ref, k_ref, v_ref, a_ref, b_ref, s0_ref, y_ref, sout_ref, *, pairs):
    lo, eye0, eye1 = _scan_masks()
    for p in range(pairs):
        sl = slice(p * LANES, (p + 1) * LANES)
        row = lambda ref: ref[0, :, sl]
        state, y = _scan_step(_load_state(s0_ref, p), row(r_ref), row(w_ref), row(k_ref), row(v_ref), row(a_ref),
                              row(b_ref), lo, eye0, eye1)
        _store_state(sout_ref, p, state)
        y_ref[0, :, sl] = y


def _rwkv_scan_prompt(r_src, r_cb, v_cb, cl, kmod, na, nb, b, t, d):
    pairs = _pick(d // LANES, (8, 4, 2, 1))
    wcol = pairs * LANES
    tchunk = _pick(t, (256, 128))
    nt = t // tchunk
    nh = d // RWKV_HEAD
    rowmap = lambda bi, g, tc: (bi * nt + tc, g)
    blk = pl.BlockSpec((tchunk, wcol), rowmap)
    return pl.pallas_call(
        functools.partial(_scan_prompt_kernel, pairs=pairs, nchunk=tchunk // CHUNK),
        out_shape=[jax.ShapeDtypeStruct((cl.shape[0], d), F32), jax.ShapeDtypeStruct((b, nh, RWKV_HEAD, RWKV_HEAD), F32)],
        grid=(b, d // wcol, nt),
        in_specs=[
            pl.BlockSpec((tchunk, wcol), lambda bi, g, tc: (bi * nt + tc, r_cb // pairs + g)),
            blk, blk,
            pl.BlockSpec((tchunk, wcol), lambda bi, g, tc: (bi * nt + tc, v_cb // pairs + g)),
            blk, blk,
        ],
        out_specs=[blk, pl.BlockSpec((1, 2 * pairs, RWKV_HEAD, RWKV_HEAD), lambda bi, g, tc: (bi, g, 0, 0))],
        scratch_shapes=[pltpu.VMEM((pairs, LANES, LANES), F32)],
        compiler_params=_params("parallel", "parallel", "arbitrary"),
    )(r_src, cl, kmod, r_src, na, nb)


def _rwkv_scan_sample(r, dec, kmod, v, na, nb, s0):
    bs, _, d = r.shape
    pairs = _pick(d // LANES, (4, 2, 1))
    wcol = pairs * LANES
    blk = pl.BlockSpec((1, 1, wcol), lambda bi, g: (bi, 0, g))
    sblk = pl.BlockSpec((1, 2 * pairs, RWKV_HEAD, RWKV_HEAD), lambda bi, g: (bi, g, 0, 0))
    return pl.pallas_call(
        functools.partial(_scan_sample_kernel, pairs=pairs),
        out_shape=[jax.ShapeDtypeStruct((bs, 1, d), F32), jax.ShapeDtypeStruct(s0.shape, F32)],
        grid=(bs, d // wcol),
        in_specs=[blk] * 6 + [sblk],
        out_specs=[blk, sblk],
        compiler_params=_params("parallel", "parallel"),
    )(r, dec, kmod, v, na, nb, s0)


def _rwkv_post_kernel(y_ref, r_ref, k_ref, v_ref, g_ref, rk_ref, lw_ref, lb_ref, gf_ref, gr_ref, gs_ref, of_ref, os_ref,
                      o_ref):
    hs = _head_sum_matrix(LANES)
    inv = 1.0 / RWKV_HEAD
    for c in range(y_ref.shape[1] // LANES):
        sl = slice(c * LANES, (c + 1) * LANES)
        y = y_ref[:, sl]
        mu = _dot_by_01(y, hs) * inv
        yc = y - mu
        var = _dot_by_01(yc * yc, hs) * inv
        yn = yc * lax.rsqrt(var + RWKV_GN_EPS) * lw_ref[:, sl] + lb_ref[:, sl]
        v = v_ref[:, sl]
        bonus = _dot_by_01(r_ref[:, sl] * k_ref[:, sl] * rk_ref[:, sl], hs) * v
        o_rwkv = (yn + bonus) * g_ref[:, sl]
        o_ref[:, sl] = (gf_ref[:, sl] * of_ref[:, sl] + gr_ref[:, sl] * o_rwkv + gs_ref[:, sl] * os_ref[:, sl]).astype(BF16)


def _rwkv_post_merge(y, zs, zoff, kmod, g, r_k, ln_w, ln_b, p, lay, o_fox, o_sb, d, tr):
    n = y.shape[0]
    tc = _pick(d, tuple(c for c in (512, 256, 128) if lay.off["gates"] % c == 0))
    gb = lay.off["gates"] // tc
    per = d // tc
    blk = pl.BlockSpec((tr, tc), lambda i, c: (i, c))
    vec = pl.BlockSpec((1, tc), lambda i, c: (0, c))
    gate = lambda which: pl.BlockSpec((tr, tc), lambda i, c: (i, gb + which * per + c))
    return pl.pallas_call(
        _rwkv_post_kernel,
        out_shape=jax.ShapeDtypeStruct((n, d), BF16),
        grid=(n // tr, d // tc),
        in_specs=[
            blk,
            pl.BlockSpec((tr, tc), lambda i, c: (i, zoff["r"] // tc + c)),
            blk,
            pl.BlockSpec((tr, tc), lambda i, c: (i, zoff["v"] // tc + c)),
            blk, vec, vec, vec,
            gate(0), gate(1), gate(2), blk, blk,
        ],
        out_specs=blk,
        compiler_params=_params("parallel", "parallel"),
    )(y, zs, kmod, zs, g, r_k, ln_w, ln_b, p, p, p, o_fox, o_sb)


def _outproj_kernel(m_ref, w_ref, x_ref, o_ref):
    o_ref[...] = x_ref[...] + jnp.dot(m_ref[...], w_ref[...], preferred_element_type=F32)


def _outproj(merged, w, x, tm):
    n, d = merged.shape
    tn = _pick(w.shape[1], (512, 256, 128))
    return pl.pallas_call(
        _outproj_kernel,
        out_shape=jax.ShapeDtypeStruct((n, w.shape[1]), F32),
        grid=(w.shape[1] // tn, n // tm),
        in_specs=[
            pl.BlockSpec((tm, d), lambda j, i: (i, 0)),
            pl.BlockSpec((d, tn), lambda j, i: (0, j)),
            pl.BlockSpec((tm, tn), lambda j, i: (i, j)),
        ],
        out_specs=pl.BlockSpec((tm, tn), lambda j, i: (i, j)),
        compiler_params=_params("parallel", "parallel"),
    )(merged, w, x)


def _top16_rows(s):
    nk = s.shape[0]
    idx = lax.broadcasted_iota(jnp.int32, s.shape, 0).astype(F32)
    work = s
    tops, where = [], []
    for _ in range(PEER_TOPK):
        m = jnp.max(work, axis=0, keepdims=True)
        first = jnp.min(jnp.where(work == m, idx, float(nk)), axis=0, keepdims=True)
        work = jnp.where(idx == first, -jnp.inf, work)
        tops.append(m)
        where.append(first)
    kept = jnp.where(work == s, -jnp.inf, s)
    return jnp.concatenate(tops, axis=0), jnp.concatenate(where, axis=0), kept


def _kth_largest(c, order, kth):
    m = first = None
    for _ in range(kth):
        m = jnp.max(c, axis=0, keepdims=True)
        first = jnp.min(jnp.where(c == m, order, jnp.inf), axis=0, keepdims=True)
        c = jnp.where(order == first, -jnp.inf, c)
    return m, first


STAT_THR, STAT_MAX, STAT_INVZ, STAT_S1, STAT_I, STAT_S2, STAT_J, STAT_TIED = range(8)


def _peer_gate_kernel(ht_ref, wq_ref, sk_ref, s1_ref, s2_ref, st_ref):
    qt = jnp.dot(wq_ref[0], ht_ref[...], preferred_element_type=F32)
    half = qt.shape[0] // 2
    sc = [jnp.dot(sk_ref[0, p], qt[p * half:(p + 1) * half].astype(BF16), preferred_element_type=F32)
          for p in range(2)]
    t1, i1, s1m = _top16_rows(sc[0])
    t2, i2, s2m = _top16_rows(sc[1])
    tn = t1.shape[1]
    pairs = [(a, b) for a in range(PEER_TOPK) for b in range(PEER_TOPK) if (a + 1) * (b + 1) <= PEER_TOPK]
    pad = -len(pairs) % SUBLANES
    cand = jnp.concatenate([t1[a:a + 1] + t2[b:b + 1] for a, b in pairs] + [jnp.full((pad, tn), -jnp.inf, F32)], axis=0)
    flat = jnp.concatenate([jnp.full((1, tn), float(a * PEER_TOPK + b), F32) for a, b in pairs]
                           + [jnp.full((pad, tn), float(PEER_TOPK * PEER_TOPK), F32)], axis=0)
    thr, last = _kth_largest(cand, flat, PEER_TOPK)
    kept = jnp.logical_or(cand > thr, jnp.logical_and(cand == thr, flat <= last))
    top = t1[0:1] + t2[0:1]
    zsum = jnp.sum(jnp.where(kept, jnp.exp(cand - top), 0.0), axis=0, keepdims=True)
    rank = lax.broadcasted_iota(jnp.int32, t1.shape, 0).astype(F32)
    a_last = jnp.floor(last * (1.0 / PEER_TOPK))
    b_last = last - a_last * PEER_TOPK
    pick = lambda table, r: jnp.sum(jnp.where(rank == r, table, 0.0), axis=0, keepdims=True)
    grid = (t1[:, None, :] + t2[None, :, :]).reshape(PEER_TOPK * PEER_TOPK, tn)
    tied = (jnp.sum(jnp.where(grid == thr, 1.0, 0.0), axis=0, keepdims=True)
            - jnp.sum(jnp.where(jnp.logical_and(cand == thr, kept), 1.0, 0.0), axis=0, keepdims=True))
    s1_ref[0] = s1m
    s2_ref[0] = s2m
    st_ref[0] = jnp.concatenate([thr, top, 1.0 / zsum, pick(t1, a_last), pick(i1, a_last), pick(t2, b_last),
                                 pick(i2, b_last), tied], axis=0)


def _peer_gate(ht, wq_t, subkeys, tn):
    d, n = ht.shape
    hp, _, nk, half = subkeys.shape
    arr = jax.ShapeDtypeStruct((hp, nk, n), F32)
    blk = pl.BlockSpec((1, nk, tn), lambda i, hh: (hh, 0, i))
    return pl.pallas_call(
        _peer_gate_kernel,
        out_shape=[arr, arr, jax.ShapeDtypeStruct((hp, SUBLANES, n), F32)],
        grid=(n // tn, hp),
        in_specs=[
            pl.BlockSpec((d, tn), lambda i, hh: (0, i)),
            pl.BlockSpec((1, 2 * half, d), lambda i, hh: (hh, 0, 0)),
            pl.BlockSpec((1, 2, nk, half), lambda i, hh: (hh, 0, 0, 0)),
        ],
        out_specs=[blk, blk, pl.BlockSpec((1, SUBLANES, tn), lambda i, hh: (hh, 0, i))],
        compiler_params=_params("parallel", "arbitrary"),
    )(ht, wq_t, subkeys)


def _gelu_exact(x):
    return 0.5 * x * (1.0 + lax.erf(x * (2.0 ** -0.5)))


def _peer_dense_kernel(tie_ref, ht_ref, u_ref, vt_ref, s1_ref, s2_ref, st_ref, o_ref, coef_ref,
                       *, rows_per_step, lane_tile):
    e = pl.program_id(1)
    hp, nk, tn = s2_ref.shape

    @pl.when(e == 0)
    def _():
        o_ref[...] = jnp.zeros_like(o_ref)

    per_group = SUBLANES // rows_per_step
    base = pl.multiple_of((e // per_group) * SUBLANES, SUBLANES)
    sub = e % per_group

    def fill_coef(break_ties):
        for lt in range(tn // lane_tile):
            ls = slice(lt * lane_tile, (lt + 1) * lane_tile)
            act = _gelu_exact(jnp.dot(u_ref[...], ht_ref[:, ls], preferred_element_type=F32))
            for ii in range(rows_per_step):
                gate = jnp.zeros((nk, lane_tile), F32)
                for hh in range(hp):
                    st = st_ref[hh, :, ls]
                    row = lambda r: st[r:r + 1]
                    grp = s1_ref[hh, pl.ds(base, SUBLANES), ls]
                    s1 = grp[ii:ii + 1]
                    for o in range(1, per_group):
                        s1 = jnp.where(sub == o, grp[o * rows_per_step + ii:o * rows_per_step + ii + 1], s1)
                    s2 = s2_ref[hh, :, ls]
                    val = s1 + s2
                    if break_ties:
                        i = (e * rows_per_step + ii).astype(F32)
                        j = lax.broadcasted_iota(jnp.int32, s2.shape, 0).astype(F32)
                        same_i = i == row(STAT_I)
                        earlier = jnp.logical_or(s1 > row(STAT_S1), jnp.logical_and(s1 == row(STAT_S1), i < row(STAT_I)))
                        within = jnp.logical_or(s2 > row(STAT_S2), jnp.logical_and(s2 == row(STAT_S2), j <= row(STAT_J)))
                        tie_ok = jnp.logical_or(earlier, jnp.logical_and(same_i, within))
                        keep = jnp.logical_or(val > row(STAT_THR), jnp.logical_and(val == row(STAT_THR), tie_ok))
                    else:
                        keep = val >= row(STAT_THR)
                    weight = jnp.exp(val - row(STAT_MAX)) * row(STAT_INVZ)
                    gate = gate + jnp.where(keep, weight, 0.0)
                coef_ref[ii * nk:(ii + 1) * nk, ls] = (gate * act[ii * nk:(ii + 1) * nk]).astype(BF16)

    tied = tie_ref[pl.program_id(0)] != 0

    @pl.when(jnp.logical_not(tied))
    def _():
        fill_coef(False)
        o_ref[...] += jnp.dot(vt_ref[0], coef_ref[...], preferred_element_type=F32)

    @pl.when(tied)
    def _():
        fill_coef(True)
        o_ref[...] += jnp.dot(vt_ref[0], coef_ref[...], preferred_element_type=F32)


def _peer_dense(ht, u, vt, s1, s2, stats, tn, n_valid):
    d, n = ht.shape
    hp, nk, _ = s1.shape
    ne = u.shape[0]
    te = vt.shape[2]
    once = dict(pipeline_mode=pl.Buffered(1))
    full = pl.BlockSpec((hp, nk, tn), lambda i, e, tie: (0, 0, i), **once)
    tied = jnp.where(jnp.arange(n) < n_valid, stats[:, STAT_TIED, :], 0.0)
    tie_flags = (jnp.max(tied.reshape(hp, n // tn, tn), axis=(0, 2)) > 0).astype(jnp.int32)
    return pl.pallas_call(
        functools.partial(_peer_dense_kernel, rows_per_step=te // nk, lane_tile=_pick(tn, (256, 128))),
        out_shape=jax.ShapeDtypeStruct((d, n), F32),
        grid_spec=pltpu.PrefetchScalarGridSpec(
            num_scalar_prefetch=1,
            grid=(n // tn, ne // te),
            in_specs=[
                pl.BlockSpec((d, tn), lambda i, e, tie: (0, i), **once),
                pl.BlockSpec((te, d), lambda i, e, tie: (e, 0)),
                pl.BlockSpec((1, d, te), lambda i, e, tie: (e, 0, 0)),
                full, full,
                pl.BlockSpec((hp, SUBLANES, tn), lambda i, e, tie: (0, 0, i), **once),
            ],
            out_specs=pl.BlockSpec((d, tn), lambda i, e, tie: (0, i)),
            scratch_shapes=[pltpu.VMEM((te, tn), BF16)],
        ),
        compiler_params=pltpu.CompilerParams(dimension_semantics=("parallel", "arbitrary"),
                                             vmem_limit_bytes=PEER_VMEM_LIMIT_BYTES),
    )(tie_flags, ht, u, vt, s1, s2, stats)


def _residual_t_kernel(x_ref, yt_ref, o_ref):
    o_ref[...] = x_ref[...] + yt_ref[...].T


def _add_transposed(x, yt, tm):
    n, d = x.shape
    return pl.pallas_call(
        _residual_t_kernel,
        out_shape=jax.ShapeDtypeStruct((n, d), F32),
        grid=(n // tm,),
        in_specs=[pl.BlockSpec((tm, d), lambda i: (i, 0)), pl.BlockSpec((d, tm), lambda i: (0, i))],
        out_specs=pl.BlockSpec((tm, d), lambda i: (i, 0)),
        compiler_params=_params("parallel"),
    )(x, yt)


def kernel(x_prompt, x_sample, cache_fox_k, cache_fox_v, cache_fox_logf, cache_sb_k, cache_sb_v, state_rwkv, state_rwkv_shift, page_table, norm_mix, w_in, b_forget, fox_q_norm, fox_k_norm, sb_q_norm, sb_k_norm, rwkv_mu, rwkv_w0, rwkv_w2, rwkv_a0, rwkv_a2, rwkv_g2, rwkv_k_k, rwkv_k_a, rwkv_r_k, rwkv_ln_w, rwkv_ln_b, w_out, norm_ffn, peer_wq, peer_subkeys, peer_u, peer_v):
    b, t, d = x_prompt.shape
    bs = x_sample.shape[0]
    depth = w_in.shape[0]
    hq = d // HEAD_DIM
    hkv = hq // GQA
    dkv = hkv * HEAD_DIM
    rd, ri, rg = rwkv_w2.shape[1], rwkv_a2.shape[1], rwkv_g2.shape[1]
    n_prompt = b * t
    tr = _pick(t, (256, 128))
    n = _round_up(n_prompt + bs, tr)
    tm = _pick(n, (768, 512, 384, 256, 128))
    tm_in = _pick(n, (1408, 768, 512, 384, 256, 128))
    tn_in = 512 if d % 2048 == 0 else LANES
    lay = _Layout(d, hq, rd, ri, rg, tn_in)
    zoff = {k: lay.off[k] - lay.z_lo for k in ("r", "k", "v", "zw", "za", "zg")}
    rdp, rip, rgp = _round_up(rd, LANES), _round_up(ri, LANES), _round_up(rg, LANES)
    tn_peer = _pick(n, (768, 256, 128))
    te_peer = 4 * peer_subkeys.shape[3]

    def tile_heads(g, count):
        return jnp.tile(g, count)

    def pad_rows(a, rows):
        return jnp.concatenate([a, jnp.zeros((rows - a.shape[0],) + a.shape[1:], a.dtype)], axis=0)

    def with_tail(full, sample_rows):
        return full.at[n_prompt:].set(pad_rows(sample_rows, n - n_prompt))

    def one_row(a):
        return a.reshape(bs, 1, d)

    def zr_original_order(rows):
        return jnp.concatenate([rows[..., zoff[k]:zoff[k] + w] for k, w in
                                (("r", d), ("k", d), ("v", d), ("zw", rd), ("za", ri), ("zg", rg))], axis=-1)

    def to_z_layout(a):
        parts, o, src = [], 0, 0
        for k, w in (("r", d), ("k", d), ("v", d), ("zw", rd), ("za", ri), ("zg", rg)):
            if zoff[k] > o:
                parts.append(jnp.zeros(a.shape[:-1] + (zoff[k] - o,), a.dtype))
            parts.append(a[..., src:src + w])
            o, src = zoff[k] + w, src + w
        if lay.z_width > o:
            parts.append(jnp.zeros(a.shape[:-1] + (lay.z_width - o,), a.dtype))
        return jnp.concatenate(parts, axis=-1)

    x = pad_rows(jnp.concatenate([x_prompt.reshape(n_prompt, d), x_sample.reshape(bs, d)], axis=0), n)
    n_pool, page = cache_fox_logf.shape[1], cache_fox_logf.shape[2]
    lf_cache = jnp.swapaxes(cache_fox_logf, 2, 3).reshape(depth * n_pool, hq, page)
    lf_cache = _pad_heads(lf_cache, hkv).reshape(depth, n_pool, hkv * DEC_ROWS, page)

    prompt_states, sample_states = [], []
    for l in range(depth):
        h = _rmsnorm(x, norm_mix[l], tm)
        w_l = lay.scatter_cols(w_in[l], lay.ncol).astype(BF16)
        colp = jnp.zeros((lay.n_in,), F32)
        for name, g, cnt in (("fq", fox_q_norm[l], hq), ("fk", fox_k_norm[l], hkv), ("sq", sb_q_norm[l], hq),
                             ("sk", sb_k_norm[l], hkv)):
            s0, w0 = lay.src[name]
            colp = colp.at[s0:s0 + w0].set(tile_heads(g, cnt))
        s0, w0 = lay.src["ff"]
        colp = colp.at[s0:s0 + w0].set(b_forget[l])
        colp = lay.scatter_cols(colp.reshape(1, -1), lay.ncol)
        p = _inproj(h, w_l, colp, jnp.asarray(lay.modes), tm_in, tn_in)

        def cols(name, width, rows=slice(None)):
            return p[rows, lay.off[name]:lay.off[name] + width]

        c = _cumsum_time(p, b, t, lay.off["ff"] // LANES)[:, :hq].reshape(b, t, hq)
        tq = _pick(t, FOX_Q_TILES)
        c_t = jnp.swapaxes(c, 1, 2)
        cq = c_t.reshape(b, hq, t, 1)
        ck = c_t.reshape(b, hq, t // tq, 1, tq)
        o_fox_p = _prompt_attention(p, lay, b, t, d, cq, ck)
        o_sb_p = _prompt_attention(p, lay, b, t, d)

        srow = slice(n_prompt, n_prompt + bs)
        scale = HEAD_DIM ** -0.5
        fq_s = _pad_heads((cols("fq", d, srow) * scale).reshape(bs, hq, HEAD_DIM), hkv)
        sq_s = _pad_heads((cols("sq", d, srow) * scale).reshape(bs, hq, HEAD_DIM), hkv)
        fk_s = cols("fk", dkv, srow).reshape(bs, hkv, HEAD_DIM)
        fv_s = cols("fv", dkv, srow).reshape(bs, hkv, HEAD_DIM)
        lf_s = cols("ff", hq, srow)
        o_fox_s = _decode_attention(l, page_table, fq_s, cache_fox_k, cache_fox_v, lf_cache, fk_s, fv_s,
                                    _pad_heads(lf_s.reshape(bs, hq, 1), hkv))
        o_sb_s = _decode_attention(l, page_table, sq_s, cache_sb_k, cache_sb_v)
        o_fox = with_tail(o_fox_p, _unpad_heads(o_fox_s, hkv))
        o_sb = with_tail(o_sb_p, _unpad_heads(o_sb_s, hkv))

        state_pad = pad_rows(to_z_layout(state_rwkv_shift[l]), tr)
        zs = _token_shift(p, lay, state_pad, to_z_layout(rwkv_mu[l]).reshape(1, -1), t, n_prompt, tr)
        pad_k = lambda w2, rp: pad_rows(w2, rp).astype(BF16)
        vec = lambda a: a.reshape(1, d)
        dec, cl, kmod, na, nb, gg = _rwkv_prep(zs, zoff, pad_k(rwkv_w2[l], rdp), pad_k(rwkv_a2[l], rip),
                                               pad_k(rwkv_g2[l], rgp), vec(rwkv_w0[l]), vec(rwkv_a0[l]),
                                               vec(rwkv_k_k[l]), vec(rwkv_k_a[l]), d, tr)
        y_p, s_p = _rwkv_scan_prompt(zs, zoff["r"] // LANES, zoff["v"] // LANES, cl, kmod, na, nb, b, t, d)
        one = lambda a: a[srow].reshape(bs, 1, d)
        zs_s = zs[srow]
        y_s, s_s = _rwkv_scan_sample(one_row(zs_s[:, zoff["r"]:zoff["r"] + d]), one(dec), one(kmod),
                                     one_row(zs_s[:, zoff["v"]:zoff["v"] + d]), one(na), one(nb), state_rwkv[l])
        y = with_tail(y_p, y_s.reshape(bs, d))
        merged = _rwkv_post_merge(y, zs, zoff, kmod, gg, vec(rwkv_r_k[l].reshape(-1)), vec(rwkv_ln_w[l]),
                                  vec(rwkv_ln_b[l]), p, lay, o_fox, o_sb, d, tr)
        x = _outproj(merged, w_out[l].astype(BF16), x, tm)

        h2t = _rmsnorm(x, norm_ffn[l], tm, transposed=True)
        wq_t = peer_wq[l].T.reshape(-1, peer_wq.shape[-1] // peer_subkeys.shape[1], d).astype(BF16)
        s1, s2, stats = _peer_gate(h2t, wq_t, peer_subkeys[l].astype(BF16), tn_peer)
        vt = jnp.swapaxes(peer_v[l].reshape(-1, te_peer, d), 1, 2).astype(BF16)
        yt = _peer_dense(h2t, peer_u[l].astype(BF16), vt, s1, s2, stats, tn_peer, n_prompt + bs)
        x = _add_transposed(x, yt, tr)

        last = jnp.arange(b) * t + (t - 1)
        zr_window = lambda rows: zr_original_order(p[rows][:, lay.z_lo:lay.z_lo + lay.z_width])
        prompt_states.append((
            cols("fk", dkv, slice(0, n_prompt)).reshape(b, t, hkv, HEAD_DIM),
            cols("fv", dkv, slice(0, n_prompt)).reshape(b, t, hkv, HEAD_DIM),
            cols("ff", hq, slice(0, n_prompt)).reshape(b, t, hq),
            cols("sk", dkv, slice(0, n_prompt)).reshape(b, t, hkv, HEAD_DIM),
            cols("sv", dkv, slice(0, n_prompt)).reshape(b, t, hkv, HEAD_DIM),
            s_p, zr_window(last)))
        sample_states.append((
            fk_s.reshape(bs, 1, hkv, HEAD_DIM), fv_s.reshape(bs, 1, hkv, HEAD_DIM), lf_s.reshape(bs, 1, hq),
            cols("sk", dkv, srow).reshape(bs, 1, hkv, HEAD_DIM), cols("sv", dkv, srow).reshape(bs, 1, hkv, HEAD_DIM),
            s_s, zr_window(srow)))

    stack = lambda states, i: jnp.stack([s[i] for s in states])
    y_prompt = x[:n_prompt].reshape(b, t, d)
    y_sample = x[n_prompt:n_prompt + bs].reshape(bs, 1, d)
    return ((y_prompt, y_sample) + tuple(stack(prompt_states, i) for i in range(7))
            + tuple(stack(sample_states, i) for i in range(7)))
```

```python
import functools

import numpy as np
import jax
import jax.numpy as jnp
from jax import lax
from jax.experimental import pallas as pl
from jax.experimental.pallas import tpu as pltpu

F32 = jnp.float32
BF16 = jnp.bfloat16
HIGHEST = lax.Precision.HIGHEST

NORM_EPS = 1e-6
RWKV_GN_EPS = 64e-5
LANES = 128
SUBLANES = 8
HEAD_DIM = 128
RWKV_HEAD = 64
CHUNK = 64
GQA = 4
FOX_Q_TILES = (512, 256, 128)
SB_Q_TILES = (256, 128)
PEER_TOPK = 16
VMEM_LIMIT_BYTES = 56 * 1024 * 1024
PEER_VMEM_LIMIT_BYTES = 60 * 1024 * 1024

MODE_ID, MODE_RMS, MODE_SIGMOID, MODE_LOGSIG = 0, 1, 2, 3


def _params(*sem):
    return pltpu.CompilerParams(dimension_semantics=sem, vmem_limit_bytes=VMEM_LIMIT_BYTES)


def _round_up(n, m):
    return (n + m - 1) // m * m


def _pick(n, cands):
    for c in cands:
        if c <= n and n % c == 0:
            return c
    raise ValueError(f"no tile for {n} in {cands}")


def _sigmoid(x):
    return 1.0 / (1.0 + jnp.exp(-x))


def _dot_by_01(x, m01, m01_first=False, pieces=3):
    mm = (lambda piece: jnp.dot(m01, piece, preferred_element_type=F32)) if m01_first else (
        lambda piece: jnp.dot(piece, m01, preferred_element_type=F32))
    out, rest = None, x
    for _ in range(pieces):
        piece = rest.astype(BF16)
        rest = rest - piece.astype(F32)
        out = mm(piece) if out is None else out + mm(piece)
    return out


def _log_sigmoid(x):
    return jnp.minimum(x, 0.0) - jnp.log1p(jnp.exp(-jnp.abs(x)))


def _log_sigmoid_bulk(x):
    return jnp.minimum(x, 0.0) - jnp.log(1.0 + jnp.exp(-jnp.abs(x)))


def _rmsnorm_kernel(x_ref, g_ref, o_ref, *, transposed):
    x = x_ref[...]
    ms = jnp.mean(x * x, axis=-1, keepdims=True)
    y = x * lax.rsqrt(ms + NORM_EPS) * g_ref[...]
    o_ref[...] = (y.T if transposed else y).astype(o_ref.dtype)


def _rmsnorm(x, g, tm, transposed=False):
    n, d = x.shape
    return pl.pallas_call(
        functools.partial(_rmsnorm_kernel, transposed=transposed),
        out_shape=jax.ShapeDtypeStruct((d, n) if transposed else (n, d), BF16),
        grid=(n // tm,),
        in_specs=[pl.BlockSpec((tm, d), lambda i: (i, 0)), pl.BlockSpec((1, d), lambda i: (0, 0))],
        out_specs=pl.BlockSpec((d, tm), lambda i: (0, i)) if transposed else pl.BlockSpec((tm, d), lambda i: (i, 0)),
        compiler_params=_params("parallel"),
    )(x, g.reshape(1, d))


def _inproj_kernel(modes_ref, h_ref, w_ref, cp_ref, o_ref):
    mode = modes_ref[pl.program_id(1)]
    project = lambda: jnp.dot(h_ref[...], w_ref[...], preferred_element_type=F32)

    @pl.when(mode == MODE_ID)
    def _():
        o_ref[...] = project()

    @pl.when(mode == MODE_RMS)
    def _():
        acc = project()
        for c in range(acc.shape[1] // HEAD_DIM):
            sl = slice(c * HEAD_DIM, (c + 1) * HEAD_DIM)
            xs = acc[:, sl]
            ms = jnp.mean(xs * xs, axis=-1, keepdims=True)
            o_ref[:, sl] = xs * lax.rsqrt(ms + NORM_EPS) * cp_ref[:, sl]

    @pl.when(mode == MODE_SIGMOID)
    def _():
        o_ref[...] = _sigmoid(project())

    @pl.when(mode == MODE_LOGSIG)
    def _():
        o_ref[...] = _log_sigmoid(project() + cp_ref[...])


def _inproj(h, w, colp, modes, tm, tn):
    n, d = h.shape
    ncol = w.shape[1]
    return pl.pallas_call(
        _inproj_kernel,
        out_shape=jax.ShapeDtypeStruct((n, ncol), F32),
        grid_spec=pltpu.PrefetchScalarGridSpec(
            num_scalar_prefetch=1,
            grid=(n // tm, ncol // tn),
            in_specs=[
                pl.BlockSpec((tm, d), lambda i, j, m: (i, 0)),
                pl.BlockSpec((d, tn), lambda i, j, m: (0, j)),
                pl.BlockSpec((1, tn), lambda i, j, m: (0, j)),
            ],
            out_specs=pl.BlockSpec((tm, tn), lambda i, j, m: (i, j)),
        ),
        compiler_params=_params("parallel", "parallel"),
    )(modes, h, w, colp)


class _Layout:
    def __init__(self, d, hq, rd, ri, rg, tn):
        dkv = d // GQA
        zw = 3 * d + rd + ri + rg
        src = {}
        o = 0
        for name, width in (("fq", d), ("fk", dkv), ("fv", dkv), ("ff", hq), ("sq", d), ("sk", dkv), ("sv", dkv)):
            src[name] = (o, width)
            o += width
        for name, width in (("r", d), ("k", d), ("v", d), ("zw", rd), ("za", ri), ("zg", rg)):
            src[name] = (o, width)
            o += width
        src["gates"] = (o, 3 * d)
        self.n_in = o + 3 * d
        self.zr_width = zw
        order = (("fq", MODE_RMS), ("sq", MODE_RMS), ("fk", MODE_RMS), ("fv", MODE_ID), ("sk", MODE_RMS),
                 ("sv", MODE_ID), ("r", MODE_ID), ("k", MODE_ID), ("v", MODE_ID), ("zw", MODE_ID),
                 ("za", MODE_ID), ("zg", MODE_ID), ("gates", MODE_SIGMOID), ("ff", MODE_LOGSIG))
        self.src = src
        self.off = {}
        self.pieces = []
        seg_modes = []
        o, prev = 0, None
        for name, mode in order:
            o = _round_up(o, tn if mode != prev else _round_up(min(src[name][1], 4 * LANES), LANES))
            self.off[name] = o
            seg_modes.append((o, mode))
            self.pieces.append((o, src[name][0], src[name][1]))
            o += src[name][1]
            prev = mode
        self.ncol = _round_up(o, tn)
        self.z_lo = self.off["r"]
        self.z_width = _round_up(self.off["zg"] + rg, LANES) - self.z_lo
        modes = np.zeros((self.ncol // tn,), np.int32)
        for t in range(self.ncol // tn):
            for so, m in seg_modes:
                if so <= t * tn:
                    modes[t] = m
        self.modes = modes

    def scatter_cols(self, a, width):
        parts, o = [], 0
        for dst, s, w in self.pieces:
            if dst > o:
                parts.append(jnp.zeros(a.shape[:-1] + (dst - o,), a.dtype))
            parts.append(a[..., s:s + w])
            o = dst + w
        if width > o:
            parts.append(jnp.zeros(a.shape[:-1] + (width - o,), a.dtype))
        return jnp.concatenate(parts, axis=-1)


def _cumsum_kernel(x_ref, o_ref, *, cb):
    t = x_ref.shape[0]
    row = lax.broadcasted_iota(jnp.int32, (cb, cb), 0)
    col = lax.broadcasted_iota(jnp.int32, (cb, cb), 1)
    tri = (row >= col).astype(F32)
    carry = jnp.zeros((1, x_ref.shape[1]), F32)
    for i in range(t // cb):
        c = jnp.dot(tri, x_ref[i * cb:(i + 1) * cb, :], precision=HIGHEST, preferred_element_type=F32) + carry
        o_ref[i * cb:(i + 1) * cb, :] = c
        carry = c[cb - 1:cb, :]


def _cumsum_time(p, b, t, col_block):
    cb = _pick(t, (256, 128, 64, 32, 16, 8))
    return pl.pallas_call(
        functools.partial(_cumsum_kernel, cb=cb),
        out_shape=jax.ShapeDtypeStruct((b * t, LANES), F32),
        grid=(b,),
        in_specs=[pl.BlockSpec((t, LANES), lambda i: (i, col_block))],
        out_specs=pl.BlockSpec((t, LANES), lambda i: (i, 0)),
        compiler_params=_params("parallel"),
    )(p)


def _stack_heads(q_ref, scale):
    tq = q_ref.shape[0]
    q = q_ref[...] * scale
    return jnp.concatenate([q[:, h * HEAD_DIM:(h + 1) * HEAD_DIM] for h in range(GQA)], axis=0).astype(BF16)


def _unstack_heads(o, o_ref):
    tq = o_ref.shape[0]
    for h in range(GQA):
        o_ref[:, h * HEAD_DIM:(h + 1) * HEAD_DIM] = o[h * tq:(h + 1) * tq, :]


def _fox_prompt_kernel(q_ref, k_ref, v_ref, cq_ref, ck_ref, o_ref, *, tq):
    qi = pl.program_id(2)
    rows = GQA * tq
    qs = _stack_heads(q_ref, HEAD_DIM ** -0.5)
    cq = cq_ref[0].reshape(rows, 1)

    def scores(ki):
        k = k_ref[pl.ds(pl.multiple_of(ki * tq, tq), tq), :].astype(BF16)
        s = lax.dot_general(qs, k, (((1,), (1,)), ((), ())), preferred_element_type=F32)
        ck = ck_ref[0, :, ki]
        return ((s + cq).reshape(GQA, tq, tq) - ck).reshape(rows, tq)

    def update(ki, s, carry):
        m, l, acc = carry
        m_new = jnp.maximum(m, jnp.max(s, axis=-1, keepdims=True))
        alpha = jnp.exp(m - m_new)
        p = jnp.exp(s - m_new)
        v = v_ref[pl.ds(pl.multiple_of(ki * tq, tq), tq), :].astype(BF16)
        acc = acc * alpha + jnp.dot(p.astype(BF16), v, preferred_element_type=F32)
        return m_new, l * alpha + jnp.sum(p, axis=-1, keepdims=True), acc

    init = (jnp.full((rows, 1), -jnp.inf, F32), jnp.zeros((rows, 1), F32), jnp.zeros((rows, HEAD_DIM), F32))
    carry = lax.fori_loop(0, qi, lambda ki, c: update(ki, scores(ki), c), init)
    r = lax.broadcasted_iota(jnp.int32, (GQA, tq, tq), 1)
    c = lax.broadcasted_iota(jnp.int32, (GQA, tq, tq), 2)
    s = jnp.where((c <= r).reshape(rows, tq), scores(qi), -jnp.inf)
    m, l, acc = update(qi, s, carry)
    _unstack_heads(acc / l, o_ref)


def _sb_prompt_kernel(q_ref, k_ref, v_ref, o_ref, *, tq):
    qi = pl.program_id(2)
    rows = GQA * tq
    qs = _stack_heads(q_ref, HEAD_DIM ** -0.5)
    jr = lax.broadcasted_iota(jnp.int32, (tq, tq), 0)
    sc = lax.broadcasted_iota(jnp.int32, (tq, tq), 1)
    later = (jr > sc).astype(BF16)

    def block(ki, carry, mask):
        run, acc = carry
        k = k_ref[pl.ds(pl.multiple_of(ki * tq, tq), tq), :].astype(BF16)
        v = v_ref[pl.ds(pl.multiple_of(ki * tq, tq), tq), :].astype(BF16)
        z = lax.dot_general(qs, k, (((1,), (1,)), ((), ())), preferred_element_type=F32)
        log_keep = _log_sigmoid_bulk(-z)
        if mask is not None:
            log_keep = jnp.where(mask, log_keep, 0.0)
        after = _dot_by_01(log_keep, later, pieces=2) + run
        w = jnp.exp(z + log_keep + after)
        if mask is not None:
            w = jnp.where(mask, w, 0.0)
        acc = acc + jnp.dot(w.astype(BF16), v, preferred_element_type=F32)
        return run + jnp.sum(log_keep, axis=-1, keepdims=True), acc

    r3 = lax.broadcasted_iota(jnp.int32, (GQA, tq, tq), 1)
    c3 = lax.broadcasted_iota(jnp.int32, (GQA, tq, tq), 2)
    carry = block(qi, (jnp.zeros((rows, 1), F32), jnp.zeros((rows, HEAD_DIM), F32)), (c3 < r3).reshape(rows, tq))
    run, acc = lax.fori_loop(0, qi, lambda it, c: block(qi - 1 - it, c, None), carry)
    _unstack_heads(acc, o_ref)


def _prompt_attention(p, lay, b, t, d, cq=None, ck=None):
    hkv = d // (GQA * HEAD_DIM)
    fox = cq is not None
    tq = ck.shape[-1] if fox else _pick(t, SB_Q_TILES)
    nq = t // tq
    gw = GQA * HEAD_DIM
    qn, kn, vn = ("fq", "fk", "fv") if fox else ("sq", "sk", "sv")
    qb, kb, vb = lay.off[qn] // gw, lay.off[kn] // HEAD_DIM, lay.off[vn] // HEAD_DIM
    in_specs = [
        pl.BlockSpec((tq, gw), lambda bi, g, qi: (bi * nq + qi, qb + g)),
        pl.BlockSpec((t, HEAD_DIM), lambda bi, g, qi: (bi, kb + g)),
        pl.BlockSpec((t, HEAD_DIM), lambda bi, g, qi: (bi, vb + g)),
    ]
    args = [p, p, p]
    if fox:
        in_specs += [
            pl.BlockSpec((1, GQA, tq, 1), lambda bi, g, qi: (bi, g, qi, 0)),
            pl.BlockSpec((1, GQA, nq, 1, tq), lambda bi, g, qi: (bi, g, 0, 0, 0)),
        ]
        args += [cq, ck]
        body = functools.partial(_fox_prompt_kernel, tq=tq)
    else:
        body = functools.partial(_sb_prompt_kernel, tq=tq)
    return pl.pallas_call(
        body,
        out_shape=jax.ShapeDtypeStruct((p.shape[0], d), F32),
        grid=(b, hkv, nq),
        in_specs=in_specs,
        out_specs=pl.BlockSpec((tq, gw), lambda bi, g, qi: (bi * nq + qi, g)),
        compiler_params=_params("parallel", "parallel", "arbitrary"),
    )(*args)


DEC_ROWS = 2 * SUBLANES


def _decode_kernel(pt_ref, q_ref, *rest, fox, hkv, pp):
    kc_refs, vc_refs, rest = rest[:pp], rest[pp:2 * pp], rest[2 * pp:]
    if fox:
        lf_refs = rest[:pp]
        knew_ref, vnew_ref, lfnew_ref, o_ref, m_ref, l_ref, run_ref, acc_ref = rest[pp:]
    else:
        o_ref, run_ref, acc_ref = rest
    pg = pl.program_id(1)
    npg = pl.num_programs(1)
    page = kc_refs[0].shape[2]
    rows = hkv * DEC_ROWS

    @pl.when(pg == 0)
    def _():
        if fox:
            for g in range(hkv):
                sl = slice(g * DEC_ROWS, (g + 1) * DEC_ROWS)
                qg = q_ref[0, sl, :].astype(BF16).astype(F32)
                kg = knew_ref[0, g:g + 1, :].astype(BF16).astype(F32)
                m_ref[sl, :] = jnp.sum(qg * kg, axis=-1, keepdims=True)
                acc_ref[sl, :] = jnp.broadcast_to(vnew_ref[0, g:g + 1, :].astype(BF16).astype(F32), (DEC_ROWS, HEAD_DIM))
            l_ref[...] = jnp.ones_like(l_ref)
            run_ref[...] = lfnew_ref[0]
        else:
            run_ref[...] = jnp.zeros_like(run_ref)
            acc_ref[...] = jnp.zeros_like(acc_ref)

    jr = lax.broadcasted_iota(jnp.int32, (page, page), 0)
    sc = lax.broadcasted_iota(jnp.int32, (page, page), 1)
    later = (jr > sc).astype(BF16)
    k_pages = [pltpu.einshape("khd->hkd", ref[0, 0]) for ref in kc_refs]
    v_pages = [pltpu.einshape("khd->hkd", ref[0, 0]) for ref in vc_refs]
    k_heads = [jnp.concatenate([kp[g] for kp in k_pages], axis=0).astype(BF16) for g in range(hkv)]
    v_heads = [jnp.concatenate([vp[g] for vp in v_pages], axis=0).astype(BF16) for g in range(hkv)]
    z = jnp.concatenate(
        [lax.dot_general(q_ref[0, g * DEC_ROWS:(g + 1) * DEC_ROWS, :].astype(BF16), k_heads[g],
                         (((1,), (1,)), ((), ())), preferred_element_type=F32) for g in range(hkv)], axis=0)
    run = run_ref[...]
    in_page = lambda x, j: x[:, j * page:(j + 1) * page]

    def later_sums(per_key):
        out, carry = [], run
        for j in range(pp):
            out.append(_dot_by_01(in_page(per_key, j), later) + carry)
            carry = carry + jnp.sum(in_page(per_key, j), axis=-1, keepdims=True)
        return jnp.concatenate(out, axis=1), carry

    if fox:
        bias, run_ref[...] = later_sums(jnp.concatenate([ref[0, 0] for ref in lf_refs], axis=1))
        s = z + bias
        m = m_ref[...]
        m_new = jnp.maximum(m, jnp.max(s, axis=-1, keepdims=True))
        alpha = jnp.exp(m - m_new)
        w = jnp.exp(s - m_new)
        l_ref[...] = l_ref[...] * alpha + jnp.sum(w, axis=-1, keepdims=True)
        m_ref[...] = m_new
    else:
        log_keep = _log_sigmoid_bulk(-z)
        after, run_ref[...] = later_sums(log_keep)
        w = jnp.exp(z + log_keep + after)
        alpha = None
    wb = w.astype(BF16)
    for g in range(hkv):
        sl = slice(g * DEC_ROWS, (g + 1) * DEC_ROWS)
        pv = jnp.dot(wb[sl, :], v_heads[g], preferred_element_type=F32)
        if fox:
            acc_ref[sl, :] = acc_ref[sl, :] * alpha[sl, :] + pv
        else:
            acc_ref[sl, :] = acc_ref[sl, :] + pv

    @pl.when(pg == npg - 1)
    def _():
        if fox:
            o_ref[0] = acc_ref[...] / l_ref[...]
        else:
            o_ref[0] = acc_ref[...]


def _decode_attention(layer, page_table, q, kcache, vcache, lf_cache=None, knew=None, vnew=None, lfnew=None):
    bs, npg = page_table.shape
    _, _, page, hkv, _ = kcache.shape
    rows = hkv * DEC_ROWS
    fox = lf_cache is not None
    pp = _pick(npg, (8, 4, 2, 1))
    nth_page = lambda j: (lambda b, p, pt: (layer, pt[b, npg - 1 - (p * pp + j)], 0, 0, 0))
    cache_specs = [pl.BlockSpec((1, 1, page, hkv, HEAD_DIM), nth_page(j)) for j in range(pp)]
    in_specs = [pl.BlockSpec((1, rows, HEAD_DIM), lambda b, p, pt: (b, 0, 0))] + cache_specs + cache_specs
    args = [q] + [kcache] * pp + [vcache] * pp
    scratch = [pltpu.VMEM((rows, 1), F32), pltpu.VMEM((rows, HEAD_DIM), F32)]
    if fox:
        in_specs += [pl.BlockSpec((1, 1, rows, page), lambda b, p, pt, j=j: (layer, pt[b, npg - 1 - (p * pp + j)], 0, 0))
                     for j in range(pp)]
        in_specs += [
            pl.BlockSpec((1, hkv, HEAD_DIM), lambda b, p, pt: (b, 0, 0)),
            pl.BlockSpec((1, hkv, HEAD_DIM), lambda b, p, pt: (b, 0, 0)),
            pl.BlockSpec((1, rows, 1), lambda b, p, pt: (b, 0, 0)),
        ]
        args += [lf_cache] * pp + [knew, vnew, lfnew]
        scratch = [pltpu.VMEM((rows, 1), F32), pltpu.VMEM((rows, 1), F32)] + scratch
    return pl.pallas_call(
        functools.partial(_decode_kernel, fox=fox, hkv=hkv, pp=pp),
        out_shape=jax.ShapeDtypeStruct((bs, rows, HEAD_DIM), F32),
        grid_spec=pltpu.PrefetchScalarGridSpec(
            num_scalar_prefetch=1,
            grid=(bs, npg // pp),
            in_specs=in_specs,
            out_specs=pl.BlockSpec((1, rows, HEAD_DIM), lambda b, p, pt: (b, 0, 0)),
            scratch_shapes=scratch,
        ),
        compiler_params=_params("parallel", "arbitrary"),
    )(page_table, *args)


def _pad_heads(a, hkv):
    bs, _, x = a.shape
    a = a.reshape(bs, hkv, GQA, x)
    a = jnp.concatenate([a, jnp.zeros((bs, hkv, DEC_ROWS - GQA, x), a.dtype)], axis=2)
    return a.reshape(bs, hkv * DEC_ROWS, x)


def _unpad_heads(a, hkv):
    bs, _, x = a.shape
    return a.reshape(bs, hkv, DEC_ROWS, x)[:, :, :GQA].reshape(bs, hkv * GQA * x)


def _shift_kernel(z_ref, prev_ref, st_ref, mu_ref, o_ref, *, tiles_per_seq, sample_tile):
    i = pl.program_id(0)
    z = z_ref[...]
    first = (i % tiles_per_seq) == 0
    row0 = jnp.where(first, 0.0, prev_ref[SUBLANES - 1:SUBLANES, :])
    rid = lax.broadcasted_iota(jnp.int32, z.shape, 0)
    zp = jnp.where(rid == 0, row0, pltpu.roll(z, 1, 0))
    zp = jnp.where(i == sample_tile, st_ref[...], zp)
    o_ref[...] = z + (zp - z) * mu_ref[...]


def _token_shift(p, lay, state_pad, mu, t, n_prompt, tr):
    n = p.shape[0]
    zp = lay.z_width
    tc = _pick(zp, tuple(c for c in (1024, 896, 768, 640, 512, 384, 256, 128) if lay.z_lo % c == 0))
    cb = lay.z_lo // tc
    per8 = tr // SUBLANES
    return pl.pallas_call(
        functools.partial(_shift_kernel, tiles_per_seq=t // tr, sample_tile=n_prompt // tr),
        out_shape=jax.ShapeDtypeStruct((n, zp), F32),
        grid=(n // tr, zp // tc),
        in_specs=[
            pl.BlockSpec((tr, tc), lambda i, c: (i, cb + c)),
            pl.BlockSpec((SUBLANES, tc), lambda i, c: (jnp.maximum(i * per8 - 1, 0), cb + c)),
            pl.BlockSpec((tr, tc), lambda i, c: (0, c)),
            pl.BlockSpec((1, tc), lambda i, c: (0, c)),
        ],
        out_specs=pl.BlockSpec((tr, tc), lambda i, c: (i, c)),
        compiler_params=_params("parallel", "parallel"),
    )(p, p, state_pad, mu)


def _head_sum_matrix(width):
    a = lax.broadcasted_iota(jnp.int32, (width, width), 0) // RWKV_HEAD
    b = lax.broadcasted_iota(jnp.int32, (width, width), 1) // RWKV_HEAD
    return (a == b).astype(BF16)


def _rwkv_prep_kernel(k_ref, zw_ref, za_ref, zg_ref, w2_ref, a2_ref, g2_ref, w0_ref, a0_ref, kk_ref, ka_ref,
                      dec_ref, cl_ref, kmod_ref, na_ref, nb_ref, g_ref):
    k = k_ref[...]
    dw = jnp.dot(jnp.tanh(zw_ref[...]).astype(BF16), w2_ref[...], preferred_element_type=F32)
    x = w0_ref[...] + dw
    w_log = _log_sigmoid(x) - 0.5
    log_dec = -jnp.exp(w_log)
    dec_ref[...] = jnp.exp(log_dec)
    tok = lax.broadcasted_iota(jnp.int32, (CHUNK, CHUNK), 0)
    src = lax.broadcasted_iota(jnp.int32, (CHUNK, CHUNK), 1)
    upto = (src <= tok).astype(BF16)
    for c in range(k.shape[0] // CHUNK):
        rows = slice(c * CHUNK, (c + 1) * CHUNK)
        cl_ref[rows, :] = _dot_by_01(log_dec[rows, :], upto, m01_first=True)
    a = _sigmoid(a0_ref[...] + jnp.dot(za_ref[...].astype(BF16), a2_ref[...], preferred_element_type=F32))
    g_ref[...] = jnp.dot(_sigmoid(zg_ref[...]).astype(BF16), g2_ref[...], preferred_element_type=F32)
    kk = k * kk_ref[...]
    hs = _head_sum_matrix(LANES)
    for c in range(k.shape[1] // LANES):
        sl = slice(c * LANES, (c + 1) * LANES)
        kc = kk[:, sl]
        ss = _dot_by_01(kc * kc, hs)
        kn = kc / jnp.maximum(jnp.sqrt(ss), 1e-12)
        na_ref[:, sl] = -kn
        nb_ref[:, sl] = kn * a[:, sl]
    kmod_ref[...] = k * (1.0 + (a - 1.0) * ka_ref[...])


def _rwkv_prep(zs, zoff, w2, a2, g2, w0, a0, k_k, k_a, d, tr):
    n = zs.shape[0]
    tc = _pick(d, (512, 256, 128))
    rdp, rip, rgp = w2.shape[0], a2.shape[0], g2.shape[0]
    row = lambda i, c: (i, c)
    vec = pl.BlockSpec((1, tc), lambda i, c: (0, c))
    out = jax.ShapeDtypeStruct((n, d), F32)
    return pl.pallas_call(
        _rwkv_prep_kernel,
        out_shape=[out] * 6,
        grid=(n // tr, d // tc),
        in_specs=[
            pl.BlockSpec((tr, tc), lambda i, c: (i, zoff["k"] // tc + c)),
            pl.BlockSpec((tr, rdp), lambda i, c: (i, zoff["zw"] // rdp)),
            pl.BlockSpec((tr, rip), lambda i, c: (i, zoff["za"] // rip)),
            pl.BlockSpec((tr, rgp), lambda i, c: (i, zoff["zg"] // rgp)),
            pl.BlockSpec((rdp, tc), lambda i, c: (0, c)),
            pl.BlockSpec((rip, tc), lambda i, c: (0, c)),
            pl.BlockSpec((rgp, tc), lambda i, c: (0, c)),
            vec, vec, vec, vec,
        ],
        out_specs=[pl.BlockSpec((tr, tc), row)] * 6,
        compiler_params=_params("parallel", "parallel"),
    )(zs, zs, zs, zs, w2, a2, g2, w0, a0, k_k, k_a)


def _scan_step(state, r, w, k, v, a, b, lo, eye0, eye1):
    def half_sums(x):
        s0 = jnp.sum(jnp.where(lo, x, 0.0), axis=1, keepdims=True)
        s1 = jnp.sum(jnp.where(lo, 0.0, x), axis=1, keepdims=True)
        return s0, s1

    sa0, sa1 = half_sums(state * a)
    vb = jnp.broadcast_to(v, state.shape)
    v0 = jnp.sum(jnp.where(eye0, vb, 0.0), axis=1, keepdims=True)
    v1 = jnp.sum(jnp.where(eye1, vb, 0.0), axis=1, keepdims=True)
    state = state * w + jnp.where(lo, sa0, sa1) * b + jnp.where(lo, v0, v1) * k
    y0, y1 = half_sums(state * r)
    y = jnp.sum(jnp.where(eye0, y0, jnp.where(eye1, y1, 0.0)), axis=0, keepdims=True)
    return state, y


def _scan_masks():
    shape = (RWKV_HEAD, LANES)
    i = lax.broadcasted_iota(jnp.int32, shape, 0)
    c = lax.broadcasted_iota(jnp.int32, shape, 1)
    return c < RWKV_HEAD, c == i, c == i + RWKV_HEAD


def _load_state(s_ref, p):
    return jnp.concatenate([s_ref[0, 2 * p], s_ref[0, 2 * p + 1]], axis=-1)


def _store_state(s_ref, p, state):
    s_ref[0, 2 * p] = state[:, :RWKV_HEAD]
    s_ref[0, 2 * p + 1] = state[:, RWKV_HEAD:]


def _mm(a, b):
    return jnp.dot(a.astype(BF16), b.astype(BF16), preferred_element_type=F32)


def _mm_nt(a, b):
    return lax.dot_general(a.astype(BF16), b.astype(BF16), (((1,), (1,)), ((), ())), preferred_element_type=F32)


def _mm_tn(a, b):
    return lax.dot_general(a.astype(BF16), b.astype(BF16), (((0,), (0,)), ((), ())), preferred_element_type=F32)


def _scan_chunk(sd, r, cl, k, v, a, b, lo, strict, incl, same_head):
    def blockdiag(x):
        return jnp.concatenate([jnp.where(lo, x, 0.0), jnp.where(lo, 0.0, x)], axis=0)

    pairs = range(len(sd))
    first = lax.broadcasted_iota(jnp.int32, cl[0].shape, 0) == 0
    g = [jnp.exp(cl[p]) for p in pairs]
    g_prev = [jnp.exp(jnp.where(first, 0.0, pltpu.roll(cl[p], 1, 0))) for p in pairs]
    g_inv = [jnp.exp(-cl[p]) for p in pairs]
    bt = [b[p] * g_inv[p] for p in pairs]
    kt = [k[p] * g_inv[p] for p in pairs]
    lhs = [jnp.concatenate([a[p] * g_prev[p], r[p] * g[p]], axis=0) for p in pairs]
    keys = [jnp.concatenate([jnp.where(lo, bt[p], 0.0), jnp.where(lo, 0.0, bt[p]),
                             jnp.where(lo, kt[p], 0.0), jnp.where(lo, 0.0, kt[p])], axis=0) for p in pairs]
    gram = [_mm_nt(lhs[p], keys[p]) for p in pairs]
    from_state = [_mm_nt(lhs[p], sd[p]) for p in pairs]
    nk = [jnp.where(strict, gram[p][:CHUNK, :LANES], 0.0) for p in pairs]
    vd = [blockdiag(v[p]) for p in pairs]
    x = [from_state[p][:CHUNK] + _mm(jnp.where(strict, gram[p][:CHUNK, LANES:], 0.0), vd[p]) for p in pairs]
    steps = CHUNK.bit_length() - 1
    for it in range(steps):
        x = [x[p] + _mm(nk[p], blockdiag(x[p])) for p in pairs]
        if it + 1 < steps:
            nk = [_mm(nk[p], blockdiag(nk[p])) for p in pairs]
    r_bk = [jnp.concatenate([jnp.where(incl, gram[p][CHUNK:, :LANES], 0.0),
                             jnp.where(incl, gram[p][CHUNK:, LANES:], 0.0)], axis=1) for p in pairs]
    y = [from_state[p][CHUNK:] + _mm(r_bk[p], jnp.concatenate([blockdiag(x[p]), vd[p]], axis=0)) for p in pairs]
    g_last = [g[p][CHUNK - 1:CHUNK, :] for p in pairs]
    update = [_mm_tn(jnp.concatenate([x[p], v[p]], axis=0),
                     jnp.concatenate([bt[p] * g_last[p], kt[p] * g_last[p]], axis=0)) for p in pairs]
    return [sd[p] * g_last[p] + jnp.where(same_head, update[p], 0.0) for p in pairs], y


def _scan_prompt_kernel(r_ref, cl_ref, k_ref, v_ref, a_ref, b_ref, y_ref, sout_ref, sd_ref, *, pairs, nchunk):
    tc = pl.program_id(2)

    @pl.when(tc == 0)
    def _():
        sd_ref[...] = jnp.zeros_like(sd_ref)

    lane = lax.broadcasted_iota(jnp.int32, (CHUNK, LANES), 1)
    tok = lax.broadcasted_iota(jnp.int32, (CHUNK, LANES), 0)
    lo = lane < RWKV_HEAD
    src = lane % RWKV_HEAD
    strict, incl = src < tok, src <= tok
    hr = lax.broadcasted_iota(jnp.int32, (LANES, LANES), 0) // RWKV_HEAD
    hc = lax.broadcasted_iota(jnp.int32, (LANES, LANES), 1) // RWKV_HEAD
    same_head = hr == hc

    def chunk(ci, carry):
        rows = pl.ds(pl.multiple_of(ci * CHUNK, CHUNK), CHUNK)
        cols = [slice(p * LANES, (p + 1) * LANES) for p in range(pairs)]
        take = lambda ref: [ref[rows, sl] for sl in cols]
        sd, y = _scan_chunk([sd_ref[p] for p in range(pairs)], take(r_ref), take(cl_ref), take(k_ref), take(v_ref),
                            take(a_ref), take(b_ref), lo, strict, incl, same_head)
        for p in range(pairs):
            sd_ref[p] = sd[p]
            y_ref[rows, cols[p]] = y[p]
        return carry

    lax.fori_loop(0, nchunk, chunk, 0)

    @pl.when(tc == pl.num_programs(2) - 1)
    def _():
        for p in range(pairs):
            sd = sd_ref[p]
            sout_ref[0, 2 * p] = sd[:RWKV_HEAD, :RWKV_HEAD]
            sout_ref[0, 2 * p + 1] = sd[RWKV_HEAD:, RWKV_HEAD:]


def _scan_sample_kernel(r_ref, w_ref, k_ref, v_ref, a_ref, b_ref, s0_ref, y_ref, sout_ref, *, pairs):
    lo, eye0, eye1 = _scan_masks()
    for p in range(pairs):
        sl = slice(p * LANES, (p + 1) * LANES)
        row = lambda ref: ref[0, :, sl]
        state, y = _scan_step(_load_state(s0_ref, p), row(r_ref), row(w_ref), row(k_ref), row(v_ref), row(a_ref),
                              row(b_ref), lo, eye0, eye1)
        _store_state(sout_ref, p, state)
        y_ref[0, :, sl] = y


def _rwkv_scan_prompt(r_src, r_cb, v_cb, cl, kmod, na, nb, b, t, d):
    pairs = _pick(d // LANES, (8, 4, 2, 1))
    wcol = pairs * LANES
    tchunk = _pick(t, (256, 128))
    nt = t // tchunk
    nh = d // RWKV_HEAD
    rowmap = lambda bi, g, tc: (bi * nt + tc, g)
    blk = pl.BlockSpec((tchunk, wcol), rowmap)
    return pl.pallas_call(
        functools.partial(_scan_prompt_kernel, pairs=pairs, nchunk=tchunk // CHUNK),
        out_shape=[jax.ShapeDtypeStruct((cl.shape[0], d), F32), jax.ShapeDtypeStruct((b, nh, RWKV_HEAD, RWKV_HEAD), F32)],
        grid=(b, d // wcol, nt),
        in_specs=[
            pl.BlockSpec((tchunk, wcol), lambda bi, g, tc: (bi * nt + tc, r_cb // pairs + g)),
            blk, blk,
            pl.BlockSpec((tchunk, wcol), lambda bi, g, tc: (bi * nt + tc, v_cb // pairs + g)),
            blk, blk,
        ],
        out_specs=[blk, pl.BlockSpec((1, 2 * pairs, RWKV_HEAD, RWKV_HEAD), lambda bi, g, tc: (bi, g, 0, 0))],
        scratch_shapes=[pltpu.VMEM((pairs, LANES, LANES), F32)],
        compiler_params=_params("parallel", "parallel", "arbitrary"),
    )(r_src, cl, kmod, r_src, na, nb)


def _rwkv_scan_sample(r, dec, kmod, v, na, nb, s0):
    bs, _, d = r.shape
    pairs = _pick(d // LANES, (4, 2, 1))
    wcol = pairs * LANES
    blk = pl.BlockSpec((1, 1, wcol), lambda bi, g: (bi, 0, g))
    sblk = pl.BlockSpec((1, 2 * pairs, RWKV_HEAD, RWKV_HEAD), lambda bi, g: (bi, g, 0, 0))
    return pl.pallas_call(
        functools.partial(_scan_sample_kernel, pairs=pairs),
        out_shape=[jax.ShapeDtypeStruct((bs, 1, d), F32), jax.ShapeDtypeStruct(s0.shape, F32)],
        grid=(bs, d // wcol),
        in_specs=[blk] * 6 + [sblk],
        out_specs=[blk, sblk],
        compiler_params=_params("parallel", "parallel"),
    )(r, dec, kmod, v, na, nb, s0)


def _rwkv_post_kernel(y_ref, r_ref, k_ref, v_ref, g_ref, rk_ref, lw_ref, lb_ref, gf_ref, gr_ref, gs_ref, of_ref, os_ref,
                      o_ref):
    hs = _head_sum_matrix(LANES)
    inv = 1.0 / RWKV_HEAD
    for c in range(y_ref.shape[1] // LANES):
        sl = slice(c * LANES, (c + 1) * LANES)
        y = y_ref[:, sl]
        mu = _dot_by_01(y, hs) * inv
        yc = y - mu
        var = _dot_by_01(yc * yc, hs) * inv
        yn = yc * lax.rsqrt(var + RWKV_GN_EPS) * lw_ref[:, sl] + lb_ref[:, sl]
        v = v_ref[:, sl]
        bonus = _dot_by_01(r_ref[:, sl] * k_ref[:, sl] * rk_ref[:, sl], hs) * v
        o_rwkv = (yn + bonus) * g_ref[:, sl]
        o_ref[:, sl] = (gf_ref[:, sl] * of_ref[:, sl] + gr_ref[:, sl] * o_rwkv + gs_ref[:, sl] * os_ref[:, sl]).astype(BF16)


def _rwkv_post_merge(y, zs, zoff, kmod, g, r_k, ln_w, ln_b, p, lay, o_fox, o_sb, d, tr):
    n = y.shape[0]
    tc = _pick(d, tuple(c for c in (512, 256, 128) if lay.off["gates"] % c == 0))
    gb = lay.off["gates"] // tc
    per = d // tc
    blk = pl.BlockSpec((tr, tc), lambda i, c: (i, c))
    vec = pl.BlockSpec((1, tc), lambda i, c: (0, c))
    gate = lambda which: pl.BlockSpec((tr, tc), lambda i, c: (i, gb + which * per + c))
    return pl.pallas_call(
        _rwkv_post_kernel,
        out_shape=jax.ShapeDtypeStruct((n, d), BF16),
        grid=(n // tr, d // tc),
        in_specs=[
            blk,
            pl.BlockSpec((tr, tc), lambda i, c: (i, zoff["r"] // tc + c)),
            blk,
            pl.BlockSpec((tr, tc), lambda i, c: (i, zoff["v"] // tc + c)),
            blk, vec, vec, vec,
            gate(0), gate(1), gate(2), blk, blk,
        ],
        out_specs=blk,
        compiler_params=_params("parallel", "parallel"),
    )(y, zs, kmod, zs, g, r_k, ln_w, ln_b, p, p, p, o_fox, o_sb)


def _outproj_kernel(m_ref, w_ref, x_ref, o_ref):
    o_ref[...] = x_ref[...] + jnp.dot(m_ref[...], w_ref[...], preferred_element_type=F32)


def _outproj(merged, w, x, tm):
    n, d = merged.shape
    tn = _pick(w.shape[1], (512, 256, 128))
    return pl.pallas_call(
        _outproj_kernel,
        out_shape=jax.ShapeDtypeStruct((n, w.shape[1]), F32),
        grid=(w.shape[1] // tn, n // tm),
        in_specs=[
            pl.BlockSpec((tm, d), lambda j, i: (i, 0)),
            pl.BlockSpec((d, tn), lambda j, i: (0, j)),
            pl.BlockSpec((tm, tn), lambda j, i: (i, j)),
        ],
        out_specs=pl.BlockSpec((tm, tn), lambda j, i: (i, j)),
        compiler_params=_params("parallel", "parallel"),
    )(merged, w, x)


def _top16_rows(s):
    nk = s.shape[0]
    idx = lax.broadcasted_iota(jnp.int32, s.shape, 0).astype(F32)
    work = s
    tops, where = [], []
    for _ in range(PEER_TOPK):
        m = jnp.max(work, axis=0, keepdims=True)
        first = jnp.min(jnp.where(work == m, idx, float(nk)), axis=0, keepdims=True)
        work = jnp.where(idx == first, -jnp.inf, work)
        tops.append(m)
        where.append(first)
    kept = jnp.where(work == s, -jnp.inf, s)
    return jnp.concatenate(tops, axis=0), jnp.concatenate(where, axis=0), kept


def _kth_largest(c, order, kth):
    m = first = None
    for _ in range(kth):
        m = jnp.max(c, axis=0, keepdims=True)
        first = jnp.min(jnp.where(c == m, order, jnp.inf), axis=0, keepdims=True)
        c = jnp.where(order == first, -jnp.inf, c)
    return m, first


STAT_THR, STAT_LOGNORM, STAT_SPARE, STAT_S1, STAT_I, STAT_S2, STAT_J, STAT_TIED = range(8)


def _peer_gate_kernel(ht_ref, wq_ref, sk_ref, s1_ref, s2_ref, st_ref):
    qt = jnp.dot(wq_ref[0], ht_ref[...], preferred_element_type=F32)
    half = qt.shape[0] // 2
    sc = [jnp.dot(sk_ref[0, p], qt[p * half:(p + 1) * half].astype(BF16), preferred_element_type=F32)
          for p in range(2)]
    t1, i1, s1m = _top16_rows(sc[0])
    t2, i2, s2m = _top16_rows(sc[1])
    tn = t1.shape[1]
    pairs = [(a, b) for a in range(PEER_TOPK) for b in range(PEER_TOPK) if (a + 1) * (b + 1) <= PEER_TOPK]
    pad = -len(pairs) % SUBLANES
    cand = jnp.concatenate([t1[a:a + 1] + t2[b:b + 1] for a, b in pairs] + [jnp.full((pad, tn), -jnp.inf, F32)], axis=0)
    flat = jnp.concatenate([jnp.full((1, tn), float(a * PEER_TOPK + b), F32) for a, b in pairs]
                           + [jnp.full((pad, tn), float(PEER_TOPK * PEER_TOPK), F32)], axis=0)
    thr, last = _kth_largest(cand, flat, PEER_TOPK)
    kept = jnp.logical_or(cand > thr, jnp.logical_and(cand == thr, flat <= last))
    top = t1[0:1] + t2[0:1]
    zsum = jnp.sum(jnp.where(kept, jnp.exp(cand - top), 0.0), axis=0, keepdims=True)
    rank = lax.broadcasted_iota(jnp.int32, t1.shape, 0).astype(F32)
    a_last = jnp.floor(last * (1.0 / PEER_TOPK))
    b_last = last - a_last * PEER_TOPK
    pick = lambda table, r: jnp.sum(jnp.where(rank == r, table, 0.0), axis=0, keepdims=True)
    grid = (t1[:, None, :] + t2[None, :, :]).reshape(PEER_TOPK * PEER_TOPK, tn)
    tied = (jnp.sum(jnp.where(grid == thr, 1.0, 0.0), axis=0, keepdims=True)
            - jnp.sum(jnp.where(jnp.logical_and(cand == thr, kept), 1.0, 0.0), axis=0, keepdims=True))
    s1_ref[0] = s1m
    s2_ref[0] = s2m
    st_ref[0] = jnp.concatenate([thr, top + jnp.log(zsum), jnp.zeros_like(thr), pick(t1, a_last), pick(i1, a_last),
                                 pick(t2, b_last), pick(i2, b_last), tied], axis=0)


def _peer_gate(ht, wq_t, subkeys, tn):
    d, n = ht.shape
    hp, _, nk, half = subkeys.shape
    arr = jax.ShapeDtypeStruct((hp, nk, n), F32)
    blk = pl.BlockSpec((1, nk, tn), lambda i, hh: (hh, 0, i))
    return pl.pallas_call(
        _peer_gate_kernel,
        out_shape=[arr, arr, jax.ShapeDtypeStruct((hp, SUBLANES, n), F32)],
        grid=(n // tn, hp),
        in_specs=[
            pl.BlockSpec((d, tn), lambda i, hh: (0, i)),
            pl.BlockSpec((1, 2 * half, d), lambda i, hh: (hh, 0, 0)),
            pl.BlockSpec((1, 2, nk, half), lambda i, hh: (hh, 0, 0, 0)),
        ],
        out_specs=[blk, blk, pl.BlockSpec((1, SUBLANES, tn), lambda i, hh: (hh, 0, i))],
        compiler_params=_params("parallel", "arbitrary"),
    )(ht, wq_t, subkeys)


def _gelu_exact(x):
    return 0.5 * x * (1.0 + lax.erf(x * (2.0 ** -0.5)))


def _peer_dense_kernel(tie_ref, ht_ref, u_ref, vt_ref, s1_ref, s2_ref, st_ref, o_ref, coef_ref,
                       *, rows_per_step, lane_tile):
    e = pl.program_id(1)
    hp, nk, tn = s2_ref.shape

    @pl.when(e == 0)
    def _():
        o_ref[...] = jnp.zeros_like(o_ref)

    per_group = SUBLANES // rows_per_step
    base = pl.multiple_of((e // per_group) * SUBLANES, SUBLANES)
    sub = e % per_group

    def fill_coef(break_ties):
        for lt in range(tn // lane_tile):
            ls = slice(lt * lane_tile, (lt + 1) * lane_tile)
            act = _gelu_exact(jnp.dot(u_ref[...], ht_ref[:, ls], preferred_element_type=F32))
            for ii in range(rows_per_step):
                gate = jnp.zeros((nk, lane_tile), F32)
                for hh in range(hp):
                    st = st_ref[hh, :, ls]
                    row = lambda r: st[r:r + 1]
                    grp = s1_ref[hh, pl.ds(base, SUBLANES), ls]
                    s1 = grp[ii:ii + 1]
                    for o in range(1, per_group):
                        s1 = jnp.where(sub == o, grp[o * rows_per_step + ii:o * rows_per_step + ii + 1], s1)
                    s2 = s2_ref[hh, :, ls]
                    val = s1 + s2
                    if break_ties:
                        i = (e * rows_per_step + ii).astype(F32)
                        j = lax.broadcasted_iota(jnp.int32, s2.shape, 0).astype(F32)
                        same_i = i == row(STAT_I)
                        earlier = jnp.logical_or(s1 > row(STAT_S1), jnp.logical_and(s1 == row(STAT_S1), i < row(STAT_I)))
                        within = jnp.logical_or(s2 > row(STAT_S2), jnp.logical_and(s2 == row(STAT_S2), j <= row(STAT_J)))
                        tie_ok = jnp.logical_or(earlier, jnp.logical_and(same_i, within))
                        keep = jnp.logical_or(val > row(STAT_THR), jnp.logical_and(val == row(STAT_THR), tie_ok))
                    else:
                        keep = val >= row(STAT_THR)
                    weight = jnp.exp(s2 - (row(STAT_LOGNORM) - s1))
                    gate = gate + jnp.where(keep, weight, 0.0)
                coef_ref[ii * nk:(ii + 1) * nk, ls] = (gate * act[ii * nk:(ii + 1) * nk]).astype(BF16)

    tied = tie_ref[pl.program_id(0)] != 0

    @pl.when(jnp.logical_not(tied))
    def _():
        fill_coef(False)
        o_ref[...] += jnp.dot(vt_ref[0], coef_ref[...], preferred_element_type=F32)

    @pl.when(tied)
    def _():
        fill_coef(True)
        o_ref[...] += jnp.dot(vt_ref[0], coef_ref[...], preferred_element_type=F32)


def _peer_dense(ht, u, vt, s1, s2, stats, tn, n_valid):
    d, n = ht.shape
    hp, nk, _ = s1.shape
    ne = u.shape[0]
    te = vt.shape[2]
    once = dict(pipeline_mode=pl.Buffered(1))
    full = pl.BlockSpec((hp, nk, tn), lambda i, e, tie: (0, 0, i), **once)
    tied = jnp.where(jnp.arange(n) < n_valid, stats[:, STAT_TIED, :], 0.0)
    tie_flags = (jnp.max(tied.reshape(hp, n // tn, tn), axis=(0, 2)) > 0).astype(jnp.int32)
    return pl.pallas_call(
        functools.partial(_peer_dense_kernel, rows_per_step=te // nk, lane_tile=_pick(tn, (256, 128))),
        out_shape=jax.ShapeDtypeStruct((d, n), F32),
        grid_spec=pltpu.PrefetchScalarGridSpec(
            num_scalar_prefetch=1,
            grid=(n // tn, ne // te),
            in_specs=[
                pl.BlockSpec((d, tn), lambda i, e, tie: (0, i), **once),
                pl.BlockSpec((te, d), lambda i, e, tie: (e, 0)),
                pl.BlockSpec((1, d, te), lambda i, e, tie: (e, 0, 0)),
                full, full,
                pl.BlockSpec((hp, SUBLANES, tn), lambda i, e, tie: (0, 0, i), **once),
            ],
            out_specs=pl.BlockSpec((d, tn), lambda i, e, tie: (0, i)),
            scratch_shapes=[pltpu.VMEM((te, tn), BF16)],
        ),
        compiler_params=pltpu.CompilerParams(dimension_semantics=("parallel", "arbitrary"),
                                             vmem_limit_bytes=PEER_VMEM_LIMIT_BYTES),
    )(tie_flags, ht, u, vt, s1, s2, stats)


def _residual_t_kernel(x_ref, yt_ref, o_ref):
    o_ref[...] = x_ref[...] + yt_ref[...].T


def _add_transposed(x, yt, tm):
    n, d = x.shape
    return pl.pallas_call(
        _residual_t_kernel,
        out_shape=jax.ShapeDtypeStruct((n, d), F32),
        grid=(n // tm,),
        in_specs=[pl.BlockSpec((tm, d), lambda i: (i, 0)), pl.BlockSpec((d, tm), lambda i: (0, i))],
        out_specs=pl.BlockSpec((tm, d), lambda i: (i, 0)),
        compiler_params=_params("parallel"),
    )(x, yt)


def kernel(x_prompt, x_sample, cache_fox_k, cache_fox_v, cache_fox_logf, cache_sb_k, cache_sb_v, state_rwkv, state_rwkv_shift, page_table, norm_mix, w_in, b_forget, fox_q_norm, fox_k_norm, sb_q_norm, sb_k_norm, rwkv_mu, rwkv_w0, rwkv_w2, rwkv_a0, rwkv_a2, rwkv_g2, rwkv_k_k, rwkv_k_a, rwkv_r_k, rwkv_ln_w, rwkv_ln_b, w_out, norm_ffn, peer_wq, peer_subkeys, peer_u, peer_v):
    b, t, d = x_prompt.shape
    bs = x_sample.shape[0]
    depth = w_in.shape[0]
    hq = d // HEAD_DIM
    hkv = hq // GQA
    dkv = hkv * HEAD_DIM
    rd, ri, rg = rwkv_w2.shape[1], rwkv_a2.shape[1], rwkv_g2.shape[1]
    n_prompt = b * t
    tr = _pick(t, (256, 128))
    n = _round_up(n_prompt + bs, tr)
    tm = _pick(n, (768, 512, 384, 256, 128))
    tm_in = _pick(n, (1408, 768, 512, 384, 256, 128))
    tn_in = 512 if d % 2048 == 0 else LANES
    lay = _Layout(d, hq, rd, ri, rg, tn_in)
    zoff = {k: lay.off[k] - lay.z_lo for k in ("r", "k", "v", "zw", "za", "zg")}
    rdp, rip, rgp = _round_up(rd, LANES), _round_up(ri, LANES), _round_up(rg, LANES)
    tn_peer = _pick(n, (768, 256, 128))
    te_peer = 4 * peer_subkeys.shape[3]

    def tile_heads(g, count):
        return jnp.tile(g, count)

    def pad_rows(a, rows):
        return jnp.concatenate([a, jnp.zeros((rows - a.shape[0],) + a.shape[1:], a.dtype)], axis=0)

    def with_tail(full, sample_rows):
        return full.at[n_prompt:].set(pad_rows(sample_rows, n - n_prompt))

    def one_row(a):
        return a.reshape(bs, 1, d)

    def zr_original_order(rows):
        return jnp.concatenate([rows[..., zoff[k]:zoff[k] + w] for k, w in
                                (("r", d), ("k", d), ("v", d), ("zw", rd), ("za", ri), ("zg", rg))], axis=-1)

    def to_z_layout(a):
        parts, o, src = [], 0, 0
        for k, w in (("r", d), ("k", d), ("v", d), ("zw", rd), ("za", ri), ("zg", rg)):
            if zoff[k] > o:
                parts.append(jnp.zeros(a.shape[:-1] + (zoff[k] - o,), a.dtype))
            parts.append(a[..., src:src + w])
            o, src = zoff[k] + w, src + w
        if lay.z_width > o:
            parts.append(jnp.zeros(a.shape[:-1] + (lay.z_width - o,), a.dtype))
        return jnp.concatenate(parts, axis=-1)

    x = pad_rows(jnp.concatenate([x_prompt.reshape(n_prompt, d), x_sample.reshape(bs, d)], axis=0), n)
    n_pool, page = cache_fox_logf.shape[1], cache_fox_logf.shape[2]
    lf_cache = jnp.swapaxes(cache_fox_logf, 2, 3).reshape(depth * n_pool, hq, page)
    lf_cache = _pad_heads(lf_cache, hkv).reshape(depth, n_pool, hkv * DEC_ROWS, page)

    prompt_states, sample_states = [], []
    for l in range(depth):
        h = _rmsnorm(x, norm_mix[l], tm)
        w_l = lay.scatter_cols(w_in[l], lay.ncol).astype(BF16)
        colp = jnp.zeros((lay.n_in,), F32)
        for name, g, cnt in (("fq", fox_q_norm[l], hq), ("fk", fox_k_norm[l], hkv), ("sq", sb_q_norm[l], hq),
                             ("sk", sb_k_norm[l], hkv)):
            s0, w0 = lay.src[name]
            colp = colp.at[s0:s0 + w0].set(tile_heads(g, cnt))
        s0, w0 = lay.src["ff"]
        colp = colp.at[s0:s0 + w0].set(b_forget[l])
        colp = lay.scatter_cols(colp.reshape(1, -1), lay.ncol)
        p = _inproj(h, w_l, colp, jnp.asarray(lay.modes), tm_in, tn_in)

        def cols(name, width, rows=slice(None)):
            return p[rows, lay.off[name]:lay.off[name] + width]

        c = _cumsum_time(p, b, t, lay.off["ff"] // LANES)[:, :hq].reshape(b, t, hq)
        tq = _pick(t, FOX_Q_TILES)
        c_t = jnp.swapaxes(c, 1, 2)
        cq = c_t.reshape(b, hq, t, 1)
        ck = c_t.reshape(b, hq, t // tq, 1, tq)
        o_fox_p = _prompt_attention(p, lay, b, t, d, cq, ck)
        o_sb_p = _prompt_attention(p, lay, b, t, d)

        srow = slice(n_prompt, n_prompt + bs)
        scale = HEAD_DIM ** -0.5
        fq_s = _pad_heads((cols("fq", d, srow) * scale).reshape(bs, hq, HEAD_DIM), hkv)
        sq_s = _pad_heads((cols("sq", d, srow) * scale).reshape(bs, hq, HEAD_DIM), hkv)
        fk_s = cols("fk", dkv, srow).reshape(bs, hkv, HEAD_DIM)
        fv_s = cols("fv", dkv, srow).reshape(bs, hkv, HEAD_DIM)
        lf_s = cols("ff", hq, srow)
        o_fox_s = _decode_attention(l, page_table, fq_s, cache_fox_k, cache_fox_v, lf_cache, fk_s, fv_s,
                                    _pad_heads(lf_s.reshape(bs, hq, 1), hkv))
        o_sb_s = _decode_attention(l, page_table, sq_s, cache_sb_k, cache_sb_v)
        o_fox = with_tail(o_fox_p, _unpad_heads(o_fox_s, hkv))
        o_sb = with_tail(o_sb_p, _unpad_heads(o_sb_s, hkv))

        state_pad = pad_rows(to_z_layout(state_rwkv_shift[l]), tr)
        zs = _token_shift(p, lay, state_pad, to_z_layout(rwkv_mu[l]).reshape(1, -1), t, n_prompt, tr)
        pad_k = lambda w2, rp: pad_rows(w2, rp).astype(BF16)
        vec = lambda a: a.reshape(1, d)
        dec, cl, kmod, na, nb, gg = _rwkv_prep(zs, zoff, pad_k(rwkv_w2[l], rdp), pad_k(rwkv_a2[l], rip),
                                               pad_k(rwkv_g2[l], rgp), vec(rwkv_w0[l]), vec(rwkv_a0[l]),
                                               vec(rwkv_k_k[l]), vec(rwkv_k_a[l]), d, tr)
        y_p, s_p = _rwkv_scan_prompt(zs, zoff["r"] // LANES, zoff["v"] // LANES, cl, kmod, na, nb, b, t, d)
        one = lambda a: a[srow].reshape(bs, 1, d)
        zs_s = zs[srow]
        y_s, s_s = _rwkv_scan_sample(one_row(zs_s[:, zoff["r"]:zoff["r"] + d]), one(dec), one(kmod),
                                     one_row(zs_s[:, zoff["v"]:zoff["v"] + d]), one(na), one(nb), state_rwkv[l])
        y = with_tail(y_p, y_s.reshape(bs, d))
        merged = _rwkv_post_merge(y, zs, zoff, kmod, gg, vec(rwkv_r_k[l].reshape(-1)), vec(rwkv_ln_w[l]),
                                  vec(rwkv_ln_b[l]), p, lay, o_fox, o_sb, d, tr)
        x = _outproj(merged, w_out[l].astype(BF16), x, tm)

        h2t = _rmsnorm(x, norm_ffn[l], tm, transposed=True)
        wq_t = peer_wq[l].T.reshape(-1, peer_wq.shape[-1] // peer_subkeys.shape[1], d).astype(BF16)
        s1, s2, stats = _peer_gate(h2t, wq_t, peer_subkeys[l].astype(BF16), tn_peer)
        vt = jnp.swapaxes(peer_v[l].reshape(-1, te_peer, d), 1, 2).astype(BF16)
        yt = _peer_dense(h2t, peer_u[l].astype(BF16), vt, s1, s2, stats, tn_peer, n_prompt + bs)
        x = _add_transposed(x, yt, tr)

        last = jnp.arange(b) * t + (t - 1)
        zr_window = lambda rows: zr_original_order(p[rows][:, lay.z_lo:lay.z_lo + lay.z_width])
        prompt_states.append((
            cols("fk", dkv, slice(0, n_prompt)).reshape(b, t, hkv, HEAD_DIM),
            cols("fv", dkv, slice(0, n_prompt)).reshape(b, t, hkv, HEAD_DIM),
            cols("ff", hq, slice(0, n_prompt)).reshape(b, t, hq),
            cols("sk", dkv, slice(0, n_prompt)).reshape(b, t, hkv, HEAD_DIM),
            cols("sv", dkv, slice(0, n_prompt)).reshape(b, t, hkv, HEAD_DIM),
            s_p, zr_window(last)))
        sample_states.append((
            fk_s.reshape(bs, 1, hkv, HEAD_DIM), fv_s.reshape(bs, 1, hkv, HEAD_DIM), lf_s.reshape(bs, 1, hq),
            cols("sk", dkv, srow).reshape(bs, 1, hkv, HEAD_DIM), cols("sv", dkv, srow).reshape(bs, 1, hkv, HEAD_DIM),
            s_s, zr_window(srow)))

    stack = lambda states, i: jnp.stack([s[i] for s in states])
    y_prompt = x[:n_prompt].reshape(b, t, d)
    y_sample = x[n_prompt:n_prompt + bs].reshape(bs, 1, d)
    return ((y_prompt, y_sample) + tuple(stack(prompt_states, i) for i in range(7))
            + tuple(stack(sample_states, i) for i in range(7)))
```

```python
import functools

import numpy as np
import jax
import jax.numpy as jnp
from jax import lax
from jax.experimental import pallas as pl
from jax.experimental.pallas import tpu as pltpu

F32 = jnp.float32
BF16 = jnp.bfloat16
HIGHEST = lax.Precision.HIGHEST

NORM_EPS = 1e-6
RWKV_GN_EPS = 64e-5
LANES = 128
SUBLANES = 8
HEAD_DIM = 128
RWKV_HEAD = 64
CHUNK = 64
GQA = 4
FOX_Q_TILES = (512, 256, 128)
SB_Q_TILES = (256, 128)
PEER_TOPK = 16
VMEM_LIMIT_BYTES = 56 * 1024 * 1024
PEER_VMEM_LIMIT_BYTES = 60 * 1024 * 1024

MODE_ID, MODE_RMS, MODE_SIGMOID, MODE_LOGSIG = 0, 1, 2, 3


def _params(*sem):
    return pltpu.CompilerParams(dimension_semantics=sem, vmem_limit_bytes=VMEM_LIMIT_BYTES)


def _round_up(n, m):
    return (n + m - 1) // m * m


def _pick(n, cands):
    for c in cands:
        if c <= n and n % c == 0:
            return c
    raise ValueError(f"no tile for {n} in {cands}")


def _sigmoid(x):
    return 1.0 / (1.0 + jnp.exp(-x))


def _dot_by_01(x, m01, m01_first=False, pieces=3):
    mm = (lambda piece: jnp.dot(m01, piece, preferred_element_type=F32)) if m01_first else (
        lambda piece: jnp.dot(piece, m01, preferred_element_type=F32))
    out, rest = None, x
    for _ in range(pieces):
        piece = rest.astype(BF16)
        rest = rest - piece.astype(F32)
        out = mm(piece) if out is None else out + mm(piece)
    return out


def _log_sigmoid(x):
    return jnp.minimum(x, 0.0) - jnp.log1p(jnp.exp(-jnp.abs(x)))


def _log_sigmoid_bulk(x):
    return jnp.minimum(x, 0.0) - jnp.log(1.0 + jnp.exp(-jnp.abs(x)))


def _rmsnorm_kernel(x_ref, g_ref, o_ref, *, transposed):
    x = x_ref[...]
    ms = jnp.mean(x * x, axis=-1, keepdims=True)
    y = x * lax.rsqrt(ms + NORM_EPS) * g_ref[...]
    o_ref[...] = (y.T if transposed else y).astype(o_ref.dtype)


def _rmsnorm(x, g, tm, transposed=False):
    n, d = x.shape
    return pl.pallas_call(
        functools.partial(_rmsnorm_kernel, transposed=transposed),
        out_shape=jax.ShapeDtypeStruct((d, n) if transposed else (n, d), BF16),
        grid=(n // tm,),
        in_specs=[pl.BlockSpec((tm, d), lambda i: (i, 0)), pl.BlockSpec((1, d), lambda i: (0, 0))],
        out_specs=pl.BlockSpec((d, tm), lambda i: (0, i)) if transposed else pl.BlockSpec((tm, d), lambda i: (i, 0)),
        compiler_params=_params("parallel"),
    )(x, g.reshape(1, d))


def _inproj_kernel(modes_ref, h_ref, w_ref, cp_ref, o_ref):
    mode = modes_ref[pl.program_id(1)]
    project = lambda: jnp.dot(h_ref[...], w_ref[...], preferred_element_type=F32)

    @pl.when(mode == MODE_ID)
    def _():
        o_ref[...] = project()

    @pl.when(mode == MODE_RMS)
    def _():
        acc = project()
        for c in range(acc.shape[1] // HEAD_DIM):
            sl = slice(c * HEAD_DIM, (c + 1) * HEAD_DIM)
            xs = acc[:, sl]
            ms = jnp.mean(xs * xs, axis=-1, keepdims=True)
            o_ref[:, sl] = xs * lax.rsqrt(ms + NORM_EPS) * cp_ref[:, sl]

    @pl.when(mode == MODE_SIGMOID)
    def _():
        o_ref[...] = _sigmoid(project())

    @pl.when(mode == MODE_LOGSIG)
    def _():
        o_ref[...] = _log_sigmoid(project() + cp_ref[...])


def _inproj(h, w, colp, modes, tm, tn):
    n, d = h.shape
    ncol = w.shape[1]
    return pl.pallas_call(
        _inproj_kernel,
        out_shape=jax.ShapeDtypeStruct((n, ncol), F32),
        grid_spec=pltpu.PrefetchScalarGridSpec(
            num_scalar_prefetch=1,
            grid=(n // tm, ncol // tn),
            in_specs=[
                pl.BlockSpec((tm, d), lambda i, j, m: (i, 0)),
                pl.BlockSpec((d, tn), lambda i, j, m: (0, j)),
                pl.BlockSpec((1, tn), lambda i, j, m: (0, j)),
            ],
            out_specs=pl.BlockSpec((tm, tn), lambda i, j, m: (i, j)),
        ),
        compiler_params=_params("parallel", "parallel"),
    )(modes, h, w, colp)


class _Layout:
    def __init__(self, d, hq, rd, ri, rg, tn):
        dkv = d // GQA
        zw = 3 * d + rd + ri + rg
        src = {}
        o = 0
        for name, width in (("fq", d), ("fk", dkv), ("fv", dkv), ("ff", hq), ("sq", d), ("sk", dkv), ("sv", dkv)):
            src[name] = (o, width)
            o += width
        for name, width in (("r", d), ("k", d), ("v", d), ("zw", rd), ("za", ri), ("zg", rg)):
            src[name] = (o, width)
            o += width
        src["gates"] = (o, 3 * d)
        self.n_in = o + 3 * d
        self.zr_width = zw
        order = (("fq", MODE_RMS), ("sq", MODE_RMS), ("fk", MODE_RMS), ("fv", MODE_ID), ("sk", MODE_RMS),
                 ("sv", MODE_ID), ("r", MODE_ID), ("k", MODE_ID), ("v", MODE_ID), ("zw", MODE_ID),
                 ("za", MODE_ID), ("zg", MODE_ID), ("gates", MODE_SIGMOID), ("ff", MODE_LOGSIG))
        self.src = src
        self.off = {}
        self.pieces = []
        seg_modes = []
        o, prev = 0, None
        for name, mode in order:
            o = _round_up(o, tn if mode != prev else _round_up(min(src[name][1], 4 * LANES), LANES))
            self.off[name] = o
            seg_modes.append((o, mode))
            self.pieces.append((o, src[name][0], src[name][1]))
            o += src[name][1]
            prev = mode
        self.ncol = _round_up(o, tn)
        self.z_lo = self.off["r"]
        self.z_width = _round_up(self.off["zg"] + rg, LANES) - self.z_lo
        modes = np.zeros((self.ncol // tn,), np.int32)
        for t in range(self.ncol // tn):
            for so, m in seg_modes:
                if so <= t * tn:
                    modes[t] = m
        self.modes = modes

    def scatter_cols(self, a, width):
        parts, o = [], 0
        for dst, s, w in self.pieces:
            if dst > o:
                parts.append(jnp.zeros(a.shape[:-1] + (dst - o,), a.dtype))
            parts.append(a[..., s:s + w])
            o = dst + w
        if width > o:
            parts.append(jnp.zeros(a.shape[:-1] + (width - o,), a.dtype))
        return jnp.concatenate(parts, axis=-1)


def _cumsum_kernel(x_ref, o_ref, *, cb):
    t = x_ref.shape[0]
    row = lax.broadcasted_iota(jnp.int32, (cb, cb), 0)
    col = lax.broadcasted_iota(jnp.int32, (cb, cb), 1)
    tri = (row >= col).astype(F32)
    carry = jnp.zeros((1, x_ref.shape[1]), F32)
    for i in range(t // cb):
        c = jnp.dot(tri, x_ref[i * cb:(i + 1) * cb, :], precision=HIGHEST, preferred_element_type=F32) + carry
        o_ref[i * cb:(i + 1) * cb, :] = c
        carry = c[cb - 1:cb, :]


def _cumsum_time(p, b, t, col_block):
    cb = _pick(t, (256, 128, 64, 32, 16, 8))
    return pl.pallas_call(
        functools.partial(_cumsum_kernel, cb=cb),
        out_shape=jax.ShapeDtypeStruct((b * t, LANES), F32),
        grid=(b,),
        in_specs=[pl.BlockSpec((t, LANES), lambda i: (i, col_block))],
        out_specs=pl.BlockSpec((t, LANES), lambda i: (i, 0)),
        compiler_params=_params("parallel"),
    )(p)


def _stack_heads(q_ref, scale):
    tq = q_ref.shape[0]
    q = q_ref[...] * scale
    return jnp.concatenate([q[:, h * HEAD_DIM:(h + 1) * HEAD_DIM] for h in range(GQA)], axis=0).astype(BF16)


def _unstack_heads(o, o_ref):
    tq = o_ref.shape[0]
    for h in range(GQA):
        o_ref[:, h * HEAD_DIM:(h + 1) * HEAD_DIM] = o[h * tq:(h + 1) * tq, :]


def _fox_prompt_kernel(q_ref, k_ref, v_ref, cq_ref, ck_ref, o_ref, *, tq):
    qi = pl.program_id(2)
    rows = GQA * tq
    qs = _stack_heads(q_ref, HEAD_DIM ** -0.5)
    cq = cq_ref[0].reshape(rows, 1)

    def scores(ki):
        k = k_ref[pl.ds(pl.multiple_of(ki * tq, tq), tq), :].astype(BF16)
        s = lax.dot_general(qs, k, (((1,), (1,)), ((), ())), preferred_element_type=F32)
        ck = ck_ref[0, :, ki]
        return ((s + cq).reshape(GQA, tq, tq) - ck).reshape(rows, tq)

    def update(ki, s, carry):
        m, l, acc = carry
        m_new = jnp.maximum(m, jnp.max(s, axis=-1, keepdims=True))
        alpha = jnp.exp(m - m_new)
        p = jnp.exp(s - m_new)
        v = v_ref[pl.ds(pl.multiple_of(ki * tq, tq), tq), :].astype(BF16)
        acc = acc * alpha + jnp.dot(p.astype(BF16), v, preferred_element_type=F32)
        return m_new, l * alpha + jnp.sum(p, axis=-1, keepdims=True), acc

    init = (jnp.full((rows, 1), -jnp.inf, F32), jnp.zeros((rows, 1), F32), jnp.zeros((rows, HEAD_DIM), F32))
    carry = lax.fori_loop(0, qi, lambda ki, c: update(ki, scores(ki), c), init)
    r = lax.broadcasted_iota(jnp.int32, (GQA, tq, tq), 1)
    c = lax.broadcasted_iota(jnp.int32, (GQA, tq, tq), 2)
    s = jnp.where((c <= r).reshape(rows, tq), scores(qi), -jnp.inf)
    m, l, acc = update(qi, s, carry)
    _unstack_heads(acc / l, o_ref)


def _sb_prompt_kernel(q_ref, k_ref, v_ref, o_ref, *, tq):
    qi = pl.program_id(2)
    rows = GQA * tq
    qs = _stack_heads(q_ref, HEAD_DIM ** -0.5)
    jr = lax.broadcasted_iota(jnp.int32, (tq, tq), 0)
    sc = lax.broadcasted_iota(jnp.int32, (tq, tq), 1)
    later = (jr > sc).astype(BF16)

    def block(ki, carry, mask):
        run, acc = carry
        k = k_ref[pl.ds(pl.multiple_of(ki * tq, tq), tq), :].astype(BF16)
        v = v_ref[pl.ds(pl.multiple_of(ki * tq, tq), tq), :].astype(BF16)
        z = lax.dot_general(qs, k, (((1,), (1,)), ((), ())), preferred_element_type=F32)
        log_keep = _log_sigmoid_bulk(-z)
        if mask is not None:
            log_keep = jnp.where(mask, log_keep, 0.0)
        after = _dot_by_01(log_keep, later, pieces=2) + run
        w = jnp.exp(z + log_keep + after)
        if mask is not None:
            w = jnp.where(mask, w, 0.0)
        acc = acc + jnp.dot(w.astype(BF16), v, preferred_element_type=F32)
        return run + jnp.sum(log_keep, axis=-1, keepdims=True), acc

    r3 = lax.broadcasted_iota(jnp.int32, (GQA, tq, tq), 1)
    c3 = lax.broadcasted_iota(jnp.int32, (GQA, tq, tq), 2)
    carry = block(qi, (jnp.zeros((rows, 1), F32), jnp.zeros((rows, HEAD_DIM), F32)), (c3 < r3).reshape(rows, tq))
    run, acc = lax.fori_loop(0, qi, lambda it, c: block(qi - 1 - it, c, None), carry)
    _unstack_heads(acc, o_ref)


def _prompt_attention(p, lay, b, t, d, cq=None, ck=None):
    hkv = d // (GQA * HEAD_DIM)
    fox = cq is not None
    tq = ck.shape[-1] if fox else _pick(t, SB_Q_TILES)
    nq = t // tq
    gw = GQA * HEAD_DIM
    qn, kn, vn = ("fq", "fk", "fv") if fox else ("sq", "sk", "sv")
    qb, kb, vb = lay.off[qn] // gw, lay.off[kn] // HEAD_DIM, lay.off[vn] // HEAD_DIM
    in_specs = [
        pl.BlockSpec((tq, gw), lambda bi, g, qi: (bi * nq + qi, qb + g)),
        pl.BlockSpec((t, HEAD_DIM), lambda bi, g, qi: (bi, kb + g)),
        pl.BlockSpec((t, HEAD_DIM), lambda bi, g, qi: (bi, vb + g)),
    ]
    args = [p, p, p]
    if fox:
        in_specs += [
            pl.BlockSpec((1, GQA, tq, 1), lambda bi, g, qi: (bi, g, qi, 0)),
            pl.BlockSpec((1, GQA, nq, 1, tq), lambda bi, g, qi: (bi, g, 0, 0, 0)),
        ]
        args += [cq, ck]
        body = functools.partial(_fox_prompt_kernel, tq=tq)
    else:
        body = functools.partial(_sb_prompt_kernel, tq=tq)
    return pl.pallas_call(
        body,
        out_shape=jax.ShapeDtypeStruct((p.shape[0], d), F32),
        grid=(b, hkv, nq),
        in_specs=in_specs,
        out_specs=pl.BlockSpec((tq, gw), lambda bi, g, qi: (bi * nq + qi, g)),
        compiler_params=_params("parallel", "parallel", "arbitrary"),
    )(*args)


DEC_ROWS = 2 * SUBLANES


def _decode_kernel(pt_ref, q_ref, *rest, fox, hkv, pp):
    kc_refs, vc_refs, rest = rest[:pp], rest[pp:2 * pp], rest[2 * pp:]
    if fox:
        lf_refs = rest[:pp]
        knew_ref, vnew_ref, lfnew_ref, o_ref, m_ref, l_ref, run_ref, acc_ref = rest[pp:]
    else:
        o_ref, run_ref, acc_ref = rest
    pg = pl.program_id(1)
    npg = pl.num_programs(1)
    page = kc_refs[0].shape[2]
    rows = hkv * DEC_ROWS

    @pl.when(pg == 0)
    def _():
        if fox:
            for g in range(hkv):
                sl = slice(g * DEC_ROWS, (g + 1) * DEC_ROWS)
                qg = q_ref[0, sl, :].astype(BF16).astype(F32)
                kg = knew_ref[0, g:g + 1, :].astype(BF16).astype(F32)
                m_ref[sl, :] = jnp.sum(qg * kg, axis=-1, keepdims=True)
                acc_ref[sl, :] = jnp.broadcast_to(vnew_ref[0, g:g + 1, :].astype(BF16).astype(F32), (DEC_ROWS, HEAD_DIM))
            l_ref[...] = jnp.ones_like(l_ref)
            run_ref[...] = lfnew_ref[0]
        else:
            run_ref[...] = jnp.zeros_like(run_ref)
            acc_ref[...] = jnp.zeros_like(acc_ref)

    jr = lax.broadcasted_iota(jnp.int32, (page, page), 0)
    sc = lax.broadcasted_iota(jnp.int32, (page, page), 1)
    later = (jr > sc).astype(BF16)
    k_pages = [pltpu.einshape("khd->hkd", ref[0, 0]) for ref in kc_refs]
    v_pages = [pltpu.einshape("khd->hkd", ref[0, 0]) for ref in vc_refs]
    k_heads = [jnp.concatenate([kp[g] for kp in k_pages], axis=0).astype(BF16) for g in range(hkv)]
    v_heads = [jnp.concatenate([vp[g] for vp in v_pages], axis=0).astype(BF16) for g in range(hkv)]
    z = jnp.concatenate(
        [lax.dot_general(q_ref[0, g * DEC_ROWS:(g + 1) * DEC_ROWS, :].astype(BF16), k_heads[g],
                         (((1,), (1,)), ((), ())), preferred_element_type=F32) for g in range(hkv)], axis=0)
    run = run_ref[...]
    in_page = lambda x, j: x[:, j * page:(j + 1) * page]

    def later_sums(per_key):
        out, carry = [], run
        for j in range(pp):
            out.append(_dot_by_01(in_page(per_key, j), later) + carry)
            carry = carry + jnp.sum(in_page(per_key, j), axis=-1, keepdims=True)
        return jnp.concatenate(out, axis=1), carry

    if fox:
        bias, run_ref[...] = later_sums(jnp.concatenate([ref[0, 0] for ref in lf_refs], axis=1))
        s = z + bias
        m = m_ref[...]
        m_new = jnp.maximum(m, jnp.max(s, axis=-1, keepdims=True))
        alpha = jnp.exp(m - m_new)
        w = jnp.exp(s - m_new)
        l_ref[...] = l_ref[...] * alpha + jnp.sum(w, axis=-1, keepdims=True)
        m_ref[...] = m_new
    else:
        log_keep = _log_sigmoid_bulk(-z)
        after, run_ref[...] = later_sums(log_keep)
        w = jnp.exp(z + log_keep + after)
        alpha = None
    wb = w.astype(BF16)
    for g in range(hkv):
        sl = slice(g * DEC_ROWS, (g + 1) * DEC_ROWS)
        pv = jnp.dot(wb[sl, :], v_heads[g], preferred_element_type=F32)
        if fox:
            acc_ref[sl, :] = acc_ref[sl, :] * alpha[sl, :] + pv
        else:
            acc_ref[sl, :] = acc_ref[sl, :] + pv

    @pl.when(pg == npg - 1)
    def _():
        if fox:
            o_ref[0] = acc_ref[...] / l_ref[...]
        else:
            o_ref[0] = acc_ref[...]


def _decode_attention(layer, page_table, q, kcache, vcache, lf_cache=None, knew=None, vnew=None, lfnew=None):
    bs, npg = page_table.shape
    _, _, page, hkv, _ = kcache.shape
    rows = hkv * DEC_ROWS
    fox = lf_cache is not None
    pp = _pick(npg, (8, 4, 2, 1))
    nth_page = lambda j: (lambda b, p, pt: (layer, pt[b, npg - 1 - (p * pp + j)], 0, 0, 0))
    cache_specs = [pl.BlockSpec((1, 1, page, hkv, HEAD_DIM), nth_page(j)) for j in range(pp)]
    in_specs = [pl.BlockSpec((1, rows, HEAD_DIM), lambda b, p, pt: (b, 0, 0))] + cache_specs + cache_specs
    args = [q] + [kcache] * pp + [vcache] * pp
    scratch = [pltpu.VMEM((rows, 1), F32), pltpu.VMEM((rows, HEAD_DIM), F32)]
    if fox:
        in_specs += [pl.BlockSpec((1, 1, rows, page), lambda b, p, pt, j=j: (layer, pt[b, npg - 1 - (p * pp + j)], 0, 0))
                     for j in range(pp)]
        in_specs += [
            pl.BlockSpec((1, hkv, HEAD_DIM), lambda b, p, pt: (b, 0, 0)),
            pl.BlockSpec((1, hkv, HEAD_DIM), lambda b, p, pt: (b, 0, 0)),
            pl.BlockSpec((1, rows, 1), lambda b, p, pt: (b, 0, 0)),
        ]
        args += [lf_cache] * pp + [knew, vnew, lfnew]
        scratch = [pltpu.VMEM((rows, 1), F32), pltpu.VMEM((rows, 1), F32)] + scratch
    return pl.pallas_call(
        functools.partial(_decode_kernel, fox=fox, hkv=hkv, pp=pp),
        out_shape=jax.ShapeDtypeStruct((bs, rows, HEAD_DIM), F32),
        grid_spec=pltpu.PrefetchScalarGridSpec(
            num_scalar_prefetch=1,
            grid=(bs, npg // pp),
            in_specs=in_specs,
            out_specs=pl.BlockSpec((1, rows, HEAD_DIM), lambda b, p, pt: (b, 0, 0)),
            scratch_shapes=scratch,
        ),
        compiler_params=_params("parallel", "arbitrary"),
    )(page_table, *args)


def _pad_heads(a, hkv):
    bs, _, x = a.shape
    a = a.reshape(bs, hkv, GQA, x)
    a = jnp.concatenate([a, jnp.zeros((bs, hkv, DEC_ROWS - GQA, x), a.dtype)], axis=2)
    return a.reshape(bs, hkv * DEC_ROWS, x)


def _unpad_heads(a, hkv):
    bs, _, x = a.shape
    return a.reshape(bs, hkv, DEC_ROWS, x)[:, :, :GQA].reshape(bs, hkv * GQA * x)


def _shift_kernel(z_ref, prev_ref, st_ref, mu_ref, o_ref, *, tiles_per_seq, sample_tile):
    i = pl.program_id(0)
    z = z_ref[...]
    first = (i % tiles_per_seq) == 0
    row0 = jnp.where(first, 0.0, prev_ref[SUBLANES - 1:SUBLANES, :])
    rid = lax.broadcasted_iota(jnp.int32, z.shape, 0)
    zp = jnp.where(rid == 0, row0, pltpu.roll(z, 1, 0))
    zp = jnp.where(i == sample_tile, st_ref[...], zp)
    o_ref[...] = z + (zp - z) * mu_ref[...]


def _token_shift(p, lay, state_pad, mu, t, n_prompt, tr):
    n = p.shape[0]
    zp = lay.z_width
    tc = _pick(zp, tuple(c for c in (1024, 896, 768, 640, 512, 384, 256, 128) if lay.z_lo % c == 0))
    cb = lay.z_lo // tc
    per8 = tr // SUBLANES
    return pl.pallas_call(
        functools.partial(_shift_kernel, tiles_per_seq=t // tr, sample_tile=n_prompt // tr),
        out_shape=jax.ShapeDtypeStruct((n, zp), F32),
        grid=(n // tr, zp // tc),
        in_specs=[
            pl.BlockSpec((tr, tc), lambda i, c: (i, cb + c)),
            pl.BlockSpec((SUBLANES, tc), lambda i, c: (jnp.maximum(i * per8 - 1, 0), cb + c)),
            pl.BlockSpec((tr, tc), lambda i, c: (0, c)),
            pl.BlockSpec((1, tc), lambda i, c: (0, c)),
        ],
        out_specs=pl.BlockSpec((tr, tc), lambda i, c: (i, c)),
        compiler_params=_params("parallel", "parallel"),
    )(p, p, state_pad, mu)


def _head_sum_matrix(width):
    a = lax.broadcasted_iota(jnp.int32, (width, width), 0) // RWKV_HEAD
    b = lax.broadcasted_iota(jnp.int32, (width, width), 1) // RWKV_HEAD
    return (a == b).astype(BF16)


def _rwkv_prep_kernel(k_ref, zw_ref, za_ref, zg_ref, w2_ref, a2_ref, g2_ref, w0_ref, a0_ref, kk_ref, ka_ref,
                      dec_ref, cl_ref, kmod_ref, na_ref, nb_ref, g_ref):
    k = k_ref[...]
    dw = jnp.dot(jnp.tanh(zw_ref[...]).astype(BF16), w2_ref[...], preferred_element_type=F32)
    x = w0_ref[...] + dw
    w_log = _log_sigmoid(x) - 0.5
    log_dec = -jnp.exp(w_log)
    dec_ref[...] = jnp.exp(log_dec)
    tok = lax.broadcasted_iota(jnp.int32, (CHUNK, CHUNK), 0)
    src = lax.broadcasted_iota(jnp.int32, (CHUNK, CHUNK), 1)
    upto = (src <= tok).astype(BF16)
    for c in range(k.shape[0] // CHUNK):
        rows = slice(c * CHUNK, (c + 1) * CHUNK)
        cl_ref[rows, :] = _dot_by_01(log_dec[rows, :], upto, m01_first=True)
    a = _sigmoid(a0_ref[...] + jnp.dot(za_ref[...].astype(BF16), a2_ref[...], preferred_element_type=F32))
    g_ref[...] = jnp.dot(_sigmoid(zg_ref[...]).astype(BF16), g2_ref[...], preferred_element_type=F32)
    kk = k * kk_ref[...]
    hs = _head_sum_matrix(LANES)
    for c in range(k.shape[1] // LANES):
        sl = slice(c * LANES, (c + 1) * LANES)
        kc = kk[:, sl]
        ss = _dot_by_01(kc * kc, hs)
        kn = kc / jnp.maximum(jnp.sqrt(ss), 1e-12)
        na_ref[:, sl] = -kn
        nb_ref[:, sl] = kn * a[:, sl]
    kmod_ref[...] = k * (1.0 + (a - 1.0) * ka_ref[...])


def _rwkv_prep(zs, zoff, w2, a2, g2, w0, a0, k_k, k_a, d, tr):
    n = zs.shape[0]
    tc = _pick(d, (1024, 512, 256, 128))
    rdp, rip, rgp = w2.shape[0], a2.shape[0], g2.shape[0]
    row = lambda i, c: (i, c)
    vec = pl.BlockSpec((1, tc), lambda i, c: (0, c))
    out = jax.ShapeDtypeStruct((n, d), F32)
    return pl.pallas_call(
        _rwkv_prep_kernel,
        out_shape=[out] * 6,
        grid=(n // tr, d // tc),
        in_specs=[
            pl.BlockSpec((tr, tc), lambda i, c: (i, zoff["k"] // tc + c)),
            pl.BlockSpec((tr, rdp), lambda i, c: (i, zoff["zw"] // rdp)),
            pl.BlockSpec((tr, rip), lambda i, c: (i, zoff["za"] // rip)),
            pl.BlockSpec((tr, rgp), lambda i, c: (i, zoff["zg"] // rgp)),
            pl.BlockSpec((rdp, tc), lambda i, c: (0, c)),
            pl.BlockSpec((rip, tc), lambda i, c: (0, c)),
            pl.BlockSpec((rgp, tc), lambda i, c: (0, c)),
            vec, vec, vec, vec,
        ],
        out_specs=[pl.BlockSpec((tr, tc), row)] * 6,
        compiler_params=_params("parallel", "parallel"),
    )(zs, zs, zs, zs, w2, a2, g2, w0, a0, k_k, k_a)


def _scan_step(state, r, w, k, v, a, b, lo, eye0, eye1):
    def half_sums(x):
        s0 = jnp.sum(jnp.where(lo, x, 0.0), axis=1, keepdims=True)
        s1 = jnp.sum(jnp.where(lo, 0.0, x), axis=1, keepdims=True)
        return s0, s1

    sa0, sa1 = half_sums(state * a)
    vb = jnp.broadcast_to(v, state.shape)
    v0 = jnp.sum(jnp.where(eye0, vb, 0.0), axis=1, keepdims=True)
    v1 = jnp.sum(jnp.where(eye1, vb, 0.0), axis=1, keepdims=True)
    state = state * w + jnp.where(lo, sa0, sa1) * b + jnp.where(lo, v0, v1) * k
    y0, y1 = half_sums(state * r)
    y = jnp.sum(jnp.where(eye0, y0, jnp.where(eye1, y1, 0.0)), axis=0, keepdims=True)
    return state, y


def _scan_masks():
    shape = (RWKV_HEAD, LANES)
    i = lax.broadcasted_iota(jnp.int32, shape, 0)
    c = lax.broadcasted_iota(jnp.int32, shape, 1)
    return c < RWKV_HEAD, c == i, c == i + RWKV_HEAD


def _load_state(s_ref, p):
    return jnp.concatenate([s_ref[0, 2 * p], s_ref[0, 2 * p + 1]], axis=-1)


def _store_state(s_ref, p, state):
    s_ref[0, 2 * p] = state[:, :RWKV_HEAD]
    s_ref[0, 2 * p + 1] = state[:, RWKV_HEAD:]


def _mm(a, b):
    return jnp.dot(a.astype(BF16), b.astype(BF16), preferred_element_type=F32)


def _mm_nt(a, b):
    return lax.dot_general(a.astype(BF16), b.astype(BF16), (((1,), (1,)), ((), ())), preferred_element_type=F32)


def _mm_tn(a, b):
    return lax.dot_general(a.astype(BF16), b.astype(BF16), (((0,), (0,)), ((), ())), preferred_element_type=F32)


def _scan_chunk(sd, r, cl, k, v, a, b, lo, strict, incl, same_head):
    def blockdiag(x):
        return jnp.concatenate([jnp.where(lo, x, 0.0), jnp.where(lo, 0.0, x)], axis=0)

    pairs = range(len(sd))
    first = lax.broadcasted_iota(jnp.int32, cl[0].shape, 0) == 0
    g = [jnp.exp(cl[p]) for p in pairs]
    g_prev = [jnp.exp(jnp.where(first, 0.0, pltpu.roll(cl[p], 1, 0))) for p in pairs]
    g_inv = [jnp.exp(-cl[p]) for p in pairs]
    bt = [b[p] * g_inv[p] for p in pairs]
    kt = [k[p] * g_inv[p] for p in pairs]
    lhs = [jnp.concatenate([a[p] * g_prev[p], r[p] * g[p]], axis=0) for p in pairs]
    keys = [jnp.concatenate([jnp.where(lo, bt[p], 0.0), jnp.where(lo, 0.0, bt[p]),
                             jnp.where(lo, kt[p], 0.0), jnp.where(lo, 0.0, kt[p])], axis=0) for p in pairs]
    gram = [_mm_nt(lhs[p], keys[p]) for p in pairs]
    from_state = [_mm_nt(lhs[p], sd[p]) for p in pairs]
    nk = [jnp.where(strict, gram[p][:CHUNK, :LANES], 0.0) for p in pairs]
    vd = [blockdiag(v[p]) for p in pairs]
    x = [from_state[p][:CHUNK] + _mm(jnp.where(strict, gram[p][:CHUNK, LANES:], 0.0), vd[p]) for p in pairs]
    steps = CHUNK.bit_length() - 1
    for it in range(steps):
        x = [x[p] + _mm(nk[p], blockdiag(x[p])) for p in pairs]
        if it + 1 < steps:
            nk = [_mm(nk[p], blockdiag(nk[p])) for p in pairs]
    r_bk = [jnp.concatenate([jnp.where(incl, gram[p][CHUNK:, :LANES], 0.0),
                             jnp.where(incl, gram[p][CHUNK:, LANES:], 0.0)], axis=1) for p in pairs]
    y = [from_state[p][CHUNK:] + _mm(r_bk[p], jnp.concatenate([blockdiag(x[p]), vd[p]], axis=0)) for p in pairs]
    g_last = [g[p][CHUNK - 1:CHUNK, :] for p in pairs]
    update = [_mm_tn(jnp.concatenate([x[p], v[p]], axis=0),
                     jnp.concatenate([bt[p] * g_last[p], kt[p] * g_last[p]], axis=0)) for p in pairs]
    return [sd[p] * g_last[p] + jnp.where(same_head, update[p], 0.0) for p in pairs], y


def _scan_prompt_kernel(r_ref, cl_ref, k_ref, v_ref, a_ref, b_ref, y_ref, sout_ref, sd_ref, *, pairs, nchunk):
    tc = pl.program_id(2)

    @pl.when(tc == 0)
    def _():
        sd_ref[...] = jnp.zeros_like(sd_ref)

    lane = lax.broadcasted_iota(jnp.int32, (CHUNK, LANES), 1)
    tok = lax.broadcasted_iota(jnp.int32, (CHUNK, LANES), 0)
    lo = lane < RWKV_HEAD
    src = lane % RWKV_HEAD
    strict, incl = src < tok, src <= tok
    hr = lax.broadcasted_iota(jnp.int32, (LANES, LANES), 0) // RWKV_HEAD
    hc = lax.broadcasted_iota(jnp.int32, (LANES, LANES), 1) // RWKV_HEAD
    same_head = hr == hc

    def chunk(ci, carry):
        rows = pl.ds(pl.multiple_of(ci * CHUNK, CHUNK), CHUNK)
        cols = [slice(p * LANES, (p + 1) * LANES) for p in range(pairs)]
        take = lambda ref: [ref[rows, sl] for sl in cols]
        sd, y = _scan_chunk([sd_ref[p] for p in range(pairs)], take(r_ref), take(cl_ref), take(k_ref), take(v_ref),
                            take(a_ref), take(b_ref), lo, strict, incl, same_head)
        for p in range(pairs):
            sd_ref[p] = sd[p]
            y_ref[rows, cols[p]] = y[p]
        return carry

    lax.fori_loop(0, nchunk, chunk, 0)

    @pl.when(tc == pl.num_programs(2) - 1)
    def _():
        for p in range(pairs):
            sd = sd_ref[p]
            sout_ref[0, 2 * p] = sd[:RWKV_HEAD, :RWKV_HEAD]
            sout_ref[0, 2 * p + 1] = sd[RWKV_HEAD:, RWKV_HEAD:]


def _scan_sample_kernel(r_ref, w_ref, k_ref, v_ref, a_ref, b_ref, s0_ref, y_ref, sout_ref, *, pairs):
    lo, eye0, eye1 = _scan_masks()
    for p in range(pairs):
        sl = slice(p * LANES, (p + 1) * LANES)
        row = lambda ref: ref[0, :, sl]
        state, y = _scan_step(_load_state(s0_ref, p), row(r_ref), row(w_ref), row(k_ref), row(v_ref), row(a_ref),
                              row(b_ref), lo, eye0, eye1)
        _store_state(sout_ref, p, state)
        y_ref[0, :, sl] = y


def _rwkv_scan_prompt(r_src, r_cb, v_cb, cl, kmod, na, nb, b, t, d):
    pairs = _pick(d // LANES, (8, 4, 2, 1))
    wcol = pairs * LANES
    tchunk = _pick(t, (256, 128))
    nt = t // tchunk
    nh = d // RWKV_HEAD
    rowmap = lambda bi, g, tc: (bi * nt + tc, g)
    blk = pl.BlockSpec((tchunk, wcol), rowmap)
    return pl.pallas_call(
        functools.partial(_scan_prompt_kernel, pairs=pairs, nchunk=tchunk // CHUNK),
        out_shape=[jax.ShapeDtypeStruct((cl.shape[0], d), F32), jax.ShapeDtypeStruct((b, nh, RWKV_HEAD, RWKV_HEAD), F32)],
        grid=(b, d // wcol, nt),
        in_specs=[
            pl.BlockSpec((tchunk, wcol), lambda bi, g, tc: (bi * nt + tc, r_cb // pairs + g)),
            blk, blk,
            pl.BlockSpec((tchunk, wcol), lambda bi, g, tc: (bi * nt + tc, v_cb // pairs + g)),
            blk, blk,
        ],
        out_specs=[blk, pl.BlockSpec((1, 2 * pairs, RWKV_HEAD, RWKV_HEAD), lambda bi, g, tc: (bi, g, 0, 0))],
        scratch_shapes=[pltpu.VMEM((pairs, LANES, LANES), F32)],
        compiler_params=_params("parallel", "parallel", "arbitrary"),
    )(r_src, cl, kmod, r_src, na, nb)


def _rwkv_scan_sample(r, dec, kmod, v, na, nb, s0):
    bs, _, d = r.shape
    pairs = _pick(d // LANES, (4, 2, 1))
    wcol = pairs * LANES
    blk = pl.BlockSpec((1, 1, wcol), lambda bi, g: (bi, 0, g))
    sblk = pl.BlockSpec((1, 2 * pairs, RWKV_HEAD, RWKV_HEAD), lambda bi, g: (bi, g, 0, 0))
    return pl.pallas_call(
        functools.partial(_scan_sample_kernel, pairs=pairs),
        out_shape=[jax.ShapeDtypeStruct((bs, 1, d), F32), jax.ShapeDtypeStruct(s0.shape, F32)],
        grid=(bs, d // wcol),
        in_specs=[blk] * 6 + [sblk],
        out_specs=[blk, sblk],
        compiler_params=_params("parallel", "parallel"),
    )(r, dec, kmod, v, na, nb, s0)


def _rwkv_post_kernel(y_ref, r_ref, k_ref, v_ref, g_ref, rk_ref, lw_ref, lb_ref, gf_ref, gr_ref, gs_ref, of_ref, os_ref,
                      o_ref):
    hs = _head_sum_matrix(LANES)
    inv = 1.0 / RWKV_HEAD
    for c in range(y_ref.shape[1] // LANES):
        sl = slice(c * LANES, (c + 1) * LANES)
        y = y_ref[:, sl]
        mu = _dot_by_01(y, hs) * inv
        yc = y - mu
        var = _dot_by_01(yc * yc, hs) * inv
        yn = yc * lax.rsqrt(var + RWKV_GN_EPS) * lw_ref[:, sl] + lb_ref[:, sl]
        v = v_ref[:, sl]
        bonus = _dot_by_01(r_ref[:, sl] * k_ref[:, sl] * rk_ref[:, sl], hs) * v
        o_rwkv = (yn + bonus) * g_ref[:, sl]
        o_ref[:, sl] = (gf_ref[:, sl] * of_ref[:, sl] + gr_ref[:, sl] * o_rwkv + gs_ref[:, sl] * os_ref[:, sl]).astype(BF16)


def _rwkv_post_merge(y, zs, zoff, kmod, g, r_k, ln_w, ln_b, p, lay, o_fox, o_sb, d, tr):
    n = y.shape[0]
    tc = _pick(d, tuple(c for c in (1024, 512, 256, 128) if lay.off["gates"] % c == 0))
    gb = lay.off["gates"] // tc
    per = d // tc
    blk = pl.BlockSpec((tr, tc), lambda i, c: (i, c))
    vec = pl.BlockSpec((1, tc), lambda i, c: (0, c))
    gate = lambda which: pl.BlockSpec((tr, tc), lambda i, c: (i, gb + which * per + c))
    return pl.pallas_call(
        _rwkv_post_kernel,
        out_shape=jax.ShapeDtypeStruct((n, d), BF16),
        grid=(n // tr, d // tc),
        in_specs=[
            blk,
            pl.BlockSpec((tr, tc), lambda i, c: (i, zoff["r"] // tc + c)),
            blk,
            pl.BlockSpec((tr, tc), lambda i, c: (i, zoff["v"] // tc + c)),
            blk, vec, vec, vec,
            gate(0), gate(1), gate(2), blk, blk,
        ],
        out_specs=blk,
        compiler_params=_params("parallel", "parallel"),
    )(y, zs, kmod, zs, g, r_k, ln_w, ln_b, p, p, p, o_fox, o_sb)


def _outproj_kernel(m_ref, w_ref, x_ref, o_ref):
    o_ref[...] = x_ref[...] + jnp.dot(m_ref[...], w_ref[...], preferred_element_type=F32)


def _outproj(merged, w, x, tm):
    n, d = merged.shape
    tn = _pick(w.shape[1], (512, 256, 128))
    return pl.pallas_call(
        _outproj_kernel,
        out_shape=jax.ShapeDtypeStruct((n, w.shape[1]), F32),
        grid=(w.shape[1] // tn, n // tm),
        in_specs=[
            pl.BlockSpec((tm, d), lambda j, i: (i, 0)),
            pl.BlockSpec((d, tn), lambda j, i: (0, j)),
            pl.BlockSpec((tm, tn), lambda j, i: (i, j)),
        ],
        out_specs=pl.BlockSpec((tm, tn), lambda j, i: (i, j)),
        compiler_params=_params("parallel", "parallel"),
    )(merged, w, x)


def _top16_rows(s):
    nk = s.shape[0]
    idx = lax.broadcasted_iota(jnp.int32, s.shape, 0).astype(F32)
    work = s
    tops, where = [], []
    for _ in range(PEER_TOPK):
        m = jnp.max(work, axis=0, keepdims=True)
        first = jnp.min(jnp.where(work == m, idx, float(nk)), axis=0, keepdims=True)
        work = jnp.where(idx == first, -jnp.inf, work)
        tops.append(m)
        where.append(first)
    kept = jnp.where(work == s, -jnp.inf, s)
    return jnp.concatenate(tops, axis=0), jnp.concatenate(where, axis=0), kept


def _kth_largest(c, order, kth):
    m = first = None
    for _ in range(kth):
        m = jnp.max(c, axis=0, keepdims=True)
        first = jnp.min(jnp.where(c == m, order, jnp.inf), axis=0, keepdims=True)
        c = jnp.where(order == first, -jnp.inf, c)
    return m, first


STAT_THR, STAT_LOGNORM, STAT_SPARE, STAT_S1, STAT_I, STAT_S2, STAT_J, STAT_TIED = range(8)


def _peer_gate_kernel(ht_ref, wq_ref, sk_ref, s1_ref, s2_ref, st_ref):
    qt = jnp.dot(wq_ref[0], ht_ref[...], preferred_element_type=F32)
    half = qt.shape[0] // 2
    sc = [jnp.dot(sk_ref[0, p], qt[p * half:(p + 1) * half].astype(BF16), preferred_element_type=F32)
          for p in range(2)]
    t1, i1, s1m = _top16_rows(sc[0])
    t2, i2, s2m = _top16_rows(sc[1])
    tn = t1.shape[1]
    pairs = [(a, b) for a in range(PEER_TOPK) for b in range(PEER_TOPK) if (a + 1) * (b + 1) <= PEER_TOPK]
    pad = -len(pairs) % SUBLANES
    cand = jnp.concatenate([t1[a:a + 1] + t2[b:b + 1] for a, b in pairs] + [jnp.full((pad, tn), -jnp.inf, F32)], axis=0)
    flat = jnp.concatenate([jnp.full((1, tn), float(a * PEER_TOPK + b), F32) for a, b in pairs]
                           + [jnp.full((pad, tn), float(PEER_TOPK * PEER_TOPK), F32)], axis=0)
    thr, last = _kth_largest(cand, flat, PEER_TOPK)
    kept = jnp.logical_or(cand > thr, jnp.logical_and(cand == thr, flat <= last))
    top = t1[0:1] + t2[0:1]
    zsum = jnp.sum(jnp.where(kept, jnp.exp(cand - top), 0.0), axis=0, keepdims=True)
    rank = lax.broadcasted_iota(jnp.int32, t1.shape, 0).astype(F32)
    a_last = jnp.floor(last * (1.0 / PEER_TOPK))
    b_last = last - a_last * PEER_TOPK
    pick = lambda table, r: jnp.sum(jnp.where(rank == r, table, 0.0), axis=0, keepdims=True)
    grid = (t1[:, None, :] + t2[None, :, :]).reshape(PEER_TOPK * PEER_TOPK, tn)
    tied = (jnp.sum(jnp.where(grid == thr, 1.0, 0.0), axis=0, keepdims=True)
            - jnp.sum(jnp.where(jnp.logical_and(cand == thr, kept), 1.0, 0.0), axis=0, keepdims=True))
    s1_ref[0] = s1m
    s2_ref[0] = s2m
    st_ref[0] = jnp.concatenate([thr, top + jnp.log(zsum), jnp.zeros_like(thr), pick(t1, a_last), pick(i1, a_last),
                                 pick(t2, b_last), pick(i2, b_last), tied], axis=0)


def _peer_gate(ht, wq_t, subkeys, tn):
    d, n = ht.shape
    hp, _, nk, half = subkeys.shape
    arr = jax.ShapeDtypeStruct((hp, nk, n), F32)
    blk = pl.BlockSpec((1, nk, tn), lambda i, hh: (hh, 0, i))
    return pl.pallas_call(
        _peer_gate_kernel,
        out_shape=[arr, arr, jax.ShapeDtypeStruct((hp, SUBLANES, n), F32)],
        grid=(n // tn, hp),
        in_specs=[
            pl.BlockSpec((d, tn), lambda i, hh: (0, i)),
            pl.BlockSpec((1, 2 * half, d), lambda i, hh: (hh, 0, 0)),
            pl.BlockSpec((1, 2, nk, half), lambda i, hh: (hh, 0, 0, 0)),
        ],
        out_specs=[blk, blk, pl.BlockSpec((1, SUBLANES, tn), lambda i, hh: (hh, 0, i))],
        compiler_params=_params("parallel", "arbitrary"),
    )(ht, wq_t, subkeys)


def _gelu_exact(x):
    return 0.5 * x * (1.0 + lax.erf(x * (2.0 ** -0.5)))


def _peer_dense_kernel(tie_ref, ht_ref, u_ref, vt_ref, s1_ref, s2_ref, st_ref, o_ref, coef_ref,
                       *, rows_per_step, lane_tile):
    e = pl.program_id(1)
    hp, nk, tn = s2_ref.shape

    @pl.when(e == 0)
    def _():
        o_ref[...] = jnp.zeros_like(o_ref)

    per_group = SUBLANES // rows_per_step
    base = pl.multiple_of((e // per_group) * SUBLANES, SUBLANES)
    sub = e % per_group

    def fill_coef(break_ties):
        for lt in range(tn // lane_tile):
            ls = slice(lt * lane_tile, (lt + 1) * lane_tile)
            act = _gelu_exact(jnp.dot(u_ref[...], ht_ref[:, ls], preferred_element_type=F32))
            for ii in range(rows_per_step):
                gate = jnp.zeros((nk, lane_tile), F32)
                for hh in range(hp):
                    st = st_ref[hh, :, ls]
                    row = lambda r: st[r:r + 1]
                    grp = s1_ref[hh, pl.ds(base, SUBLANES), ls]
                    s1 = grp[ii:ii + 1]
                    for o in range(1, per_group):
                        s1 = jnp.where(sub == o, grp[o * rows_per_step + ii:o * rows_per_step + ii + 1], s1)
                    s2 = s2_ref[hh, :, ls]
                    val = s1 + s2
                    if break_ties:
                        i = (e * rows_per_step + ii).astype(F32)
                        j = lax.broadcasted_iota(jnp.int32, s2.shape, 0).astype(F32)
                        same_i = i == row(STAT_I)
                        earlier = jnp.logical_or(s1 > row(STAT_S1), jnp.logical_and(s1 == row(STAT_S1), i < row(STAT_I)))
                        within = jnp.logical_or(s2 > row(STAT_S2), jnp.logical_and(s2 == row(STAT_S2), j <= row(STAT_J)))
                        tie_ok = jnp.logical_or(earlier, jnp.logical_and(same_i, within))
                        keep = jnp.logical_or(val > row(STAT_THR), jnp.logical_and(val == row(STAT_THR), tie_ok))
                    else:
                        keep = val >= row(STAT_THR)
                    weight = jnp.exp(s2 - (row(STAT_LOGNORM) - s1))
                    gate = gate + jnp.where(keep, weight, 0.0)
                coef_ref[ii * nk:(ii + 1) * nk, ls] = (gate * act[ii * nk:(ii + 1) * nk]).astype(BF16)

    tied = tie_ref[pl.program_id(0)] != 0

    @pl.when(jnp.logical_not(tied))
    def _():
        fill_coef(False)
        o_ref[...] += jnp.dot(vt_ref[0], coef_ref[...], preferred_element_type=F32)

    @pl.when(tied)
    def _():
        fill_coef(True)
        o_ref[...] += jnp.dot(vt_ref[0], coef_ref[...], preferred_element_type=F32)


def _peer_dense(ht, u, vt, s1, s2, stats, tn, n_valid):
    d, n = ht.shape
    hp, nk, _ = s1.shape
    ne = u.shape[0]
    te = vt.shape[2]
    once = dict(pipeline_mode=pl.Buffered(1))
    full = pl.BlockSpec((hp, nk, tn), lambda i, e, tie: (0, 0, i), **once)
    tied = jnp.where(jnp.arange(n) < n_valid, stats[:, STAT_TIED, :], 0.0)
    tie_flags = (jnp.max(tied.reshape(hp, n // tn, tn), axis=(0, 2)) > 0).astype(jnp.int32)
    return pl.pallas_call(
        functools.partial(_peer_dense_kernel, rows_per_step=te // nk, lane_tile=_pick(tn, (256, 128))),
        out_shape=jax.ShapeDtypeStruct((d, n), F32),
        grid_spec=pltpu.PrefetchScalarGridSpec(
            num_scalar_prefetch=1,
            grid=(n // tn, ne // te),
            in_specs=[
                pl.BlockSpec((d, tn), lambda i, e, tie: (0, i), **once),
                pl.BlockSpec((te, d), lambda i, e, tie: (e, 0)),
                pl.BlockSpec((1, d, te), lambda i, e, tie: (e, 0, 0)),
                full, full,
                pl.BlockSpec((hp, SUBLANES, tn), lambda i, e, tie: (0, 0, i), **once),
            ],
            out_specs=pl.BlockSpec((d, tn), lambda i, e, tie: (0, i)),
            scratch_shapes=[pltpu.VMEM((te, tn), BF16)],
        ),
        compiler_params=pltpu.CompilerParams(dimension_semantics=("parallel", "arbitrary"),
                                             vmem_limit_bytes=PEER_VMEM_LIMIT_BYTES),
    )(tie_flags, ht, u, vt, s1, s2, stats)


def _residual_t_kernel(x_ref, yt_ref, o_ref):
    o_ref[...] = x_ref[...] + yt_ref[...].T


def _add_transposed(x, yt, tm):
    n, d = x.shape
    return pl.pallas_call(
        _residual_t_kernel,
        out_shape=jax.ShapeDtypeStruct((n, d), F32),
        grid=(n // tm,),
        in_specs=[pl.BlockSpec((tm, d), lambda i: (i, 0)), pl.BlockSpec((d, tm), lambda i: (0, i))],
        out_specs=pl.BlockSpec((tm, d), lambda i: (i, 0)),
        compiler_params=_params("parallel"),
    )(x, yt)


def kernel(x_prompt, x_sample, cache_fox_k, cache_fox_v, cache_fox_logf, cache_sb_k, cache_sb_v, state_rwkv, state_rwkv_shift, page_table, norm_mix, w_in, b_forget, fox_q_norm, fox_k_norm, sb_q_norm, sb_k_norm, rwkv_mu, rwkv_w0, rwkv_w2, rwkv_a0, rwkv_a2, rwkv_g2, rwkv_k_k, rwkv_k_a, rwkv_r_k, rwkv_ln_w, rwkv_ln_b, w_out, norm_ffn, peer_wq, peer_subkeys, peer_u, peer_v):
    b, t, d = x_prompt.shape
    bs = x_sample.shape[0]
    depth = w_in.shape[0]
    hq = d // HEAD_DIM
    hkv = hq // GQA
    dkv = hkv * HEAD_DIM
    rd, ri, rg = rwkv_w2.shape[1], rwkv_a2.shape[1], rwkv_g2.shape[1]
    n_prompt = b * t
    tr = _pick(t, (256, 128))
    n = _round_up(n_prompt + bs, tr)
    tm = _pick(n, (768, 512, 384, 256, 128))
    tm_in = _pick(n, (1408, 768, 512, 384, 256, 128))
    tn_in = 512 if d % 2048 == 0 else LANES
    lay = _Layout(d, hq, rd, ri, rg, tn_in)
    zoff = {k: lay.off[k] - lay.z_lo for k in ("r", "k", "v", "zw", "za", "zg")}
    rdp, rip, rgp = _round_up(rd, LANES), _round_up(ri, LANES), _round_up(rg, LANES)
    tn_peer = _pick(n, (768, 256, 128))
    te_peer = 4 * peer_subkeys.shape[3]

    def tile_heads(g, count):
        return jnp.tile(g, count)

    def pad_rows(a, rows):
        return jnp.concatenate([a, jnp.zeros((rows - a.shape[0],) + a.shape[1:], a.dtype)], axis=0)

    def with_tail(full, sample_rows):
        return full.at[n_prompt:].set(pad_rows(sample_rows, n - n_prompt))

    def one_row(a):
        return a.reshape(bs, 1, d)

    def zr_original_order(rows):
        return jnp.concatenate([rows[..., zoff[k]:zoff[k] + w] for k, w in
                                (("r", d), ("k", d), ("v", d), ("zw", rd), ("za", ri), ("zg", rg))], axis=-1)

    def to_z_layout(a):
        parts, o, src = [], 0, 0
        for k, w in (("r", d), ("k", d), ("v", d), ("zw", rd), ("za", ri), ("zg", rg)):
            if zoff[k] > o:
                parts.append(jnp.zeros(a.shape[:-1] + (zoff[k] - o,), a.dtype))
            parts.append(a[..., src:src + w])
            o, src = zoff[k] + w, src + w
        if lay.z_width > o:
            parts.append(jnp.zeros(a.shape[:-1] + (lay.z_width - o,), a.dtype))
        return jnp.concatenate(parts, axis=-1)

    x = pad_rows(jnp.concatenate([x_prompt.reshape(n_prompt, d), x_sample.reshape(bs, d)], axis=0), n)
    n_pool, page = cache_fox_logf.shape[1], cache_fox_logf.shape[2]
    lf_cache = jnp.swapaxes(cache_fox_logf, 2, 3).reshape(depth * n_pool, hq, page)
    lf_cache = _pad_heads(lf_cache, hkv).reshape(depth, n_pool, hkv * DEC_ROWS, page)

    prompt_states, sample_states = [], []
    for l in range(depth):
        h = _rmsnorm(x, norm_mix[l], tm)
        w_l = lay.scatter_cols(w_in[l], lay.ncol).astype(BF16)
        colp = jnp.zeros((lay.n_in,), F32)
        for name, g, cnt in (("fq", fox_q_norm[l], hq), ("fk", fox_k_norm[l], hkv), ("sq", sb_q_norm[l], hq),
                             ("sk", sb_k_norm[l], hkv)):
            s0, w0 = lay.src[name]
            colp = colp.at[s0:s0 + w0].set(tile_heads(g, cnt))
        s0, w0 = lay.src["ff"]
        colp = colp.at[s0:s0 + w0].set(b_forget[l])
        colp = lay.scatter_cols(colp.reshape(1, -1), lay.ncol)
        p = _inproj(h, w_l, colp, jnp.asarray(lay.modes), tm_in, tn_in)

        def cols(name, width, rows=slice(None)):
            return p[rows, lay.off[name]:lay.off[name] + width]

        c = _cumsum_time(p, b, t, lay.off["ff"] // LANES)[:, :hq].reshape(b, t, hq)
        tq = _pick(t, FOX_Q_TILES)
        c_t = jnp.swapaxes(c, 1, 2)
        cq = c_t.reshape(b, hq, t, 1)
        ck = c_t.reshape(b, hq, t // tq, 1, tq)
        o_fox_p = _prompt_attention(p, lay, b, t, d, cq, ck)
        o_sb_p = _prompt_attention(p, lay, b, t, d)

        srow = slice(n_prompt, n_prompt + bs)
        scale = HEAD_DIM ** -0.5
        fq_s = _pad_heads((cols("fq", d, srow) * scale).reshape(bs, hq, HEAD_DIM), hkv)
        sq_s = _pad_heads((cols("sq", d, srow) * scale).reshape(bs, hq, HEAD_DIM), hkv)
        fk_s = cols("fk", dkv, srow).reshape(bs, hkv, HEAD_DIM)
        fv_s = cols("fv", dkv, srow).reshape(bs, hkv, HEAD_DIM)
        lf_s = cols("ff", hq, srow)
        o_fox_s = _decode_attention(l, page_table, fq_s, cache_fox_k, cache_fox_v, lf_cache, fk_s, fv_s,
                                    _pad_heads(lf_s.reshape(bs, hq, 1), hkv))
        o_sb_s = _decode_attention(l, page_table, sq_s, cache_sb_k, cache_sb_v)
        o_fox = with_tail(o_fox_p, _unpad_heads(o_fox_s, hkv))
        o_sb = with_tail(o_sb_p, _unpad_heads(o_sb_s, hkv))

        state_pad = pad_rows(to_z_layout(state_rwkv_shift[l]), tr)
        zs = _token_shift(p, lay, state_pad, to_z_layout(rwkv_mu[l]).reshape(1, -1), t, n_prompt, tr)
        pad_k = lambda w2, rp: pad_rows(w2, rp).astype(BF16)
        vec = lambda a: a.reshape(1, d)
        dec, cl, kmod, na, nb, gg = _rwkv_prep(zs, zoff, pad_k(rwkv_w2[l], rdp), pad_k(rwkv_a2[l], rip),
                                               pad_k(rwkv_g2[l], rgp), vec(rwkv_w0[l]), vec(rwkv_a0[l]),
                                               vec(rwkv_k_k[l]), vec(rwkv_k_a[l]), d, tr)
        y_p, s_p = _rwkv_scan_prompt(zs, zoff["r"] // LANES, zoff["v"] // LANES, cl, kmod, na, nb, b, t, d)
        one = lambda a: a[srow].reshape(bs, 1, d)
        zs_s = zs[srow]
        y_s, s_s = _rwkv_scan_sample(one_row(zs_s[:, zoff["r"]:zoff["r"] + d]), one(dec), one(kmod),
                                     one_row(zs_s[:, zoff["v"]:zoff["v"] + d]), one(na), one(nb), state_rwkv[l])
        y = with_tail(y_p, y_s.reshape(bs, d))
        merged = _rwkv_post_merge(y, zs, zoff, kmod, gg, vec(rwkv_r_k[l].reshape(-1)), vec(rwkv_ln_w[l]),
                                  vec(rwkv_ln_b[l]), p, lay, o_fox, o_sb, d, tr)
        x = _outproj(merged, w_out[l].astype(BF16), x, tm)

        h2t = _rmsnorm(x, norm_ffn[l], tm, transposed=True)
        wq_t = peer_wq[l].T.reshape(-1, peer_wq.shape[-1] // peer_subkeys.shape[1], d).astype(BF16)
        s1, s2, stats = _peer_gate(h2t, wq_t, peer_subkeys[l].astype(BF16), tn_peer)
        vt = jnp.swapaxes(peer_v[l].reshape(-1, te_peer, d), 1, 2).astype(BF16)
        yt = _peer_dense(h2t, peer_u[l].astype(BF16), vt, s1, s2, stats, tn_peer, n_prompt + bs)
        x = _add_transposed(x, yt, tr)

        last = jnp.arange(b) * t + (t - 1)
        zr_window = lambda rows: zr_original_order(p[rows][:, lay.z_lo:lay.z_lo + lay.z_width])
        prompt_states.append((
            cols("fk", dkv, slice(0, n_prompt)).reshape(b, t, hkv, HEAD_DIM),
            cols("fv", dkv, slice(0, n_prompt)).reshape(b, t, hkv, HEAD_DIM),
            cols("ff", hq, slice(0, n_prompt)).reshape(b, t, hq),
            cols("sk", dkv, slice(0, n_prompt)).reshape(b, t, hkv, HEAD_DIM),
            cols("sv", dkv, slice(0, n_prompt)).reshape(b, t, hkv, HEAD_DIM),
            s_p, zr_window(last)))
        sample_states.append((
            fk_s.reshape(bs, 1, hkv, HEAD_DIM), fv_s.reshape(bs, 1, hkv, HEAD_DIM), lf_s.reshape(bs, 1, hq),
            cols("sk", dkv, srow).reshape(bs, 1, hkv, HEAD_DIM), cols("sv", dkv, srow).reshape(bs, 1, hkv, HEAD_DIM),
            s_s, zr_window(srow)))

    stack = lambda states, i: jnp.stack([s[i] for s in states])
    y_prompt = x[:n_prompt].reshape(b, t, d)
    y_sample = x[n_prompt:n_prompt + bs].reshape(bs, 1, d)
    return ((y_prompt, y_sample) + tuple(stack(prompt_states, i) for i in range(7))
            + tuple(stack(sample_states, i) for i in range(7)))
```
